```python
import math
import jax, jax.numpy as jnp
from jax import lax
import numpy as np

D_MODEL = 4096
BATCH = 2
SEQ = 4096
DEPTH = 1

N_META = 16
CHUNK = 128
LEAD = CHUNK
LEAD_PAD = CHUNK - N_META
Q_BLOCK = 128

MLA_HEADS = (D_MODEL // 2) // 128
QK_NOPE = 128
QK_ROPE = 64
V_HEAD = 128
Q_LORA = D_MODEL // 4
KV_LORA = 512
ROPE_THETA = 10000.0

D_INNER = D_MODEL // 2
SSD_HEAD_DIM = 64
SSD_HEADS = D_INNER // SSD_HEAD_DIM
SSD_GROUPS = 4
HEADS_PER_GROUP = SSD_HEADS // SSD_GROUPS
SSD_STATE = 128
CONV_W = 4
CONV_CH = D_INNER + 2 * SSD_GROUPS * SSD_STATE

MIX_WIDTH = MLA_HEADS * V_HEAD + D_INNER
IN_SIZES = (Q_LORA, KV_LORA, QK_ROPE, D_INNER, CONV_CH, SSD_HEADS)
IN_COLS = sum(IN_SIZES)
SPLIT_OFFSETS = tuple(int(v) for v in np.cumsum(IN_SIZES)[:-1])

PEER_HEADS = 8
PEER_TOPK = 16
N_KEYS = 128
N_EXPERTS = N_KEYS * N_KEYS
D_KEY = 256
HALF_KEY = D_KEY // 2
PEER_BLOCK = 64

EPS = 1e-6
NEG_INF = -1e30

kernel_name = "hymba_mla_ssd_peer_layer"


def rmsnorm(x, g):
    xf = x.astype(jnp.float32)
    y = xf * lax.rsqrt(jnp.mean(xf * xf, axis=-1, keepdims=True) + EPS)
    return y.astype(x.dtype) * g


def apply_rope(x, cos, sin):
    x1, x2 = jnp.split(x, 2, axis=-1)
    c = cos[None, :, None, :]
    s = sin[None, :, None, :]
    return jnp.concatenate([x1 * c - x2 * s, x1 * s + x2 * c], axis=-1)


def mla_mixer(c_q, c_kv, k_rope, q_norm, w_uq, kv_norm, w_ukv, cos, sin, valid):
    bsz, t, _ = c_q.shape
    q = (rmsnorm(c_q, q_norm) @ w_uq).reshape(bsz, t, MLA_HEADS, QK_NOPE + QK_ROPE)
    q_nope, q_pe = q[..., :QK_NOPE], q[..., QK_NOPE:]
    kv = (rmsnorm(c_kv, kv_norm) @ w_ukv).reshape(bsz, t, MLA_HEADS, QK_NOPE + V_HEAD)
    k_nope, v = kv[..., :QK_NOPE], kv[..., QK_NOPE:]
    q_pe = apply_rope(q_pe, cos, sin)
    k_pe = apply_rope(k_rope[:, :, None, :], cos, sin)
    q = jnp.concatenate([q_nope, q_pe], axis=-1)
    k = jnp.concatenate([k_nope, jnp.broadcast_to(k_pe, (bsz, t, MLA_HEADS, QK_ROPE))], axis=-1)
    scale = (QK_NOPE + QK_ROPE) ** -0.5
    n_blk = t // Q_BLOCK
    q_blocks = q.reshape(bsz, n_blk, Q_BLOCK, MLA_HEADS, QK_NOPE + QK_ROPE).transpose(1, 0, 2, 3, 4)
    q_pos = jnp.arange(t).reshape(n_blk, Q_BLOCK)
    k_pos = jnp.arange(t)

    def attn_block(args):
        qb, qp = args
        s = jnp.einsum('bqhd,bkhd->bhqk', qb, k).astype(jnp.float32) * scale
        mask = (k_pos[None, :] <= qp[:, None]) & valid[None, :]
        s = jnp.where(mask[None, None], s, NEG_INF)
        p = jax.nn.softmax(s, axis=-1).astype(v.dtype)
        return jnp.einsum('bhqk,bkhd->bqhd', p, v)

    out = lax.map(attn_block, (q_blocks, q_pos))
    return out.transpose(1, 0, 2, 3, 4).reshape(bsz, t, MLA_HEADS * V_HEAD)


def ssd_mixer(z, xbc, dt_raw, conv_w, conv_b, dt_bias, a_log, d_skip, norm_g, valid):
    bsz, t, _ = xbc.shape
    n_c = t // CHUNK
    xbc = lax.conv_general_dilated(xbc, conv_w[:, None, :], window_strides=(1,), padding=[(CONV_W - 1, 0)],
                                   dimension_numbers=('NWC', 'WIO', 'NWC'), feature_group_count=CONV_CH) + conv_b
    xbc = jax.nn.silu(xbc)
    xs = xbc[..., :D_INNER].reshape(bsz, t, SSD_GROUPS, HEADS_PER_GROUP, SSD_HEAD_DIM)
    b_in = xbc[..., D_INNER:D_INNER + SSD_GROUPS * SSD_STATE].reshape(bsz, n_c, CHUNK, SSD_GROUPS, SSD_STATE)
    c_in = xbc[..., D_INNER + SSD_GROUPS * SSD_STATE:].reshape(bsz, n_c, CHUNK, SSD_GROUPS, SSD_STATE)

    dt = jax.nn.softplus((dt_raw + dt_bias).astype(jnp.float32))
    dt = jnp.where(valid[None, :, None], dt, 0.0)
    a = -jnp.exp(a_log.astype(jnp.float32))
    dt_g = dt.reshape(bsz, t, SSD_GROUPS, HEADS_PER_GROUP)
    x_dt = (xs * dt_g[..., None]).reshape(bsz, n_c, CHUNK, SSD_GROUPS, HEADS_PER_GROUP, SSD_HEAD_DIM)
    da = (dt * a).reshape(bsz, n_c, CHUNK, SSD_GROUPS, HEADS_PER_GROUP).transpose(0, 3, 4, 1, 2)
    cs = jnp.cumsum(da, axis=-1)

    causal = jnp.tril(jnp.ones((CHUNK, CHUNK), dtype=bool))
    seg = cs[..., :, None] - cs[..., None, :]
    l_mat = jnp.exp(jnp.where(causal, seg, -jnp.inf))
    cb = jnp.einsum('bclgn,bcsgn->bgcls', c_in, b_in)
    y_diag = jnp.einsum('bgcls,bgrcls,bcsgrp->bclgrp', cb, l_mat, x_dt)

    decay_states = jnp.exp(cs[..., -1:] - cs)
    states = jnp.einsum('bcsgn,bgrcs,bcsgrp->cbgrpn', b_in, decay_states, x_dt)
    chunk_decay = jnp.exp(cs[..., -1]).transpose(3, 0, 1, 2)

    def step(carry, inp):
        st, dec = inp
        return carry * dec[..., None, None] + st, carry

    init = jnp.zeros((bsz, SSD_GROUPS, HEADS_PER_GROUP, SSD_HEAD_DIM, SSD_STATE), jnp.float32)
    _, prev = lax.scan(step, init, (states, chunk_decay))
    y_off = jnp.einsum('bclgn,cbgrpn,bgrcl->bclgrp', c_in, prev, jnp.exp(cs))

    y = (y_diag + y_off).reshape(bsz, t, SSD_GROUPS, HEADS_PER_GROUP, SSD_HEAD_DIM)
    y = y + xs * d_skip.reshape(SSD_GROUPS, HEADS_PER_GROUP)[..., None]
    gated = y.reshape(bsz, t, SSD_GROUPS, D_INNER // SSD_GROUPS) * jax.nn.silu(
        z.astype(jnp.float32)).reshape(bsz, t, SSD_GROUPS, D_INNER // SSD_GROUPS)
    gated = gated * lax.rsqrt(jnp.mean(gated * gated, axis=-1, keepdims=True) + EPS)
    return gated.reshape(bsz, t, D_INNER).astype(z.dtype) * norm_g


def peer_ffn(x, w_query, sub_keys, u_experts, v_experts):
    bsz, s_len, d = x.shape
    xt = x.reshape(-1, d)
    n = xt.shape[0]
    q = (xt @ w_query).reshape(n, PEER_HEADS, 2, HALF_KEY)
    scores = jnp.einsum('nhpk,pmk->nhpm', q, sub_keys)
    s1, i1 = lax.top_k(scores[:, :, 0], PEER_TOPK)
    s2, i2 = lax.top_k(scores[:, :, 1], PEER_TOPK)
    cand = (s1[..., :, None] + s2[..., None, :]).reshape(n, PEER_HEADS, PEER_TOPK * PEER_TOPK)
    top_s, top_c = lax.top_k(cand, PEER_TOPK)
    expert = (jnp.take_along_axis(i1, top_c // PEER_TOPK, axis=-1) * N_KEYS
              + jnp.take_along_axis(i2, top_c % PEER_TOPK, axis=-1))
    gate = jax.nn.softmax(top_s.astype(jnp.float32), axis=-1).astype(x.dtype)
    n_blk = n // PEER_BLOCK

    def block(args):
        xb, eb, gb = args
        u = u_experts[eb]
        act = jax.nn.gelu(jnp.einsum('nd,nhkd->nhk', xb, u), approximate=False) * gb
        return jnp.einsum('nhk,nhkd->nd', act, v_experts[eb])

    out = lax.map(block, (xt.reshape(n_blk, PEER_BLOCK, d),
                          expert.reshape(n_blk, PEER_BLOCK, PEER_HEADS, PEER_TOPK),
                          gate.reshape(n_blk, PEER_BLOCK, PEER_HEADS, PEER_TOPK)))
    return out.reshape(bsz, s_len, d)


def setup_inputs(seed: int = 0) -> dict:
    key = jax.random.key(seed)
    ks = jax.random.split(key, 24)
    f32 = jnp.float32

    def nrm(k, shape, scale):
        return jax.random.normal(k, shape, f32) * scale

    def gain(k, shape):
        return 1.0 + 0.02 * jax.random.normal(k, shape, f32)

    dt0 = jnp.exp(jax.random.uniform(ks[10], (DEPTH, SSD_HEADS), f32, math.log(1e-3), math.log(0.1)))
    return {
        "x": jax.random.normal(ks[0], (BATCH, SEQ, D_MODEL), f32),
        "meta_tokens": nrm(ks[1], (N_META, D_MODEL), 1.0),
        "attn_norm": gain(ks[2], (DEPTH, D_MODEL)),
        "w_in": nrm(ks[3], (DEPTH, D_MODEL, IN_COLS), D_MODEL ** -0.5),
        "q_norm": gain(ks[4], (DEPTH, Q_LORA)),
        "w_uq": nrm(ks[5], (DEPTH, Q_LORA, MLA_HEADS * (QK_NOPE + QK_ROPE)), Q_LORA ** -0.5),
        "kv_norm": gain(ks[6], (DEPTH, KV_LORA)),
        "w_ukv": nrm(ks[7], (DEPTH, KV_LORA, MLA_HEADS * (QK_NOPE + V_HEAD)), KV_LORA ** -0.5),
        "conv_w": nrm(ks[8], (DEPTH, CONV_W, CONV_CH), CONV_W ** -0.5),
        "conv_b": nrm(ks[9], (DEPTH, CONV_CH), 0.02),
        "dt_bias": dt0 + jnp.log(-jnp.expm1(-dt0)),
        "a_log": jnp.log(jax.random.uniform(ks[11], (DEPTH, SSD_HEADS), f32, 1.0, 16.0)),
        "d_skip": gain(ks[12], (DEPTH, SSD_HEADS)),
        "ssd_norm": gain(ks[13], (DEPTH, D_INNER)),
        "w_out": nrm(ks[14], (DEPTH, MIX_WIDTH, D_MODEL), MIX_WIDTH ** -0.5),
        "ffn_norm": gain(ks[15], (DEPTH, D_MODEL)),
        "w_query": nrm(ks[16], (DEPTH, D_MODEL, PEER_HEADS * D_KEY), D_MODEL ** -0.5),
        "sub_keys": nrm(ks[17], (DEPTH, 2, N_KEYS, HALF_KEY), HALF_KEY ** -0.5),
        "u_experts": nrm(ks[18], (DEPTH, N_EXPERTS, D_MODEL), D_MODEL ** -0.5),
        "v_experts": nrm(ks[19], (DEPTH, N_EXPERTS, D_MODEL), PEER_HEADS ** -0.5),
        "final_norm": gain(ks[20], (D_MODEL,)),
    }


def reference(x, meta_tokens, attn_norm, w_in, q_norm, w_uq, kv_norm, w_ukv, conv_w, conv_b, dt_bias, a_log,
              d_skip, ssd_norm, w_out, ffn_norm, w_query, sub_keys, u_experts, v_experts, final_norm):
    bsz, s_len, d = x.shape
    t = s_len + LEAD
    h = jnp.concatenate([jnp.zeros((bsz, LEAD_PAD, d), x.dtype),
                         jnp.broadcast_to(meta_tokens.astype(x.dtype)[None], (bsz, N_META, d)),
                         x], axis=1)
    idx = jnp.arange(t)
    valid = idx >= LEAD_PAD
    pos = (idx - LEAD_PAD).astype(jnp.float32)
    inv_freq = ROPE_THETA ** (-jnp.arange(0, QK_ROPE, 2, dtype=jnp.float32) / QK_ROPE)
    ang = pos[:, None] * inv_freq[None, :]
    cos = jnp.cos(ang).astype(x.dtype)
    sin = jnp.sin(ang).astype(x.dtype)

    for i in range(DEPTH):
        hn = rmsnorm(h, attn_norm[i])
        proj = hn @ w_in[i]
        c_q, c_kv, k_rope, z, xbc, dt_raw = jnp.split(proj, SPLIT_OFFSETS, axis=-1)
        a_out = mla_mixer(c_q, c_kv, k_rope, q_norm[i], w_uq[i], kv_norm[i], w_ukv[i], cos, sin, valid)
        s_out = ssd_mixer(z, xbc, dt_raw, conv_w[i], conv_b[i], dt_bias[i], a_log[i], d_skip[i], ssd_norm[i], valid)
        h = h + jnp.concatenate([a_out, s_out], axis=-1) @ w_out[i]
        h = jnp.where(valid[None, :, None], h, jnp.zeros((), h.dtype))
        if i == DEPTH - 1:
            h = h[:, LEAD:]
        h = h + peer_ffn(rmsnorm(h, ffn_norm[i]), w_query[i], sub_keys[i], u_experts[i], v_experts[i])
    return rmsnorm(h, final_norm)
```

```python
import functools

import jax
import jax.numpy as jnp
import numpy as np
from jax import lax
from jax.experimental import pallas as pl
from jax.experimental.pallas import tpu as pltpu

f32 = jnp.float32
bf16 = jnp.bfloat16

D_MODEL = 4096
N_META = 16
CHUNK = 128
LEAD = CHUNK
LEAD_PAD = CHUNK - N_META
MLA_HEADS = 16
QK_NOPE = 128
QK_ROPE = 64
V_HEAD = 128
Q_LORA = 1024
KV_LORA = 512
ROPE_THETA = 10000.0
D_INNER = 2048
SSD_HEAD_DIM = 64
SSD_HEADS = 32
SSD_GROUPS = 4
SSD_STATE = 128
CONV_W = 4
PEER_HEADS = 8
PEER_TOPK = 16
N_KEYS = 128
HALF_KEY = 128
EPS = 1e-6
NEG_INF = -1e30

LANE = 128
QK_SLAB = 2 * LANE
GROUP_W = D_INNER // SSD_GROUPS

P_Z = 0
P_XS = 2048
P_CQ = 4096
P_B = 5120
P_C = 5632
P_CKV = 6144
P_KR = 6656
P_DT = 6784
NP = 6912

VMEM_LIMIT = 56 * 1024 * 1024


def _cparams(sem, vmem=VMEM_LIMIT):
    return pltpu.CompilerParams(dimension_semantics=sem, vmem_limit_bytes=vmem)


def _dot(a, b):
    return jnp.dot(a, b, preferred_element_type=f32)


def _dot_nt(a, b):
    return lax.dot_general(a, b, (((1,), (1,)), ((), ())), preferred_element_type=f32)


def _rms(x, g):
    return x * lax.rsqrt(jnp.mean(x * x, axis=-1, keepdims=True) + EPS) * g


def _inproj_kernel(x_ref, g_ref, w_ref, o_ref, xn_ref, *, sub):
    @pl.when(pl.program_id(1) == 0)
    def _():
        def body(r, c):
            rows = pl.ds(pl.multiple_of(r * sub, sub), sub)
            xn_ref[rows, :] = _rms(x_ref[rows, :], g_ref[...]).astype(bf16)
            return c
        lax.fori_loop(0, x_ref.shape[0] // sub, body, 0)

    o_ref[...] = _dot(xn_ref[...], w_ref[...])


def _inproj(x2d, g, w, *, tm, tn=768):
    m = x2d.shape[0]
    assert m % tm == 0 and NP % tn == 0
    sub = min(64, tm)
    return pl.pallas_call(
        functools.partial(_inproj_kernel, sub=sub),
        grid=(m // tm, NP // tn),
        in_specs=[pl.BlockSpec((tm, D_MODEL), lambda i, j: (i, 0)),
                  pl.BlockSpec((1, D_MODEL), lambda i, j: (0, 0)),
                  pl.BlockSpec((D_MODEL, tn), lambda i, j: (0, j))],
        out_specs=pl.BlockSpec((tm, tn), lambda i, j: (i, j)),
        out_shape=jax.ShapeDtypeStruct((m, NP), f32),
        scratch_shapes=[pltpu.VMEM((tm, D_MODEL), bf16)],
        compiler_params=_cparams(("parallel", "arbitrary")),
        name="inproj",
    )(x2d, g, w)


def _mla_up_kernel(cq_ref, ckv_ref, kr_ref, cc_ref, ss_ref, qn_ref, kvn_ref, wq_ref, wk_ref, wv_ref,
                   q_ref, k_ref, v_ref, *, scale):
    cc = cc_ref[...]
    ss = ss_ref[...]

    def rope(t):
        return t * cc + pltpu.roll(t, 64, 1) * ss

    cqn = _rms(cq_ref[...], qn_ref[...]).astype(bf16)
    q = _dot(cqn, wq_ref[...])
    for h in range(MLA_HEADS):
        lo = h * QK_SLAB
        q_ref[:, lo:lo + LANE] = (q[:, lo:lo + LANE] * scale).astype(bf16)
        q_ref[:, lo + LANE:lo + QK_SLAB] = (rope(q[:, lo + LANE:lo + QK_SLAB]) * scale).astype(bf16)

    kvn = _rms(ckv_ref[...], kvn_ref[...]).astype(bf16)
    kn = _dot(kvn, wk_ref[...])
    kpe = rope(kr_ref[...]).astype(bf16)
    for h in range(MLA_HEADS):
        lo = h * QK_SLAB
        k_ref[:, lo:lo + LANE] = kn[:, h * LANE:(h + 1) * LANE].astype(bf16)
        k_ref[:, lo + LANE:lo + QK_SLAB] = kpe
    v_ref[...] = _dot(kvn, wv_ref[...]).astype(bf16)


def _mla_up(proj, cc, ss, qn, kvn, wq, wk, wv, *, tm, n_pos_blocks):
    m = proj.shape[0]
    assert m % tm == 0
    scale = float((QK_NOPE + QK_ROPE) ** -0.5)
    hq = MLA_HEADS * QK_SLAB
    hv = MLA_HEADS * V_HEAD
    const = lambda i: (0, 0)
    return pl.pallas_call(
        functools.partial(_mla_up_kernel, scale=scale),
        grid=(m // tm,),
        in_specs=[pl.BlockSpec((tm, Q_LORA), lambda i: (i, P_CQ // Q_LORA)),
                  pl.BlockSpec((tm, KV_LORA), lambda i: (i, P_CKV // KV_LORA)),
                  pl.BlockSpec((tm, LANE), lambda i: (i, P_KR // LANE)),
                  pl.BlockSpec((tm, LANE), lambda i: (i % n_pos_blocks, 0)),
                  pl.BlockSpec((tm, LANE), lambda i: (i % n_pos_blocks, 0)),
                  pl.BlockSpec((1, Q_LORA), const),
                  pl.BlockSpec((1, KV_LORA), const),
                  pl.BlockSpec((Q_LORA, hq), const),
                  pl.BlockSpec((KV_LORA, hv), const),
                  pl.BlockSpec((KV_LORA, hv), const)],
        out_specs=[pl.BlockSpec((tm, hq), lambda i: (i, 0)),
                   pl.BlockSpec((tm, hq), lambda i: (i, 0)),
                   pl.BlockSpec((tm, hv), lambda i: (i, 0))],
        out_shape=[jax.ShapeDtypeStruct((m, hq), bf16),
                   jax.ShapeDtypeStruct((m, hq), bf16),
                   jax.ShapeDtypeStruct((m, hv), bf16)],
        compiler_params=_cparams(("parallel",)),
        name="mla_up",
    )(proj, proj, proj, cc, ss, qn, kvn, wq, wk, wv)


def _attn_kernel(q_ref, k_ref, v_ref, kl_ref, vl_ref, o_ref, *, tq):
    i = pl.program_id(2)
    q = q_ref[...]

    s = _dot_nt(q, kl_ref[...])
    col = lax.broadcasted_iota(jnp.int32, s.shape, 1)
    s = jnp.where(col >= LEAD_PAD, s, NEG_INF)
    m = jnp.max(s, axis=-1, keepdims=True)
    p = jnp.exp(s - m)
    l = jnp.sum(p, axis=-1, keepdims=True)
    acc = _dot(p.astype(bf16), vl_ref[...])

    def update(s, vb, carry):
        m, l, acc = carry
        m_new = jnp.maximum(m, jnp.max(s, axis=-1, keepdims=True))
        alpha = jnp.exp(m - m_new)
        p = jnp.exp(s - m_new)
        l = alpha * l + jnp.sum(p, axis=-1, keepdims=True)
        acc = alpha * acc + _dot(p.astype(bf16), vb)
        return m_new, l, acc

    def body(j, carry):
        rows = pl.ds(pl.multiple_of(j * tq, tq), tq)
        return update(_dot_nt(q, k_ref[rows, :]), v_ref[rows, :], carry)

    carry = lax.fori_loop(0, i, body, (m, l, acc))

    rows = pl.ds(pl.multiple_of(i * tq, tq), tq)
    s = _dot_nt(q, k_ref[rows, :])
    r_id = lax.broadcasted_iota(jnp.int32, s.shape, 0)
    c_id = lax.broadcasted_iota(jnp.int32, s.shape, 1)
    s = jnp.where(c_id <= r_id, s, NEG_INF)
    m, l, acc = update(s, v_ref[rows, :], carry)
    o_ref[...] = (acc / l).astype(bf16)


def _attention(q, k, v, kl, vl, *, bsz, seq, tq):
    assert seq % tq == 0
    nq = seq // tq
    return pl.pallas_call(
        functools.partial(_attn_kernel, tq=tq),
        grid=(bsz, MLA_HEADS, nq),
        in_specs=[pl.BlockSpec((tq, QK_SLAB), lambda b, h, i: (b * nq + i, h)),
                  pl.BlockSpec((seq, QK_SLAB), lambda b, h, i: (b, h)),
                  pl.BlockSpec((seq, V_HEAD), lambda b, h, i: (b, h)),
                  pl.BlockSpec((LEAD, QK_SLAB), lambda b, h, i: (0, h)),
                  pl.BlockSpec((LEAD, V_HEAD), lambda b, h, i: (0, h))],
        out_specs=pl.BlockSpec((tq, V_HEAD), lambda b, h, i: (b * nq + i, h)),
        out_shape=jax.ShapeDtypeStruct((bsz * seq, MLA_HEADS * V_HEAD), bf16),
        compiler_params=_cparams(("parallel", "parallel", "arbitrary")),
        name="mla_attn",
    )(q, k, v, kl, vl)


def _split3(v):
    b0 = v.astype(bf16).astype(f32)
    r1 = v - b0
    b1 = r1.astype(bf16).astype(f32)
    b2 = (r1 - b1).astype(bf16).astype(f32)
    return b0, b1, b2


def _ssd_kernel(zx_l, bc_l, dt_l, zx_r, bc_r, dt_r, cwx, cwbc, cbx, cbbc, dtb, alog, dskip, ng, e64, e128,
                o_ref, st_ref, px_ref, pbc_ref):
    c = pl.program_id(1)
    is_lead = c == 0

    @pl.when(is_lead)
    def _():
        st_ref[...] = jnp.zeros_like(st_ref)
        px_ref[...] = jnp.zeros_like(px_ref)
        pbc_ref[...] = jnp.zeros_like(pbc_ref)

    zx = jnp.where(is_lead, zx_l[...], zx_r[...])
    bcraw = jnp.where(is_lead, bc_l[...], bc_r[...])
    dtraw = jnp.where(is_lead, dt_l[...], dt_r[...])
    z = zx[:, :D_INNER]
    xraw = zx[:, D_INNER:]

    row1 = lax.broadcasted_iota(jnp.int32, (CHUNK, 1), 0)
    r_id = lax.broadcasted_iota(jnp.int32, (CHUNK, CHUNK), 0)
    c_id = lax.broadcasted_iota(jnp.int32, (CHUNK, CHUNK), 1)
    tril = r_id >= c_id

    def conv_silu(cur, prev_ref, w_ref, b_ref):
        prev = prev_ref[...]
        acc = cur * w_ref[CONV_W - 1:CONV_W, :] + b_ref[...]
        for k in range(1, CONV_W):
            comb = jnp.where(row1 >= CHUNK - k, prev, cur)
            acc = acc + pltpu.roll(comb, k, 0) * w_ref[CONV_W - 1 - k:CONV_W - k, :]
        prev_ref[...] = cur
        return acc * jax.nn.sigmoid(acc)

    xs = conv_silu(xraw, px_ref, cwx, cbx)
    bcc = conv_silu(bcraw, pbc_ref, cwbc, cbbc)
    b_all = bcc[:, :SSD_GROUPS * SSD_STATE]
    c_all = bcc[:, SSD_GROUPS * SSD_STATE:].astype(bf16)

    pre = dtraw + dtb[...]
    dt = jnp.maximum(pre, 0.0) + jnp.log1p(jnp.exp(-jnp.abs(pre)))
    dt = jnp.where(jnp.logical_and(is_lead, row1 < LEAD_PAD), 0.0, dt)
    da = dt * (-jnp.exp(alog[...]))
    tri = tril.astype(f32).astype(bf16)
    d0, d1, d2 = _split3(da)
    cs = _dot(tri, d0.astype(bf16)) + _dot(tri, d1.astype(bf16)) + _dot(tri, d2.astype(bf16))
    cs_t = cs.T
    ecs = jnp.exp(cs)
    dec = jnp.exp(cs[CHUNK - 1:CHUNK, :] - cs)

    def kcat(v):
        b0, b1, b2 = _split3(v)
        return jnp.where(c_id < 32, b0, jnp.where(c_id < 64, b1, jnp.where(c_id < 96, b2, 0.0))).astype(bf16)

    ex = _dot(jnp.concatenate([kcat(dt), kcat(ecs), kcat(dec)], axis=0), e64[...])
    dt_x = ex[0:CHUNK]
    ecs_x = ex[CHUNK:2 * CHUNK]
    dec_x = ex[2 * CHUNK:3 * CHUNK]
    cs_bc = _dot(kcat(cs), e128[...])

    xdt = xs * dt_x
    xdt_b = xdt.astype(bf16)
    xd_b = (xdt * dec_x).astype(bf16)

    ys = []
    for g in range(SSD_GROUPS):
        gl = slice(g * GROUP_W, (g + 1) * GROUP_W)
        b_g = b_all[:, g * SSD_STATE:(g + 1) * SSD_STATE]
        c_g = c_all[:, g * SSD_STATE:(g + 1) * SSD_STATE]
        cb = _dot_nt(c_g, b_g.astype(bf16))
        st_g = st_ref[:, gl]
        y_g = _dot(c_g, st_g.astype(bf16)) * ecs_x[:, gl]
        pieces = []
        for pr in range(GROUP_W // LANE):
            h0 = g * (GROUP_W // SSD_HEAD_DIM) + 2 * pr
            xpair = xdt_b[:, h0 * SSD_HEAD_DIM:(h0 + 2) * SSD_HEAD_DIM]
            yh = []
            for h in (h0, h0 + 1):
                seg = cs_bc[:, h * CHUNK:(h + 1) * CHUNK] - cs_t[h:h + 1, :]
                lm = jnp.where(tril, jnp.exp(jnp.minimum(seg, 0.0)), 0.0)
                yh.append(_dot((cb * lm).astype(bf16), xpair))
            pieces.append(jnp.where(c_id < SSD_HEAD_DIM, yh[0], yh[1]))
        y_g = y_g + jnp.concatenate(pieces, axis=1)
        st_ref[:, gl] = st_g * ecs_x[CHUNK - 1:CHUNK, gl] + _dot(b_g.T.astype(bf16), xd_b[:, gl])
        ys.append(y_g)

    y = jnp.concatenate(ys, axis=1) + xs * dskip[...]
    gated = y * (z * jax.nn.sigmoid(z))
    outs = []
    for g in range(SSD_GROUPS):
        gg = gated[:, g * GROUP_W:(g + 1) * GROUP_W]
        outs.append(gg * lax.rsqrt(jnp.mean(gg * gg, axis=-1, keepdims=True) + EPS))
    o_ref[...] = (jnp.concatenate(outs, axis=1) * ng[...]).astype(bf16)


def _ssd(proj_l, proj_r, cwx, cwbc, cbx, cbbc, dtb, alog, dskip, ng, e64, e128, *, bsz, seq):
    nch = seq // CHUNK
    real = lambda w: (lambda b, c: (b * nch + jnp.maximum(c - 1, 0), w))
    lead = lambda w: (lambda b, c: (0, w))
    const = lambda b, c: (0, 0)
    zx_w, bc_w, dt_w = 2 * D_INNER, 2 * SSD_GROUPS * SSD_STATE, LANE
    return pl.pallas_call(
        _ssd_kernel,
        grid=(bsz, nch + 1),
        in_specs=[pl.BlockSpec((CHUNK, zx_w), lead(P_Z // zx_w)),
                  pl.BlockSpec((CHUNK, bc_w), lead(P_B // bc_w)),
                  pl.BlockSpec((CHUNK, dt_w), lead(P_DT // dt_w)),
                  pl.BlockSpec((CHUNK, zx_w), real(P_Z // zx_w)),
                  pl.BlockSpec((CHUNK, bc_w), real(P_B // bc_w)),
                  pl.BlockSpec((CHUNK, dt_w), real(P_DT // dt_w)),
                  pl.BlockSpec((CONV_W, D_INNER), const),
                  pl.BlockSpec((CONV_W, bc_w), const),
                  pl.BlockSpec((1, D_INNER), const),
                  pl.BlockSpec((1, bc_w), const),
                  pl.BlockSpec((1, LANE), const),
                  pl.BlockSpec((1, LANE), const),
                  pl.BlockSpec((1, D_INNER), const),
                  pl.BlockSpec((1, D_INNER), const),
                  pl.BlockSpec((LANE, D_INNER), const),
                  pl.BlockSpec((LANE, SSD_HEADS * CHUNK), const)],
        out_specs=pl.BlockSpec((CHUNK, D_INNER), real(0)),
        out_shape=jax.ShapeDtypeStruct((bsz * seq, D_INNER), bf16),
        scratch_shapes=[pltpu.VMEM((SSD_STATE, D_INNER), f32),
                        pltpu.VMEM((CHUNK, D_INNER), f32),
                        pltpu.VMEM((CHUNK, bc_w), f32)],
        compiler_params=_cparams(("parallel", "arbitrary")),
        name="ssd",
    )(proj_l, proj_l, proj_l, proj_r, proj_r, proj_r, cwx, cwbc, cbx, cbbc, dtb, alog, dskip, ng, e64, e128)


def _outproj_kernel(a_ref, s_ref, wa_ref, ws_ref, x_ref, o_ref):
    o_ref[...] = _dot(a_ref[...], wa_ref[...]) + _dot(s_ref[...], ws_ref[...]) + x_ref[...]


def _outproj(a, s, w, x2d, *, tm, tn=512):
    m = a.shape[0]
    ka = a.shape[1]
    assert m % tm == 0 and D_MODEL % tn == 0
    return pl.pallas_call(
        _outproj_kernel,
        grid=(m // tm, D_MODEL // tn),
        in_specs=[pl.BlockSpec((tm, ka), lambda i, j: (i, 0)),
                  pl.BlockSpec((tm, D_INNER), lambda i, j: (i, 0)),
                  pl.BlockSpec((ka, tn), lambda i, j: (0, j)),
                  pl.BlockSpec((D_INNER, tn), lambda i, j: (1, j)),
                  pl.BlockSpec((tm, tn), lambda i, j: (i, j))],
        out_specs=pl.BlockSpec((tm, tn), lambda i, j: (i, j)),
        out_shape=jax.ShapeDtypeStruct((m, D_MODEL), f32),
        compiler_params=_cparams(("parallel", "arbitrary")),
        name="outproj",
    )(a, s, w, w, x2d)


def _extract_topk(x, k):
    rows = x.shape[0]
    idx = lax.broadcasted_iota(jnp.int32, x.shape, 0)
    vals = []
    for _ in range(k):
        m = jnp.max(x, axis=0, keepdims=True)
        first = jnp.min(jnp.where(x == m, idx, rows), axis=0, keepdims=True)
        x = jnp.where(idx == first, -jnp.inf, x)
        vals.append(m)
    return vals


def _stack_rows(vals):
    n = len(vals)
    rid = lax.broadcasted_iota(jnp.int32, (n, vals[0].shape[1]), 0)
    out = jnp.zeros((n, vals[0].shape[1]), f32)
    for r, v in enumerate(vals):
        out = jnp.where(rid == r, v, out)
    return out


def _peer_q_kernel(h_ref, g_ref, wq_ref, sk_ref, xnt_ref, s1_ref, s2_ref, st_ref, cand_ref):
    xn = _rms(h_ref[...], g_ref[...])
    xnt = xn.T.astype(bf16)
    xnt_ref[...] = xnt
    qt = _dot(wq_ref[...], xnt)
    sk1 = sk_ref[0].astype(bf16)
    sk2 = sk_ref[1].astype(bf16)
    half = PEER_TOPK // 2
    for h in range(PEER_HEADS):
        lo = h * 2 * HALF_KEY
        s1 = _dot(sk1, qt[lo:lo + HALF_KEY, :].astype(bf16))
        s2 = _dot(sk2, qt[lo + HALF_KEY:lo + 2 * HALF_KEY, :].astype(bf16))
        s1_ref[h] = s1
        s2_ref[h] = s2
        v1 = _extract_topk(s1, PEER_TOPK)
        v2 = _extract_topk(s2, PEER_TOPK)
        v2_all = _stack_rows(v2)
        cand_ref[0:PEER_TOPK, :] = v1[0] + v2_all
        for a in range(1, half):
            cand_ref[PEER_TOPK + (a - 1) * half:PEER_TOPK + a * half, :] = v1[a] + v2_all[0:half, :]
        cand_ref[PEER_TOPK + (half - 1) * half:, :] = _stack_rows(v1[half:]) + v2[0]
        top = _extract_topk(cand_ref[...], PEER_TOPK)
        zsum = jnp.zeros_like(top[0])
        for t in top:
            zsum = zsum + jnp.exp(t - top[0])
        st_ref[0, h:h + 1, :] = top[PEER_TOPK - 1]
        st_ref[1, h:h + 1, :] = v1[0]
        st_ref[2, h:h + 1, :] = v2[0]
        st_ref[3, h:h + 1, :] = 1.0 / zsum


def _peer_q(h1, g, wqt, sub_keys, *, tm):
    n = h1.shape[0]
    assert n % tm == 0
    nq = PEER_HEADS * 2 * HALF_KEY
    half = PEER_TOPK // 2
    n_cand = PEER_TOPK + (half - 1) * half + half
    return pl.pallas_call(
        _peer_q_kernel,
        grid=(n // tm,),
        in_specs=[pl.BlockSpec((tm, D_MODEL), lambda i: (i, 0)),
                  pl.BlockSpec((1, D_MODEL), lambda i: (0, 0)),
                  pl.BlockSpec((nq, D_MODEL), lambda i: (0, 0)),
                  pl.BlockSpec((2, N_KEYS, HALF_KEY), lambda i: (0, 0, 0))],
        out_specs=[pl.BlockSpec((D_MODEL, tm), lambda i: (0, i)),
                   pl.BlockSpec((PEER_HEADS, N_KEYS, tm), lambda i: (0, 0, i)),
                   pl.BlockSpec((PEER_HEADS, N_KEYS, tm), lambda i: (0, 0, i)),
                   pl.BlockSpec((4, PEER_HEADS, tm), lambda i: (0, 0, i))],
        out_shape=[jax.ShapeDtypeStruct((D_MODEL, n), bf16),
                   jax.ShapeDtypeStruct((PEER_HEADS, N_KEYS, n), f32),
                   jax.ShapeDtypeStruct((PEER_HEADS, N_KEYS, n), f32),
                   jax.ShapeDtypeStruct((4, PEER_HEADS, n), f32)],
        scratch_shapes=[pltpu.VMEM((n_cand, tm), f32)],
        compiler_params=_cparams(("parallel",)),
        name="peer_query",
    )(h1, g, wqt, sub_keys)


def _peer_dense_kernel(xnt_ref, u_ref, vt_ref, s1_ref, s2_ref, st_ref, o_ref, e1_ref, e2_ref, *, te):
    j = pl.program_id(1)

    @pl.when(j == 0)
    def _():
        for h in range(PEER_HEADS):
            e1_ref[h] = jnp.exp(s1_ref[h] - st_ref[1, h:h + 1, :])
            e2_ref[h] = jnp.exp(s2_ref[h] - st_ref[2, h:h + 1, :]) * st_ref[3, h:h + 1, :]

    ht = _dot(u_ref[...], xnt_ref[...])
    acts = []
    for ii in range(te // N_KEYS):
        i1 = j * (te // N_KEYS) + ii
        w = jnp.zeros((N_KEYS, ht.shape[1]), f32)
        for h in range(PEER_HEADS):
            s1row = s1_ref[h, pl.ds(i1, 1), :]
            e1row = e1_ref[h, pl.ds(i1, 1), :]
            sel = (s2_ref[h] + s1row) >= st_ref[0, h:h + 1, :]
            w = w + jnp.where(sel, e2_ref[h] * e1row, 0.0)
        hh = ht[ii * N_KEYS:(ii + 1) * N_KEYS, :]
        gelu = 0.5 * hh * (1.0 + lax.erf(hh * np.float32(1.0 / np.sqrt(2.0))))
        acts.append((gelu * w).astype(bf16))
    upd = _dot(vt_ref[...], jnp.concatenate(acts, axis=0))

    @pl.when(j == 0)
    def _():
        o_ref[...] = upd

    @pl.when(j > 0)
    def _():
        o_ref[...] += upd


def _peer_dense(xnt, u_b, vt_b, s1, s2, stats, *, tm, te):
    n = xnt.shape[1]
    n_exp = u_b.shape[0]
    assert n % tm == 0 and n_exp % te == 0 and te % N_KEYS == 0
    once = pl.Buffered(1)
    return pl.pallas_call(
        functools.partial(_peer_dense_kernel, te=te),
        grid=(n // tm, n_exp // te),
        in_specs=[pl.BlockSpec((D_MODEL, tm), lambda i, j: (0, i), pipeline_mode=once),
                  pl.BlockSpec((te, D_MODEL), lambda i, j: (j, 0)),
                  pl.BlockSpec((D_MODEL, te), lambda i, j: (0, j)),
                  pl.BlockSpec((PEER_HEADS, N_KEYS, tm), lambda i, j: (0, 0, i), pipeline_mode=once),
                  pl.BlockSpec((PEER_HEADS, N_KEYS, tm), lambda i, j: (0, 0, i), pipeline_mode=once),
                  pl.BlockSpec((4, PEER_HEADS, tm), lambda i, j: (0, 0, i))],
        out_specs=pl.BlockSpec((D_MODEL, tm), lambda i, j: (0, i)),
        out_shape=jax.ShapeDtypeStruct((D_MODEL, n), f32),
        scratch_shapes=[pltpu.VMEM((PEER_HEADS, N_KEYS, tm), f32),
                        pltpu.VMEM((PEER_HEADS, N_KEYS, tm), f32)],
        compiler_params=_cparams(("parallel", "arbitrary")),
        name="peer_dense",
    )(xnt, u_b, vt_b, s1, s2, stats)


def _final_kernel(ft_ref, h_ref, g_ref, o_ref):
    o_ref[...] = _rms(h_ref[...] + ft_ref[...].T, g_ref[...])


def _final(ffn_t, h1, g, *, tm):
    n = h1.shape[0]
    assert n % tm == 0
    return pl.pallas_call(
        _final_kernel,
        grid=(n // tm,),
        in_specs=[pl.BlockSpec((D_MODEL, tm), lambda i: (0, i)),
                  pl.BlockSpec((tm, D_MODEL), lambda i: (i, 0)),
                  pl.BlockSpec((1, D_MODEL), lambda i: (0, 0))],
        out_specs=pl.BlockSpec((tm, D_MODEL), lambda i: (i, 0)),
        out_shape=jax.ShapeDtypeStruct((n, D_MODEL), f32),
        compiler_params=_cparams(("parallel",)),
        name="final_norm",
    )(ffn_t, h1, g)


def _prep_w_in(w):
    cq, ckv, kr = w[:, 0:1024], w[:, 1024:1536], w[:, 1536:1600]
    z, xs = w[:, 1600:3648], w[:, 3648:5696]
    bm, cm, dt = w[:, 5696:6208], w[:, 6208:6720], w[:, 6720:6752]
    kr_sw = jnp.concatenate([kr[:, 32:], kr[:, :32]], axis=1)
    pad = jnp.zeros((w.shape[0], LANE - 3 * SSD_HEADS), w.dtype)
    return jnp.concatenate([z, xs, cq, bm, cm, ckv, kr, kr_sw, dt, dt, dt, pad], axis=1).astype(bf16)


def _prep_w_uq(w):
    w = w.reshape(Q_LORA, MLA_HEADS, QK_NOPE + QK_ROPE)
    nope, x1, x2 = w[..., :QK_NOPE], w[..., QK_NOPE:QK_NOPE + 32], w[..., QK_NOPE + 32:]
    return jnp.concatenate([nope, x1, x2, x2, x1], axis=-1).reshape(Q_LORA, MLA_HEADS * QK_SLAB).astype(bf16)


def _rope_tables(pos):
    inv_freq = ROPE_THETA ** (-jnp.arange(0, QK_ROPE, 2, dtype=f32) / QK_ROPE)
    ang = pos[:, None] * inv_freq[None, :]
    c, s = jnp.cos(ang), jnp.sin(ang)
    zero = jnp.zeros((pos.shape[0], LANE - QK_ROPE), f32)
    return jnp.concatenate([c, c, zero], axis=1), jnp.concatenate([-s, s, zero], axis=1)


def _expand_matrix(width):
    r = np.arange(LANE)[:, None]
    col = np.arange(SSD_HEADS * width)[None, :]
    return jnp.asarray(((r % SSD_HEADS) == (col // width)) & (r < 3 * SSD_HEADS), dtype=bf16)


def _tile(m, pref):
    t = min(pref, m)
    assert m % t == 0
    return t


def kernel(x, meta_tokens, attn_norm, w_in, q_norm, w_uq, kv_norm, w_ukv, conv_w, conv_b, dt_bias, a_log, d_skip,
           ssd_norm, w_out, ffn_norm, w_query, sub_keys, u_experts, v_experts, final_norm):
    bsz, seq, d = x.shape
    assert d == D_MODEL and seq % CHUNK == 0
    n_tok = bsz * seq
    x2d = x.reshape(n_tok, d)

    w_in_b = _prep_w_in(w_in[0])
    wq_b = _prep_w_uq(w_uq[0])
    wkv = w_ukv[0].reshape(KV_LORA, MLA_HEADS, QK_NOPE + V_HEAD)
    wk_b = wkv[..., :QK_NOPE].reshape(KV_LORA, MLA_HEADS * QK_NOPE).astype(bf16)
    wv_b = wkv[..., QK_NOPE:].reshape(KV_LORA, MLA_HEADS * V_HEAD).astype(bf16)
    w_out_b = w_out[0].astype(bf16)
    wqt_b = w_query[0].T.astype(bf16)
    u_b = u_experts[0].astype(bf16)
    vt_b = v_experts[0].T.astype(bf16)
    cwx, cwbc = conv_w[0][:, :D_INNER], conv_w[0][:, D_INNER:]
    cbx, cbbc = conv_b[0][None, :D_INNER], conv_b[0][None, D_INNER:]
    rep3 = lambda v, fill: jnp.concatenate([v, v, v, jnp.full((LANE - 3 * SSD_HEADS,), fill, f32)])[None, :]
    dtb = rep3(dt_bias[0], 0.0)
    alog = rep3(a_log[0], 0.0)
    dskip = jnp.repeat(d_skip[0], SSD_HEAD_DIM)[None, :]
    e64 = _expand_matrix(SSD_HEAD_DIM)
    e128 = _expand_matrix(CHUNK)

    lead_rows = jnp.concatenate([jnp.zeros((LEAD_PAD, d), x.dtype), meta_tokens.astype(x.dtype)], axis=0)
    cc_l, ss_l = _rope_tables(jnp.arange(LEAD, dtype=f32) - LEAD_PAD)
    cc_r, ss_r = _rope_tables(jnp.arange(seq, dtype=f32) + N_META)

    g_attn = attn_norm[0][None, :]
    proj_l = _inproj(lead_rows, g_attn, w_in_b, tm=LEAD)
    proj_r = _inproj(x2d, g_attn, w_in_b, tm=_tile(n_tok, 512))

    qn, kvn = q_norm[0][None, :], kv_norm[0][None, :]
    _, k_l, v_l = _mla_up(proj_l, cc_l, ss_l, qn, kvn, wq_b, wk_b, wv_b, tm=LEAD, n_pos_blocks=1)
    tmu = _tile(seq, 256)
    q_r, k_r, v_r = _mla_up(proj_r, cc_r, ss_r, qn, kvn, wq_b, wk_b, wv_b, tm=tmu, n_pos_blocks=seq // tmu)
    a_out = _attention(q_r, k_r, v_r, k_l, v_l, bsz=bsz, seq=seq, tq=_tile(seq, 512))

    s_out = _ssd(proj_l, proj_r, cwx, cwbc, cbx, cbbc, dtb, alog, dskip, ssd_norm[0][None, :], e64, e128,
                 bsz=bsz, seq=seq)

    h1 = _outproj(a_out, s_out, w_out_b, x2d, tm=_tile(n_tok, 1024))

    xnt, s1, s2, stats = _peer_q(h1, ffn_norm[0][None, :], wqt_b, sub_keys[0], tm=_tile(n_tok, 256))
    ffn_t = _peer_dense(xnt, u_b, vt_b, s1, s2, stats, tm=_tile(n_tok, 512), te=512)
    out = _final(ffn_t, h1, final_norm[None, :], tm=_tile(n_tok, 256))
    return out.reshape(bsz, seq, d)
```

```python
import functools

import jax
import jax.numpy as jnp
import numpy as np
from jax import lax
from jax.experimental import pallas as pl
from jax.experimental.pallas import tpu as pltpu

f32 = jnp.float32
bf16 = jnp.bfloat16

D_MODEL = 4096
N_META = 16
CHUNK = 128
LEAD = CHUNK
LEAD_PAD = CHUNK - N_META
MLA_HEADS = 16
QK_NOPE = 128
QK_ROPE = 64
V_HEAD = 128
Q_LORA = 1024
KV_LORA = 512
ROPE_THETA = 10000.0
D_INNER = 2048
SSD_HEAD_DIM = 64
SSD_HEADS = 32
SSD_GROUPS = 4
SSD_STATE = 128
CONV_W = 4
PEER_HEADS = 8
PEER_TOPK = 16
N_KEYS = 128
HALF_KEY = 128
EPS = 1e-6
NEG_INF = -1e30

LANE = 128
QK_SLAB = 2 * LANE
GROUP_W = D_INNER // SSD_GROUPS

P_Z = 0
P_XS = 2048
P_CQ = 4096
P_B = 5120
P_C = 5632
P_CKV = 6144
P_KR = 6656
P_DT = 6784
NP = 6912

VMEM_LIMIT = 56 * 1024 * 1024


def _cparams(sem, vmem=VMEM_LIMIT, flags=None):
    return pltpu.CompilerParams(dimension_semantics=sem, vmem_limit_bytes=vmem, flags=flags)


def _dot(a, b):
    return jnp.dot(a, b, preferred_element_type=f32)


def _dot_nt(a, b):
    return lax.dot_general(a, b, (((1,), (1,)), ((), ())), preferred_element_type=f32)


def _rms(x, g):
    return x * lax.rsqrt(jnp.mean(x * x, axis=-1, keepdims=True) + EPS) * g


def _inproj_kernel(x_ref, g_ref, w_ref, o_ref, xn_ref, *, sub):
    @pl.when(pl.program_id(1) == 0)
    def _():
        def body(r, c):
            rows = pl.ds(pl.multiple_of(r * sub, sub), sub)
            xn_ref[rows, :] = _rms(x_ref[rows, :], g_ref[...]).astype(bf16)
            return c
        lax.fori_loop(0, x_ref.shape[0] // sub, body, 0)

    o_ref[...] = _dot(xn_ref[...], w_ref[...])


def _inproj(x2d, g, w, *, tm, tn=768):
    m = x2d.shape[0]
    assert m % tm == 0 and NP % tn == 0
    sub = min(64, tm)
    return pl.pallas_call(
        functools.partial(_inproj_kernel, sub=sub),
        grid=(m // tm, NP // tn),
        in_specs=[pl.BlockSpec((tm, D_MODEL), lambda i, j: (i, 0)),
                  pl.BlockSpec((1, D_MODEL), lambda i, j: (0, 0)),
                  pl.BlockSpec((D_MODEL, tn), lambda i, j: (0, j))],
        out_specs=pl.BlockSpec((tm, tn), lambda i, j: (i, j)),
        out_shape=jax.ShapeDtypeStruct((m, NP), f32),
        scratch_shapes=[pltpu.VMEM((tm, D_MODEL), bf16)],
        compiler_params=_cparams(("parallel", "arbitrary")),
        name="inproj",
    )(x2d, g, w)


def _mla_up_kernel(cq_ref, ckv_ref, kr_ref, cc_ref, ss_ref, qn_ref, kvn_ref, wq_ref, wk_ref, wv_ref,
                   q_ref, k_ref, v_ref, *, scale):
    cc = cc_ref[...]
    ss = ss_ref[...]

    def rope(t):
        return t * cc + pltpu.roll(t, 64, 1) * ss

    cqn = _rms(cq_ref[...], qn_ref[...]).astype(bf16)
    q = _dot(cqn, wq_ref[...])
    for h in range(MLA_HEADS):
        lo = h * QK_SLAB
        q_ref[:, lo:lo + LANE] = (q[:, lo:lo + LANE] * scale).astype(bf16)
        q_ref[:, lo + LANE:lo + QK_SLAB] = (rope(q[:, lo + LANE:lo + QK_SLAB]) * scale).astype(bf16)

    kvn = _rms(ckv_ref[...], kvn_ref[...]).astype(bf16)
    kn = _dot(kvn, wk_ref[...])
    kpe = rope(kr_ref[...]).astype(bf16)
    for h in range(MLA_HEADS):
        lo = h * QK_SLAB
        k_ref[:, lo:lo + LANE] = kn[:, h * LANE:(h + 1) * LANE].astype(bf16)
        k_ref[:, lo + LANE:lo + QK_SLAB] = kpe
    v_ref[...] = _dot(kvn, wv_ref[...]).astype(bf16)


def _mla_up(proj, cc, ss, qn, kvn, wq, wk, wv, *, tm, n_pos_blocks):
    m = proj.shape[0]
    assert m % tm == 0
    scale = float((QK_NOPE + QK_ROPE) ** -0.5)
    hq = MLA_HEADS * QK_SLAB
    hv = MLA_HEADS * V_HEAD
    const = lambda i: (0, 0)
    return pl.pallas_call(
        functools.partial(_mla_up_kernel, scale=scale),
        grid=(m // tm,),
        in_specs=[pl.BlockSpec((tm, Q_LORA), lambda i: (i, P_CQ // Q_LORA)),
                  pl.BlockSpec((tm, KV_LORA), lambda i: (i, P_CKV // KV_LORA)),
                  pl.BlockSpec((tm, LANE), lambda i: (i, P_KR // LANE)),
                  pl.BlockSpec((tm, LANE), lambda i: (i % n_pos_blocks, 0)),
                  pl.BlockSpec((tm, LANE), lambda i: (i % n_pos_blocks, 0)),
                  pl.BlockSpec((1, Q_LORA), const),
                  pl.BlockSpec((1, KV_LORA), const),
                  pl.BlockSpec((Q_LORA, hq), const),
                  pl.BlockSpec((KV_LORA, hv), const),
                  pl.BlockSpec((KV_LORA, hv), const)],
        out_specs=[pl.BlockSpec((tm, hq), lambda i: (i, 0)),
                   pl.BlockSpec((tm, hq), lambda i: (i, 0)),
                   pl.BlockSpec((tm, hv), lambda i: (i, 0))],
        out_shape=[jax.ShapeDtypeStruct((m, hq), bf16),
                   jax.ShapeDtypeStruct((m, hq), bf16),
                   jax.ShapeDtypeStruct((m, hv), bf16)],
        compiler_params=_cparams(("parallel",)),
        name="mla_up",
    )(proj, proj, proj, cc, ss, qn, kvn, wq, wk, wv)


def _attn_kernel(q_ref, k_ref, v_ref, kl_ref, vl_ref, o_ref, *, tq):
    i = pl.program_id(2)
    q = q_ref[...]

    diag = pl.ds(pl.multiple_of(i * tq, tq), tq)
    s_l = _dot_nt(q, kl_ref[...])
    col = lax.broadcasted_iota(jnp.int32, s_l.shape, 1)
    s_l = jnp.where(col >= LEAD_PAD, s_l, NEG_INF)
    s_d = _dot_nt(q, k_ref[diag, :])
    r_id = lax.broadcasted_iota(jnp.int32, s_d.shape, 0)
    c_id = lax.broadcasted_iota(jnp.int32, s_d.shape, 1)
    s_d = jnp.where(c_id <= r_id, s_d, NEG_INF)
    m = jnp.maximum(jnp.max(s_l, axis=-1, keepdims=True), jnp.max(s_d, axis=-1, keepdims=True))
    p_l = jnp.exp(s_l - m)
    p_d = jnp.exp(s_d - m)
    l = jnp.sum(p_l, axis=-1, keepdims=True) + jnp.sum(p_d, axis=-1, keepdims=True)
    acc = _dot(p_l.astype(bf16), vl_ref[...]) + _dot(p_d.astype(bf16), v_ref[diag, :])

    def update(s, vb, carry):
        m, l, acc = carry
        m_new = jnp.maximum(m, jnp.max(s, axis=-1, keepdims=True))
        alpha = jnp.exp(m - m_new)
        p = jnp.exp(s - m_new)
        l = alpha * l + jnp.sum(p, axis=-1, keepdims=True)
        acc = alpha * acc + _dot(p.astype(bf16), vb)
        return m_new, l, acc

    def body(j, carry):
        rows = pl.ds(pl.multiple_of(j * tq, tq), tq)
        return update(_dot_nt(q, k_ref[rows, :]), v_ref[rows, :], carry)

    m, l, acc = lax.fori_loop(0, i, body, (m, l, acc))
    o_ref[...] = (acc / l).astype(bf16)


def _attention(q, k, v, kl, vl, *, bsz, seq, tq):
    assert seq % tq == 0
    nq = seq // tq
    return pl.pallas_call(
        functools.partial(_attn_kernel, tq=tq),
        grid=(bsz, MLA_HEADS, nq),
        in_specs=[pl.BlockSpec((tq, QK_SLAB), lambda b, h, i: (b * nq + i, h)),
                  pl.BlockSpec((seq, QK_SLAB), lambda b, h, i: (b, h)),
                  pl.BlockSpec((seq, V_HEAD), lambda b, h, i: (b, h)),
                  pl.BlockSpec((LEAD, QK_SLAB), lambda b, h, i: (0, h)),
                  pl.BlockSpec((LEAD, V_HEAD), lambda b, h, i: (0, h))],
        out_specs=pl.BlockSpec((tq, V_HEAD), lambda b, h, i: (b * nq + i, h)),
        out_shape=jax.ShapeDtypeStruct((bsz * seq, MLA_HEADS * V_HEAD), bf16),
        compiler_params=_cparams(("parallel", "parallel", "arbitrary")),
        name="mla_attn",
    )(q, k, v, kl, vl)


def _split3(v):
    b0 = v.astype(bf16).astype(f32)
    r1 = v - b0
    b1 = r1.astype(bf16).astype(f32)
    b2 = (r1 - b1).astype(bf16).astype(f32)
    return b0, b1, b2


def _ssd_kernel(zx_l, bc_l, dt_l, zx_r, bc_r, dt_r, cwx, cwbc, cbx, cbbc, dtb, alog, dskip, ng, e64, e128,
                o_ref, st_ref, px_ref, pbc_ref):
    c = pl.program_id(1)
    is_lead = c == 0

    @pl.when(is_lead)
    def _():
        st_ref[...] = jnp.zeros_like(st_ref)
        px_ref[...] = jnp.zeros_like(px_ref)
        pbc_ref[...] = jnp.zeros_like(pbc_ref)

    zx = jnp.where(is_lead, zx_l[...], zx_r[...])
    bcraw = jnp.where(is_lead, bc_l[...], bc_r[...])
    dtraw = jnp.where(is_lead, dt_l[...], dt_r[...])
    z = zx[:, :D_INNER]
    xraw = zx[:, D_INNER:]

    row1 = lax.broadcasted_iota(jnp.int32, (CHUNK, 1), 0)
    r_id = lax.broadcasted_iota(jnp.int32, (CHUNK, CHUNK), 0)
    c_id = lax.broadcasted_iota(jnp.int32, (CHUNK, CHUNK), 1)
    tril = r_id >= c_id

    def conv_silu(cur, prev_ref, w_ref, b_ref):
        prev = prev_ref[...]
        acc = cur * w_ref[CONV_W - 1:CONV_W, :] + b_ref[...]
        for k in range(1, CONV_W):
            comb = jnp.where(row1 >= CHUNK - k, prev, cur)
            acc = acc + pltpu.roll(comb, k, 0) * w_ref[CONV_W - 1 - k:CONV_W - k, :]
        prev_ref[...] = cur
        return acc * jax.nn.sigmoid(acc)

    xs = conv_silu(xraw, px_ref, cwx, cbx)
    bcc = conv_silu(bcraw, pbc_ref, cwbc, cbbc)
    b_all = bcc[:, :SSD_GROUPS * SSD_STATE]
    c_all = bcc[:, SSD_GROUPS * SSD_STATE:].astype(bf16)

    pre = dtraw + dtb[...]
    dt = jnp.maximum(pre, 0.0) + jnp.log1p(jnp.exp(-jnp.abs(pre)))
    dt = jnp.where(jnp.logical_and(is_lead, row1 < LEAD_PAD), 0.0, dt)
    da = dt * (-jnp.exp(alog[...]))
    tri = tril.astype(f32).astype(bf16)
    d0, d1, d2 = _split3(da)
    cs = _dot(tri, d0.astype(bf16)) + _dot(tri, d1.astype(bf16)) + _dot(tri, d2.astype(bf16))
    cs_t = cs.T
    ecs = jnp.exp(cs)
    dec = jnp.exp(cs[CHUNK - 1:CHUNK, :] - cs)

    def kcat(v):
        b0, b1, b2 = _split3(v)
        return jnp.where(c_id < 32, b0, jnp.where(c_id < 64, b1, jnp.where(c_id < 96, b2, 0.0))).astype(bf16)

    ex = _dot(jnp.concatenate([kcat(dt), kcat(ecs), kcat(dec)], axis=0), e64[...])
    dt_x = ex[0:CHUNK]
    ecs_x = ex[CHUNK:2 * CHUNK]
    dec_x = ex[2 * CHUNK:3 * CHUNK]
    cs_bc = _dot(kcat(cs), e128[...])

    xdt = xs * dt_x
    xdt_b = xdt.astype(bf16)
    xd_b = (xdt * dec_x).astype(bf16)

    ys = []
    for g in range(SSD_GROUPS):
        gl = slice(g * GROUP_W, (g + 1) * GROUP_W)
        b_g = b_all[:, g * SSD_STATE:(g + 1) * SSD_STATE]
        c_g = c_all[:, g * SSD_STATE:(g + 1) * SSD_STATE]
        cb = _dot_nt(c_g, b_g.astype(bf16))
        st_g = st_ref[:, gl]
        y_g = _dot(c_g, st_g.astype(bf16)) * ecs_x[:, gl]
        pieces = []
        for pr in range(GROUP_W // LANE):
            h0 = g * (GROUP_W // SSD_HEAD_DIM) + 2 * pr
            xpair = xdt_b[:, h0 * SSD_HEAD_DIM:(h0 + 2) * SSD_HEAD_DIM]
            yh = []
            for h in (h0, h0 + 1):
                seg = cs_bc[:, h * CHUNK:(h + 1) * CHUNK] - cs_t[h:h + 1, :]
                lm = jnp.where(tril, jnp.exp(jnp.minimum(seg, 0.0)), 0.0)
                yh.append(_dot((cb * lm).astype(bf16), xpair))
            pieces.append(jnp.where(c_id < SSD_HEAD_DIM, yh[0], yh[1]))
        y_g = y_g + jnp.concatenate(pieces, axis=1)
        st_ref[:, gl] = st_g * ecs_x[CHUNK - 1:CHUNK, gl] + _dot(b_g.T.astype(bf16), xd_b[:, gl])
        ys.append(y_g)

    y = jnp.concatenate(ys, axis=1) + xs * dskip[...]
    gated = y * (z * jax.nn.sigmoid(z))
    outs = []
    for g in range(SSD_GROUPS):
        gg = gated[:, g * GROUP_W:(g + 1) * GROUP_W]
        outs.append(gg * lax.rsqrt(jnp.mean(gg * gg, axis=-1, keepdims=True) + EPS))
    o_ref[...] = (jnp.concatenate(outs, axis=1) * ng[...]).astype(bf16)


def _ssd(proj_l, proj_r, cwx, cwbc, cbx, cbbc, dtb, alog, dskip, ng, e64, e128, *, bsz, seq):
    nch = seq // CHUNK
    real = lambda w: (lambda b, c: (b * nch + jnp.maximum(c - 1, 0), w))
    lead = lambda w: (lambda b, c: (0, w))
    const = lambda b, c: (0, 0)
    zx_w, bc_w, dt_w = 2 * D_INNER, 2 * SSD_GROUPS * SSD_STATE, LANE
    return pl.pallas_call(
        _ssd_kernel,
        grid=(bsz, nch + 1),
        in_specs=[pl.BlockSpec((CHUNK, zx_w), lead(P_Z // zx_w)),
                  pl.BlockSpec((CHUNK, bc_w), lead(P_B // bc_w)),
                  pl.BlockSpec((CHUNK, dt_w), lead(P_DT // dt_w)),
                  pl.BlockSpec((CHUNK, zx_w), real(P_Z // zx_w)),
                  pl.BlockSpec((CHUNK, bc_w), real(P_B // bc_w)),
                  pl.BlockSpec((CHUNK, dt_w), real(P_DT // dt_w)),
                  pl.BlockSpec((CONV_W, D_INNER), const),
                  pl.BlockSpec((CONV_W, bc_w), const),
                  pl.BlockSpec((1, D_INNER), const),
                  pl.BlockSpec((1, bc_w), const),
                  pl.BlockSpec((1, LANE), const),
                  pl.BlockSpec((1, LANE), const),
                  pl.BlockSpec((1, D_INNER), const),
                  pl.BlockSpec((1, D_INNER), const),
                  pl.BlockSpec((LANE, D_INNER), const),
                  pl.BlockSpec((LANE, SSD_HEADS * CHUNK), const)],
        out_specs=pl.BlockSpec((CHUNK, D_INNER), real(0)),
        out_shape=jax.ShapeDtypeStruct((bsz * seq, D_INNER), bf16),
        scratch_shapes=[pltpu.VMEM((SSD_STATE, D_INNER), f32),
                        pltpu.VMEM((CHUNK, D_INNER), f32),
                        pltpu.VMEM((CHUNK, bc_w), f32)],
        compiler_params=_cparams(("parallel", "arbitrary")),
        name="ssd",
    )(proj_l, proj_l, proj_l, proj_r, proj_r, proj_r, cwx, cwbc, cbx, cbbc, dtb, alog, dskip, ng, e64, e128)


def _outproj_kernel(a_ref, s_ref, wa_ref, ws_ref, x_ref, o_ref):
    o_ref[...] = _dot(a_ref[...], wa_ref[...]) + _dot(s_ref[...], ws_ref[...]) + x_ref[...]


def _outproj(a, s, w, x2d, *, tm, tn=512):
    m = a.shape[0]
    ka = a.shape[1]
    assert m % tm == 0 and D_MODEL % tn == 0
    return pl.pallas_call(
        _outproj_kernel,
        grid=(m // tm, D_MODEL // tn),
        in_specs=[pl.BlockSpec((tm, ka), lambda i, j: (i, 0)),
                  pl.BlockSpec((tm, D_INNER), lambda i, j: (i, 0)),
                  pl.BlockSpec((ka, tn), lambda i, j: (0, j)),
                  pl.BlockSpec((D_INNER, tn), lambda i, j: (1, j)),
                  pl.BlockSpec((tm, tn), lambda i, j: (i, j))],
        out_specs=pl.BlockSpec((tm, tn), lambda i, j: (i, j)),
        out_shape=jax.ShapeDtypeStruct((m, D_MODEL), f32),
        compiler_params=_cparams(("parallel", "arbitrary")),
        name="outproj",
    )(a, s, w, w, x2d)


def _extract_topk(x, k):
    vals = []
    for _ in range(k):
        m = jnp.max(x, axis=0, keepdims=True)
        x = jnp.where(x == m, -jnp.inf, x)
        vals.append(m)
    return vals


def _stack_rows(vals):
    n = len(vals)
    rid = lax.broadcasted_iota(jnp.int32, (n, vals[0].shape[1]), 0)
    out = jnp.zeros((n, vals[0].shape[1]), f32)
    for r, v in enumerate(vals):
        out = jnp.where(rid == r, v, out)
    return out


N_CAND = PEER_TOPK + (PEER_TOPK // 2 - 1) * (PEER_TOPK // 2) + PEER_TOPK


def _peer_q_kernel(h_ref, g_ref, wq_ref, sk_ref, xnt_ref, e1_ref, thr_ref, e2_ref, cand_ref):
    xn = _rms(h_ref[...], g_ref[...])
    xnt_ref[...] = xn.T.astype(bf16)
    q = _dot(xn.astype(bf16), wq_ref[...]).astype(bf16)
    sk1 = sk_ref[0].astype(bf16)
    sk2 = sk_ref[1].astype(bf16)
    k = PEER_TOPK
    half = k // 2
    for h in range(PEER_HEADS):
        lo = h * 2 * HALF_KEY
        s1 = _dot_nt(sk1, q[:, lo:lo + HALF_KEY])
        s2 = _dot_nt(sk2, q[:, lo + HALF_KEY:lo + 2 * HALF_KEY])
        v1 = _extract_topk(s1, k + 1)
        v2 = _extract_topk(s2, k + 1)
        v2_top = _stack_rows(v2[:k])
        cand_ref[0:k, :] = v1[0] + v2_top
        for a in range(1, half):
            cand_ref[k + (a - 1) * half:k + a * half, :] = v1[a] + v2_top[0:half, :]
        cand_ref[k + (half - 1) * half:k + half * half, :] = _stack_rows(v1[half:k]) + v2[0]
        ninf = jnp.full_like(v1[0], -jnp.inf)
        cand_ref[k + half * half:, :] = _stack_rows([v1[0] + v2[k], v1[k] + v2[0]] + [ninf] * (half - 2))
        top = _extract_topk(cand_ref[...], k + 1)
        zsum = jnp.zeros_like(top[0])
        for t in top[:k]:
            zsum = zsum + jnp.exp(t - top[0])
        tau = 0.5 * (top[k - 1] + top[k])
        scale = 0.5 / zsum
        e1_ref[h] = jnp.exp(s1 - v1[0])
        thr_ref[h] = jnp.exp((tau - v2[0]) - s1) * scale
        e2_ref[h] = (jnp.exp(s2 - v2[0]) * scale).astype(bf16)


def _peer_q(h1, g, wq, sub_keys, *, tm):
    n = h1.shape[0]
    assert n % tm == 0
    nq = PEER_HEADS * 2 * HALF_KEY
    per_head = lambda dt: (pl.BlockSpec((PEER_HEADS, N_KEYS, tm), lambda i: (0, 0, i)),
                           jax.ShapeDtypeStruct((PEER_HEADS, N_KEYS, n), dt))
    specs, shapes = zip((pl.BlockSpec((D_MODEL, tm), lambda i: (0, i)), jax.ShapeDtypeStruct((D_MODEL, n), bf16)),
                        per_head(f32), per_head(f32), per_head(bf16))
    return pl.pallas_call(
        _peer_q_kernel,
        grid=(n // tm,),
        in_specs=[pl.BlockSpec((tm, D_MODEL), lambda i: (i, 0)),
                  pl.BlockSpec((1, D_MODEL), lambda i: (0, 0)),
                  pl.BlockSpec((D_MODEL, nq), lambda i: (0, 0)),
                  pl.BlockSpec((2, N_KEYS, HALF_KEY), lambda i: (0, 0, 0))],
        out_specs=list(specs),
        out_shape=list(shapes),
        scratch_shapes=[pltpu.VMEM((N_CAND, tm), f32)],
        compiler_params=_cparams(("parallel",)),
        name="peer_query",
    )(h1, g, wq, sub_keys)


BF16_ROWS = 16


def _peer_dense_kernel(xnt_ref, u_ref, v_ref, e1_ref, thr_ref, e2_ref, o_ref, act_ref, *, te):
    j = pl.program_id(1)
    tm = xnt_ref.shape[1]

    @pl.when(j == 0)
    def _():
        o_ref[...] = jnp.zeros_like(o_ref)

    for ii in range(te // N_KEYS):
        i1 = j * (te // N_KEYS) + ii
        rows = slice(ii * N_KEYS, (ii + 1) * N_KEYS)
        ht = _dot(u_ref[rows, :], xnt_ref[...])
        e1rows = [e1_ref[h, pl.ds(i1, 1), :] for h in range(PEER_HEADS)]
        thrrows = [thr_ref[h, pl.ds(i1, 1), :] for h in range(PEER_HEADS)]
        for c in range(tm // LANE):
            cl = slice(c * LANE, (c + 1) * LANE)
            w = jnp.zeros((N_KEYS, LANE), bf16)
            for h in range(PEER_HEADS):
                e1b = jnp.broadcast_to(e1rows[h][:, cl].astype(bf16), (N_KEYS, LANE))
                thrb = jnp.broadcast_to(thrrows[h][:, cl].astype(bf16), (N_KEYS, LANE))
                e2 = e2_ref[h, :, cl]
                w = w + jnp.where(e2 >= thrb, e2, jnp.zeros_like(e2)) * e1b
            hh = ht[:, cl]
            g = hh * (1.0 + lax.erf(hh * np.float32(1.0 / np.sqrt(2.0))))
            act_ref[rows, cl] = g.astype(bf16) * w
    o_ref[...] += _dot(act_ref[...].T, v_ref[...])


def _peer_dense(xnt, u_b, v_b, e1, thr, e2b, *, tm, te):
    n = xnt.shape[1]
    n_exp = u_b.shape[0]
    assert n % tm == 0 and n_exp % te == 0 and te % N_KEYS == 0 and tm % LANE == 0
    once = pl.Buffered(1)
    return pl.pallas_call(
        functools.partial(_peer_dense_kernel, te=te),
        grid=(n // tm, n_exp // te),
        in_specs=[pl.BlockSpec((D_MODEL, tm), lambda i, j: (0, i), pipeline_mode=once),
                  pl.BlockSpec((te, D_MODEL), lambda i, j: (j, 0)),
                  pl.BlockSpec((te, D_MODEL), lambda i, j: (j, 0)),
                  pl.BlockSpec((PEER_HEADS, N_KEYS, tm), lambda i, j: (0, 0, i), pipeline_mode=once),
                  pl.BlockSpec((PEER_HEADS, N_KEYS, tm), lambda i, j: (0, 0, i), pipeline_mode=once),
                  pl.BlockSpec((PEER_HEADS, N_KEYS, tm), lambda i, j: (0, 0, i), pipeline_mode=once)],
        out_specs=pl.BlockSpec((tm, D_MODEL), lambda i, j: (i, 0)),
        out_shape=jax.ShapeDtypeStruct((n, D_MODEL), f32),
        scratch_shapes=[pltpu.VMEM((te, tm), bf16)],
        compiler_params=_cparams(("parallel", "arbitrary")),
        name="peer_dense",
    )(xnt, u_b, v_b, e1, thr, e2b)


def _final_kernel(f_ref, h_ref, g_ref, o_ref):
    o_ref[...] = _rms(h_ref[...] + f_ref[...], g_ref[...])


def _final(ffn, h1, g, *, tm):
    n = h1.shape[0]
    assert n % tm == 0
    return pl.pallas_call(
        _final_kernel,
        grid=(n // tm,),
        in_specs=[pl.BlockSpec((tm, D_MODEL), lambda i: (i, 0)),
                  pl.BlockSpec((tm, D_MODEL), lambda i: (i, 0)),
                  pl.BlockSpec((1, D_MODEL), lambda i: (0, 0))],
        out_specs=pl.BlockSpec((tm, D_MODEL), lambda i: (i, 0)),
        out_shape=jax.ShapeDtypeStruct((n, D_MODEL), f32),
        compiler_params=_cparams(("parallel",)),
        name="final_norm",
    )(ffn, h1, g)


def _prep_w_in(w):
    cq, ckv, kr = w[:, 0:1024], w[:, 1024:1536], w[:, 1536:1600]
    z, xs = w[:, 1600:3648], w[:, 3648:5696]
    bm, cm, dt = w[:, 5696:6208], w[:, 6208:6720], w[:, 6720:6752]
    kr_sw = jnp.concatenate([kr[:, 32:], kr[:, :32]], axis=1)
    pad = jnp.zeros((w.shape[0], LANE - 3 * SSD_HEADS), w.dtype)
    return jnp.concatenate([z, xs, cq, bm, cm, ckv, kr, kr_sw, dt, dt, dt, pad], axis=1).astype(bf16)


def _prep_w_uq(w):
    w = w.reshape(Q_LORA, MLA_HEADS, QK_NOPE + QK_ROPE)
    nope, x1, x2 = w[..., :QK_NOPE], w[..., QK_NOPE:QK_NOPE + 32], w[..., QK_NOPE + 32:]
    return jnp.concatenate([nope, x1, x2, x2, x1], axis=-1).reshape(Q_LORA, MLA_HEADS * QK_SLAB).astype(bf16)


def _rope_tables(pos):
    inv_freq = ROPE_THETA ** (-jnp.arange(0, QK_ROPE, 2, dtype=f32) / QK_ROPE)
    ang = pos[:, None] * inv_freq[None, :]
    c, s = jnp.cos(ang), jnp.sin(ang)
    zero = jnp.zeros((pos.shape[0], LANE - QK_ROPE), f32)
    return jnp.concatenate([c, c, zero], axis=1), jnp.concatenate([-s, s, zero], axis=1)


def _expand_matrix(width):
    r = np.arange(LANE)[:, None]
    col = np.arange(SSD_HEADS * width)[None, :]
    return jnp.asarray(((r % SSD_HEADS) == (col // width)) & (r < 3 * SSD_HEADS), dtype=bf16)


def _tile(m, pref):
    t = min(pref, m)
    assert m % t == 0
    return t


def kernel(x, meta_tokens, attn_norm, w_in, q_norm, w_uq, kv_norm, w_ukv, conv_w, conv_b, dt_bias, a_log, d_skip,
           ssd_norm, w_out, ffn_norm, w_query, sub_keys, u_experts, v_experts, final_norm):
    bsz, seq, d = x.shape
    assert d == D_MODEL and seq % CHUNK == 0
    n_tok = bsz * seq
    x2d = x.reshape(n_tok, d)

    w_in_b = _prep_w_in(w_in[0])
    wq_b = _prep_w_uq(w_uq[0])
    wkv = w_ukv[0].reshape(KV_LORA, MLA_HEADS, QK_NOPE + V_HEAD)
    wk_b = wkv[..., :QK_NOPE].reshape(KV_LORA, MLA_HEADS * QK_NOPE).astype(bf16)
    wv_b = wkv[..., QK_NOPE:].reshape(KV_LORA, MLA_HEADS * V_HEAD).astype(bf16)
    w_out_b = w_out[0].astype(bf16)
    w_query_b = w_query[0].astype(bf16)
    u_b = u_experts[0].astype(bf16)
    v_b = v_experts[0].astype(bf16)
    cwx, cwbc = conv_w[0][:, :D_INNER], conv_w[0][:, D_INNER:]
    cbx, cbbc = conv_b[0][None, :D_INNER], conv_b[0][None, D_INNER:]
    rep3 = lambda v, fill: jnp.concatenate([v, v, v, jnp.full((LANE - 3 * SSD_HEADS,), fill, f32)])[None, :]
    dtb = rep3(dt_bias[0], 0.0)
    alog = rep3(a_log[0], 0.0)
    dskip = jnp.repeat(d_skip[0], SSD_HEAD_DIM)[None, :]
    e64 = _expand_matrix(SSD_HEAD_DIM)
    e128 = _expand_matrix(CHUNK)

    lead_rows = jnp.concatenate([jnp.zeros((LEAD_PAD, d), x.dtype), meta_tokens.astype(x.dtype)], axis=0)
    cc_l, ss_l = _rope_tables(jnp.arange(LEAD, dtype=f32) - LEAD_PAD)
    cc_r, ss_r = _rope_tables(jnp.arange(seq, dtype=f32) + N_META)

    g_attn = attn_norm[0][None, :]
    proj_l = _inproj(lead_rows, g_attn, w_in_b, tm=LEAD)
    proj_r = _inproj(x2d, g_attn, w_in_b, tm=_tile(n_tok, 512))

    qn, kvn = q_norm[0][None, :], kv_norm[0][None, :]
    _, k_l, v_l = _mla_up(proj_l, cc_l, ss_l, qn, kvn, wq_b, wk_b, wv_b, tm=LEAD, n_pos_blocks=1)
    tmu = _tile(seq, 256)
    q_r, k_r, v_r = _mla_up(proj_r, cc_r, ss_r, qn, kvn, wq_b, wk_b, wv_b, tm=tmu, n_pos_blocks=seq // tmu)
    a_out = _attention(q_r, k_r, v_r, k_l, v_l, bsz=bsz, seq=seq, tq=_tile(seq, 512))

    s_out = _ssd(proj_l, proj_r, cwx, cwbc, cbx, cbbc, dtb, alog, dskip, ssd_norm[0][None, :], e64, e128,
                 bsz=bsz, seq=seq)

    h1 = _outproj(a_out, s_out, w_out_b, x2d, tm=_tile(n_tok, 1024))

    xnt, e1, thr, e2b = _peer_q(h1, ffn_norm[0][None, :], w_query_b, sub_keys[0], tm=_tile(n_tok, 256))
    ffn = _peer_dense(xnt, u_b, v_b, e1, thr, e2b, tm=_tile(n_tok, 512), te=512)
    out = _final(ffn, h1, final_norm[None, :], tm=_tile(n_tok, 256))
    return out.reshape(bsz, seq, d)
```

```python
import functools

import jax
import jax.numpy as jnp
import numpy as np
from jax import lax
from jax.experimental import pallas as pl
from jax.experimental.pallas import tpu as pltpu

f32 = jnp.float32
bf16 = jnp.bfloat16

D_MODEL = 4096
N_META = 16
CHUNK = 128
LEAD = CHUNK
LEAD_PAD = CHUNK - N_META
MLA_HEADS = 16
QK_NOPE = 128
QK_ROPE = 64
V_HEAD = 128
Q_LORA = 1024
KV_LORA = 512
ROPE_THETA = 10000.0
D_INNER = 2048
SSD_HEAD_DIM = 64
SSD_HEADS = 32
SSD_GROUPS = 4
SSD_STATE = 128
CONV_W = 4
PEER_HEADS = 8
PEER_TOPK = 16
N_KEYS = 128
HALF_KEY = 128
EPS = 1e-6
NEG_INF = -1e30

LANE = 128
QK_SLAB = 2 * LANE
GROUP_W = D_INNER // SSD_GROUPS

P_Z = 0
P_XS = 2048
P_CQ = 4096
P_B = 5120
P_C = 5632
P_CKV = 6144
P_KR = 6656
P_DT = 6784
NP = 6912

VMEM_LIMIT = 56 * 1024 * 1024


def _cparams(sem, vmem=VMEM_LIMIT, flags=None):
    return pltpu.CompilerParams(dimension_semantics=sem, vmem_limit_bytes=vmem, flags=flags)


def _dot(a, b):
    return jnp.dot(a, b, preferred_element_type=f32)


def _dot_nt(a, b):
    return lax.dot_general(a, b, (((1,), (1,)), ((), ())), preferred_element_type=f32)


def _rms(x, g):
    return x * lax.rsqrt(jnp.mean(x * x, axis=-1, keepdims=True) + EPS) * g


def _inproj_kernel(x_ref, g_ref, w_ref, o_ref, xn_ref, *, sub):
    @pl.when(pl.program_id(1) == 0)
    def _():
        def body(r, c):
            rows = pl.ds(pl.multiple_of(r * sub, sub), sub)
            xn_ref[rows, :] = _rms(x_ref[rows, :], g_ref[...]).astype(bf16)
            return c
        lax.fori_loop(0, x_ref.shape[0] // sub, body, 0)

    o_ref[...] = _dot(xn_ref[...], w_ref[...])


def _inproj(x2d, g, w, *, tm, tn=768):
    m = x2d.shape[0]
    assert m % tm == 0 and NP % tn == 0
    sub = min(64, tm)
    return pl.pallas_call(
        functools.partial(_inproj_kernel, sub=sub),
        grid=(m // tm, NP // tn),
        in_specs=[pl.BlockSpec((tm, D_MODEL), lambda i, j: (i, 0)),
                  pl.BlockSpec((1, D_MODEL), lambda i, j: (0, 0)),
                  pl.BlockSpec((D_MODEL, tn), lambda i, j: (0, j))],
        out_specs=pl.BlockSpec((tm, tn), lambda i, j: (i, j)),
        out_shape=jax.ShapeDtypeStruct((m, NP), f32),
        scratch_shapes=[pltpu.VMEM((tm, D_MODEL), bf16)],
        compiler_params=_cparams(("parallel", "arbitrary")),
        name="inproj",
    )(x2d, g, w)


def _mla_up_kernel(cq_ref, ckv_ref, kr_ref, cc_ref, ss_ref, qn_ref, kvn_ref, wq_ref, wk_ref, wv_ref,
                   q_ref, k_ref, v_ref, *, scale):
    cc = cc_ref[...]
    ss = ss_ref[...]

    def rope(t):
        return t * cc + pltpu.roll(t, 64, 1) * ss

    cqn = _rms(cq_ref[...], qn_ref[...]).astype(bf16)
    q = _dot(cqn, wq_ref[...])
    for h in range(MLA_HEADS):
        lo = h * QK_SLAB
        q_ref[:, lo:lo + LANE] = (q[:, lo:lo + LANE] * scale).astype(bf16)
        q_ref[:, lo + LANE:lo + QK_SLAB] = (rope(q[:, lo + LANE:lo + QK_SLAB]) * scale).astype(bf16)

    kvn = _rms(ckv_ref[...], kvn_ref[...]).astype(bf16)
    kn = _dot(kvn, wk_ref[...])
    kpe = rope(kr_ref[...]).astype(bf16)
    for h in range(MLA_HEADS):
        lo = h * QK_SLAB
        k_ref[:, lo:lo + LANE] = kn[:, h * LANE:(h + 1) * LANE].astype(bf16)
        k_ref[:, lo + LANE:lo + QK_SLAB] = kpe
    v_ref[...] = _dot(kvn, wv_ref[...]).astype(bf16)


def _mla_up(proj, cc, ss, qn, kvn, wq, wk, wv, *, tm, n_pos_blocks):
    m = proj.shape[0]
    assert m % tm == 0
    scale = float((QK_NOPE + QK_ROPE) ** -0.5 * np.log2(np.e))
    hq = MLA_HEADS * QK_SLAB
    hv = MLA_HEADS * V_HEAD
    const = lambda i: (0, 0)
    return pl.pallas_call(
        functools.partial(_mla_up_kernel, scale=scale),
        grid=(m // tm,),
        in_specs=[pl.BlockSpec((tm, Q_LORA), lambda i: (i, P_CQ // Q_LORA)),
                  pl.BlockSpec((tm, KV_LORA), lambda i: (i, P_CKV // KV_LORA)),
                  pl.BlockSpec((tm, LANE), lambda i: (i, P_KR // LANE)),
                  pl.BlockSpec((tm, LANE), lambda i: (i % n_pos_blocks, 0)),
                  pl.BlockSpec((tm, LANE), lambda i: (i % n_pos_blocks, 0)),
                  pl.BlockSpec((1, Q_LORA), const),
                  pl.BlockSpec((1, KV_LORA), const),
                  pl.BlockSpec((Q_LORA, hq), const),
                  pl.BlockSpec((KV_LORA, hv), const),
                  pl.BlockSpec((KV_LORA, hv), const)],
        out_specs=[pl.BlockSpec((tm, hq), lambda i: (i, 0)),
                   pl.BlockSpec((tm, hq), lambda i: (i, 0)),
                   pl.BlockSpec((tm, hv), lambda i: (i, 0))],
        out_shape=[jax.ShapeDtypeStruct((m, hq), bf16),
                   jax.ShapeDtypeStruct((m, hq), bf16),
                   jax.ShapeDtypeStruct((m, hv), bf16)],
        compiler_params=_cparams(("parallel",)),
        name="mla_up",
    )(proj, proj, proj, cc, ss, qn, kvn, wq, wk, wv)


def _attn_kernel(q_ref, k_ref, v_ref, kl_ref, vl_ref, o_ref, *, tq, nh):
    i = pl.program_id(2)
    qs = [q_ref[:, h * QK_SLAB:(h + 1) * QK_SLAB] for h in range(nh)]
    ksl = lambda h: slice(h * QK_SLAB, (h + 1) * QK_SLAB)
    vsl = lambda h: slice(h * V_HEAD, (h + 1) * V_HEAD)

    diag = pl.ds(pl.multiple_of(i * tq, tq), tq)
    col = lax.broadcasted_iota(jnp.int32, (tq, LEAD), 1)
    r_id = lax.broadcasted_iota(jnp.int32, (tq, tq), 0)
    c_id = lax.broadcasted_iota(jnp.int32, (tq, tq), 1)
    carry = []
    for h in range(nh):
        s_l = jnp.where(col >= LEAD_PAD, _dot_nt(qs[h], kl_ref[:, ksl(h)]), NEG_INF)
        s_d = jnp.where(c_id <= r_id, _dot_nt(qs[h], k_ref[diag, ksl(h)]), NEG_INF)
        m = jnp.maximum(jnp.max(s_l, axis=-1, keepdims=True), jnp.max(s_d, axis=-1, keepdims=True))
        p_l = jnp.exp2(s_l - m)
        p_d = jnp.exp2(s_d - m)
        l = jnp.sum(p_l, axis=-1, keepdims=True) + jnp.sum(p_d, axis=-1, keepdims=True)
        acc = _dot(p_l.astype(bf16), vl_ref[:, vsl(h)]) + _dot(p_d.astype(bf16), v_ref[diag, vsl(h)])
        carry += [m, l, acc]

    def update(s, vb, m, l, acc):
        m_new = jnp.maximum(m, jnp.max(s, axis=-1, keepdims=True))
        alpha = jnp.exp2(m - m_new)
        p = jnp.exp2(s - m_new)
        l = alpha * l + jnp.sum(p, axis=-1, keepdims=True)
        acc = alpha * acc + _dot(p.astype(bf16), vb)
        return [m_new, l, acc]

    def body(j, carry):
        rows = pl.ds(pl.multiple_of(j * tq, tq), tq)
        out = []
        for h in range(nh):
            out += update(_dot_nt(qs[h], k_ref[rows, ksl(h)]), v_ref[rows, vsl(h)], *carry[3 * h:3 * h + 3])
        return tuple(out)

    carry = lax.fori_loop(0, i, body, tuple(carry))
    for h in range(nh):
        o_ref[:, vsl(h)] = (carry[3 * h + 2] / carry[3 * h + 1]).astype(bf16)


def _attention(q, k, v, kl, vl, *, bsz, seq, tq, nh=2):
    assert seq % tq == 0 and MLA_HEADS % nh == 0
    nq = seq // tq
    kw, vw = nh * QK_SLAB, nh * V_HEAD
    return pl.pallas_call(
        functools.partial(_attn_kernel, tq=tq, nh=nh),
        grid=(bsz, MLA_HEADS // nh, nq),
        in_specs=[pl.BlockSpec((tq, kw), lambda b, h, i: (b * nq + i, h)),
                  pl.BlockSpec((seq, kw), lambda b, h, i: (b, h)),
                  pl.BlockSpec((seq, vw), lambda b, h, i: (b, h)),
                  pl.BlockSpec((LEAD, kw), lambda b, h, i: (0, h)),
                  pl.BlockSpec((LEAD, vw), lambda b, h, i: (0, h))],
        out_specs=pl.BlockSpec((tq, vw), lambda b, h, i: (b * nq + i, h)),
        out_shape=jax.ShapeDtypeStruct((bsz * seq, MLA_HEADS * V_HEAD), bf16),
        compiler_params=_cparams(("parallel", "parallel", "arbitrary")),
        name="mla_attn",
    )(q, k, v, kl, vl)


def _split3(v):
    b0 = v.astype(bf16).astype(f32)
    r1 = v - b0
    b1 = r1.astype(bf16).astype(f32)
    b2 = (r1 - b1).astype(bf16).astype(f32)
    return b0, b1, b2


def _ssd_kernel(zx_l, bc_l, dt_l, zx_r, bc_r, dt_r, cwx, cwbc, cbx, cbbc, dtb, alog, dskip, ng, e64, e128,
                o_ref, st_ref, px_ref, pbc_ref):
    c = pl.program_id(1)
    is_lead = c == 0

    @pl.when(is_lead)
    def _():
        st_ref[...] = jnp.zeros_like(st_ref)
        px_ref[...] = jnp.zeros_like(px_ref)
        pbc_ref[...] = jnp.zeros_like(pbc_ref)

    zx = jnp.where(is_lead, zx_l[...], zx_r[...])
    bcraw = jnp.where(is_lead, bc_l[...], bc_r[...])
    dtraw = jnp.where(is_lead, dt_l[...], dt_r[...])
    z = zx[:, :D_INNER]
    xraw = zx[:, D_INNER:]

    row1 = lax.broadcasted_iota(jnp.int32, (CHUNK, 1), 0)
    r_id = lax.broadcasted_iota(jnp.int32, (CHUNK, CHUNK), 0)
    c_id = lax.broadcasted_iota(jnp.int32, (CHUNK, CHUNK), 1)
    tril = r_id >= c_id

    def conv_silu(cur, prev_ref, w_ref, b_ref):
        prev = prev_ref[...]
        acc = cur * w_ref[CONV_W - 1:CONV_W, :] + b_ref[...]
        for k in range(1, CONV_W):
            comb = jnp.where(row1 >= CHUNK - k, prev, cur)
            acc = acc + pltpu.roll(comb, k, 0) * w_ref[CONV_W - 1 - k:CONV_W - k, :]
        prev_ref[...] = cur
        return acc * jax.nn.sigmoid(acc)

    xs = conv_silu(xraw, px_ref, cwx, cbx)
    bcc = conv_silu(bcraw, pbc_ref, cwbc, cbbc)
    b_all = bcc[:, :SSD_GROUPS * SSD_STATE]
    c_all = bcc[:, SSD_GROUPS * SSD_STATE:].astype(bf16)

    pre = dtraw + dtb[...]
    dt = jnp.maximum(pre, 0.0) + jnp.log1p(jnp.exp(-jnp.abs(pre)))
    dt = jnp.where(jnp.logical_and(is_lead, row1 < LEAD_PAD), 0.0, dt)
    da = dt * (-jnp.exp(alog[...]))
    tri = tril.astype(f32).astype(bf16)
    d0, d1, d2 = _split3(da)
    cs = _dot(tri, d0.astype(bf16)) + _dot(tri, d1.astype(bf16)) + _dot(tri, d2.astype(bf16))
    cs_t = cs.T
    ecs = jnp.exp(cs)
    dec = jnp.exp(cs[CHUNK - 1:CHUNK, :] - cs)

    def kcat(v):
        b0, b1, b2 = _split3(v)
        return jnp.where(c_id < 32, b0, jnp.where(c_id < 64, b1, jnp.where(c_id < 96, b2, 0.0))).astype(bf16)

    ex = _dot(jnp.concatenate([kcat(dt), kcat(ecs), kcat(dec)], axis=0), e64[...])
    dt_x = ex[0:CHUNK]
    ecs_x = ex[CHUNK:2 * CHUNK]
    dec_x = ex[2 * CHUNK:3 * CHUNK]
    cs_bc = _dot(kcat(cs), e128[...])

    xdt = xs * dt_x
    xdt_b = xdt.astype(bf16)
    xd_b = (xdt * dec_x).astype(bf16)

    ys = []
    for g in range(SSD_GROUPS):
        gl = slice(g * GROUP_W, (g + 1) * GROUP_W)
        b_g = b_all[:, g * SSD_STATE:(g + 1) * SSD_STATE]
        c_g = c_all[:, g * SSD_STATE:(g + 1) * SSD_STATE]
        cb = _dot_nt(c_g, b_g.astype(bf16))
        st_g = st_ref[:, gl]
        y_g = _dot(c_g, st_g.astype(bf16)) * ecs_x[:, gl]
        pieces = []
        for pr in range(GROUP_W // LANE):
            h0 = g * (GROUP_W // SSD_HEAD_DIM) + 2 * pr
            xpair = xdt_b[:, h0 * SSD_HEAD_DIM:(h0 + 2) * SSD_HEAD_DIM]
            yh = []
            for h in (h0, h0 + 1):
                seg = cs_bc[:, h * CHUNK:(h + 1) * CHUNK] - cs_t[h:h + 1, :]
                lm = jnp.where(tril, jnp.exp(jnp.minimum(seg, 0.0)), 0.0)
                yh.append(_dot((cb * lm).astype(bf16), xpair))
            pieces.append(jnp.where(c_id < SSD_HEAD_DIM, yh[0], yh[1]))
        y_g = y_g + jnp.concatenate(pieces, axis=1)
        st_ref[:, gl] = st_g * ecs_x[CHUNK - 1:CHUNK, gl] + _dot(b_g.T.astype(bf16), xd_b[:, gl])
        ys.append(y_g)

    y = jnp.concatenate(ys, axis=1) + xs * dskip[...]
    gated = y * (z * jax.nn.sigmoid(z))
    outs = []
    for g in range(SSD_GROUPS):
        gg = gated[:, g * GROUP_W:(g + 1) * GROUP_W]
        outs.append(gg * lax.rsqrt(jnp.mean(gg * gg, axis=-1, keepdims=True) + EPS))
    o_ref[...] = (jnp.concatenate(outs, axis=1) * ng[...]).astype(bf16)


def _ssd(proj_l, proj_r, cwx, cwbc, cbx, cbbc, dtb, alog, dskip, ng, e64, e128, *, bsz, seq):
    nch = seq // CHUNK
    real = lambda w: (lambda b, c: (b * nch + jnp.maximum(c - 1, 0), w))
    lead = lambda w: (lambda b, c: (0, w))
    const = lambda b, c: (0, 0)
    zx_w, bc_w, dt_w = 2 * D_INNER, 2 * SSD_GROUPS * SSD_STATE, LANE
    return pl.pallas_call(
        _ssd_kernel,
        grid=(bsz, nch + 1),
        in_specs=[pl.BlockSpec((CHUNK, zx_w), lead(P_Z // zx_w)),
                  pl.BlockSpec((CHUNK, bc_w), lead(P_B // bc_w)),
                  pl.BlockSpec((CHUNK, dt_w), lead(P_DT // dt_w)),
                  pl.BlockSpec((CHUNK, zx_w), real(P_Z // zx_w)),
                  pl.BlockSpec((CHUNK, bc_w), real(P_B // bc_w)),
                  pl.BlockSpec((CHUNK, dt_w), real(P_DT // dt_w)),
                  pl.BlockSpec((CONV_W, D_INNER), const),
                  pl.BlockSpec((CONV_W, bc_w), const),
                  pl.BlockSpec((1, D_INNER), const),
                  pl.BlockSpec((1, bc_w), const),
                  pl.BlockSpec((1, LANE), const),
                  pl.BlockSpec((1, LANE), const),
                  pl.BlockSpec((1, D_INNER), const),
                  pl.BlockSpec((1, D_INNER), const),
                  pl.BlockSpec((LANE, D_INNER), const),
                  pl.BlockSpec((LANE, SSD_HEADS * CHUNK), const)],
        out_specs=pl.BlockSpec((CHUNK, D_INNER), real(0)),
        out_shape=jax.ShapeDtypeStruct((bsz * seq, D_INNER), bf16),
        scratch_shapes=[pltpu.VMEM((SSD_STATE, D_INNER), f32),
                        pltpu.VMEM((CHUNK, D_INNER), f32),
                        pltpu.VMEM((CHUNK, bc_w), f32)],
        compiler_params=_cparams(("parallel", "arbitrary")),
        name="ssd",
    )(proj_l, proj_l, proj_l, proj_r, proj_r, proj_r, cwx, cwbc, cbx, cbbc, dtb, alog, dskip, ng, e64, e128)


def _outproj_kernel(a_ref, s_ref, wa_ref, ws_ref, x_ref, o_ref):
    o_ref[...] = _dot(a_ref[...], wa_ref[...]) + _dot(s_ref[...], ws_ref[...]) + x_ref[...]


def _outproj(a, s, w, x2d, *, tm, tn=512):
    m = a.shape[0]
    ka = a.shape[1]
    assert m % tm == 0 and D_MODEL % tn == 0
    return pl.pallas_call(
        _outproj_kernel,
        grid=(m // tm, D_MODEL // tn),
        in_specs=[pl.BlockSpec((tm, ka), lambda i, j: (i, 0)),
                  pl.BlockSpec((tm, D_INNER), lambda i, j: (i, 0)),
                  pl.BlockSpec((ka, tn), lambda i, j: (0, j)),
                  pl.BlockSpec((D_INNER, tn), lambda i, j: (1, j)),
                  pl.BlockSpec((tm, tn), lambda i, j: (i, j))],
        out_specs=pl.BlockSpec((tm, tn), lambda i, j: (i, j)),
        out_shape=jax.ShapeDtypeStruct((m, D_MODEL), f32),
        compiler_params=_cparams(("parallel", "arbitrary")),
        name="outproj",
    )(a, s, w, w, x2d)


def _extract_topk(x, k):
    vals = []
    for _ in range(k):
        m = jnp.max(x, axis=0, keepdims=True)
        x = jnp.where(x == m, -jnp.inf, x)
        vals.append(m)
    return vals


def _stack_rows(vals):
    n = len(vals)
    rid = lax.broadcasted_iota(jnp.int32, (n, vals[0].shape[1]), 0)
    out = jnp.zeros((n, vals[0].shape[1]), f32)
    for r, v in enumerate(vals):
        out = jnp.where(rid == r, v, out)
    return out


N_CAND = PEER_TOPK + (PEER_TOPK // 2 - 1) * (PEER_TOPK // 2) + PEER_TOPK


def _peer_q_kernel(h_ref, g_ref, wq_ref, sk_ref, xnt_ref, e1_ref, thr_ref, e2_ref, cand_ref):
    xn = _rms(h_ref[...], g_ref[...])
    xnt_ref[...] = xn.T.astype(bf16)
    q = _dot(xn.astype(bf16), wq_ref[...]).astype(bf16)
    sk1 = sk_ref[0].astype(bf16)
    sk2 = sk_ref[1].astype(bf16)
    k = PEER_TOPK
    half = k // 2
    for h in range(PEER_HEADS):
        lo = h * 2 * HALF_KEY
        s1 = _dot_nt(sk1, q[:, lo:lo + HALF_KEY])
        s2 = _dot_nt(sk2, q[:, lo + HALF_KEY:lo + 2 * HALF_KEY])
        v1 = _extract_topk(s1, k + 1)
        v2 = _extract_topk(s2, k + 1)
        v2_top = _stack_rows(v2[:k])
        cand_ref[0:k, :] = v1[0] + v2_top
        for a in range(1, half):
            cand_ref[k + (a - 1) * half:k + a * half, :] = v1[a] + v2_top[0:half, :]
        cand_ref[k + (half - 1) * half:k + half * half, :] = _stack_rows(v1[half:k]) + v2[0]
        ninf = jnp.full_like(v1[0], -jnp.inf)
        cand_ref[k + half * half:, :] = _stack_rows([v1[0] + v2[k], v1[k] + v2[0]] + [ninf] * (half - 2))
        top = _extract_topk(cand_ref[...], k + 1)
        zsum = jnp.zeros_like(top[0])
        for t in top[:k]:
            zsum = zsum + jnp.exp(t - top[0])
        tau = 0.5 * (top[k - 1] + top[k])
        scale = 0.5 / zsum
        e1_ref[h] = jnp.exp(s1 - v1[0])
        thr_ref[h] = jnp.exp((tau - v2[0]) - s1) * scale
        e2_ref[h] = pltpu.bitcast((jnp.exp(s2 - v2[0]) * scale).astype(bf16), jnp.uint32)


def _peer_q(h1, g, wq, sub_keys, *, tm):
    n = h1.shape[0]
    assert n % tm == 0
    nq = PEER_HEADS * 2 * HALF_KEY
    per_head = lambda rows, dt: (pl.BlockSpec((PEER_HEADS, rows, tm), lambda i: (0, 0, i)),
                                 jax.ShapeDtypeStruct((PEER_HEADS, rows, n), dt))
    specs, shapes = zip((pl.BlockSpec((D_MODEL, tm), lambda i: (0, i)), jax.ShapeDtypeStruct((D_MODEL, n), bf16)),
                        per_head(N_KEYS, f32), per_head(N_KEYS, f32), per_head(N_KEYS // 2, jnp.uint32))
    return pl.pallas_call(
        _peer_q_kernel,
        grid=(n // tm,),
        in_specs=[pl.BlockSpec((tm, D_MODEL), lambda i: (i, 0)),
                  pl.BlockSpec((1, D_MODEL), lambda i: (0, 0)),
                  pl.BlockSpec((D_MODEL, nq), lambda i: (0, 0)),
                  pl.BlockSpec((2, N_KEYS, HALF_KEY), lambda i: (0, 0, 0))],
        out_specs=list(specs),
        out_shape=list(shapes),
        scratch_shapes=[pltpu.VMEM((N_CAND, tm), f32)],
        compiler_params=_cparams(("parallel",)),
        name="peer_query",
    )(h1, g, wq, sub_keys)


def _peer_dense_kernel(xnt_ref, u_ref, v_ref, e1_ref, thr_ref, e2_ref, o_ref, ht_ref, act_ref, *, te):
    j = pl.program_id(1)
    tm = xnt_ref.shape[1]
    pk = N_KEYS // 2

    @pl.when(j == 0)
    def _():
        o_ref[...] = jnp.zeros_like(o_ref)

    ht_ref[...] = _dot(u_ref[...], xnt_ref[...])
    for ii in range(te // N_KEYS):
        i1 = j * (te // N_KEYS) + ii
        e1rows = [e1_ref[h, pl.ds(i1, 1), :] for h in range(PEER_HEADS)]
        thrrows = [thr_ref[h, pl.ds(i1, 1), :] for h in range(PEER_HEADS)]
        for c in range(tm // LANE):
            cl = slice(c * LANE, (c + 1) * LANE)
            w = jnp.zeros((N_KEYS, LANE), bf16)
            for h in range(PEER_HEADS):
                e1b = jnp.broadcast_to(e1rows[h][:, cl], (N_KEYS, LANE)).astype(bf16)
                thrb = jnp.broadcast_to(thrrows[h][:, cl], (N_KEYS, LANE)).astype(bf16)
                e2 = pltpu.bitcast(e2_ref[h, :, cl], bf16)
                w = w + jnp.where(e2 >= thrb, e2, jnp.zeros_like(e2)) * e1b
            hh = ht_ref[ii * N_KEYS:(ii + 1) * N_KEYS, cl]
            g = hh * (1.0 + lax.erf(hh * np.float32(1.0 / np.sqrt(2.0))))
            act_ref[ii * pk:(ii + 1) * pk, cl] = pltpu.bitcast(g.astype(bf16) * w, jnp.uint32)
    o_ref[...] += _dot(pltpu.bitcast(act_ref[...], bf16).T, v_ref[...])


def _peer_dense(xnt, u_b, v_b, e1, thr, e2b, *, tm, te):
    n = xnt.shape[1]
    n_exp = u_b.shape[0]
    assert n % tm == 0 and n_exp % te == 0 and te % N_KEYS == 0 and tm % LANE == 0
    once = pl.Buffered(1)
    return pl.pallas_call(
        functools.partial(_peer_dense_kernel, te=te),
        grid=(n // tm, n_exp // te),
        in_specs=[pl.BlockSpec((D_MODEL, tm), lambda i, j: (0, i), pipeline_mode=once),
                  pl.BlockSpec((te, D_MODEL), lambda i, j: (j, 0)),
                  pl.BlockSpec((te, D_MODEL), lambda i, j: (j, 0)),
                  pl.BlockSpec((PEER_HEADS, N_KEYS, tm), lambda i, j: (0, 0, i), pipeline_mode=once),
                  pl.BlockSpec((PEER_HEADS, N_KEYS, tm), lambda i, j: (0, 0, i), pipeline_mode=once),
                  pl.BlockSpec((PEER_HEADS, N_KEYS // 2, tm), lambda i, j: (0, 0, i), pipeline_mode=once)],
        out_specs=pl.BlockSpec((tm, D_MODEL), lambda i, j: (i, 0)),
        out_shape=jax.ShapeDtypeStruct((n, D_MODEL), f32),
        scratch_shapes=[pltpu.VMEM((te, tm), f32),
                        pltpu.VMEM((te // 2, tm), jnp.uint32)],
        compiler_params=_cparams(("parallel", "arbitrary")),
        name="peer_dense",
    )(xnt, u_b, v_b, e1, thr, e2b)


def _final_kernel(f_ref, h_ref, g_ref, o_ref):
    o_ref[...] = _rms(h_ref[...] + f_ref[...], g_ref[...])


def _final(ffn, h1, g, *, tm):
    n = h1.shape[0]
    assert n % tm == 0
    return pl.pallas_call(
        _final_kernel,
        grid=(n // tm,),
        in_specs=[pl.BlockSpec((tm, D_MODEL), lambda i: (i, 0)),
                  pl.BlockSpec((tm, D_MODEL), lambda i: (i, 0)),
                  pl.BlockSpec((1, D_MODEL), lambda i: (0, 0))],
        out_specs=pl.BlockSpec((tm, D_MODEL), lambda i: (i, 0)),
        out_shape=jax.ShapeDtypeStruct((n, D_MODEL), f32),
        compiler_params=_cparams(("parallel",)),
        name="final_norm",
    )(ffn, h1, g)


def _prep_w_in(w):
    cq, ckv, kr = w[:, 0:1024], w[:, 1024:1536], w[:, 1536:1600]
    z, xs = w[:, 1600:3648], w[:, 3648:5696]
    bm, cm, dt = w[:, 5696:6208], w[:, 6208:6720], w[:, 6720:6752]
    kr_sw = jnp.concatenate([kr[:, 32:], kr[:, :32]], axis=1)
    pad = jnp.zeros((w.shape[0], LANE - 3 * SSD_HEADS), w.dtype)
    return jnp.concatenate([z, xs, cq, bm, cm, ckv, kr, kr_sw, dt, dt, dt, pad], axis=1).astype(bf16)


def _prep_w_uq(w):
    w = w.reshape(Q_LORA, MLA_HEADS, QK_NOPE + QK_ROPE)
    nope, x1, x2 = w[..., :QK_NOPE], w[..., QK_NOPE:QK_NOPE + 32], w[..., QK_NOPE + 32:]
    return jnp.concatenate([nope, x1, x2, x2, x1], axis=-1).reshape(Q_LORA, MLA_HEADS * QK_SLAB).astype(bf16)


def _rope_tables(pos):
    inv_freq = ROPE_THETA ** (-jnp.arange(0, QK_ROPE, 2, dtype=f32) / QK_ROPE)
    ang = pos[:, None] * inv_freq[None, :]
    c, s = jnp.cos(ang), jnp.sin(ang)
    zero = jnp.zeros((pos.shape[0], LANE - QK_ROPE), f32)
    return jnp.concatenate([c, c, zero], axis=1), jnp.concatenate([-s, s, zero], axis=1)


def _expand_matrix(width):
    r = np.arange(LANE)[:, None]
    col = np.arange(SSD_HEADS * width)[None, :]
    return jnp.asarray(((r % SSD_HEADS) == (col // width)) & (r < 3 * SSD_HEADS), dtype=bf16)


def _tile(m, pref):
    t = min(pref, m)
    assert m % t == 0
    return t


def kernel(x, meta_tokens, attn_norm, w_in, q_norm, w_uq, kv_norm, w_ukv, conv_w, conv_b, dt_bias, a_log, d_skip,
           ssd_norm, w_out, ffn_norm, w_query, sub_keys, u_experts, v_experts, final_norm):
    bsz, seq, d = x.shape
    assert d == D_MODEL and seq % CHUNK == 0
    n_tok = bsz * seq
    x2d = x.reshape(n_tok, d)

    w_in_b = _prep_w_in(w_in[0])
    wq_b = _prep_w_uq(w_uq[0])
    wkv = w_ukv[0].reshape(KV_LORA, MLA_HEADS, QK_NOPE + V_HEAD)
    wk_b = wkv[..., :QK_NOPE].reshape(KV_LORA, MLA_HEADS * QK_NOPE).astype(bf16)
    wv_b = wkv[..., QK_NOPE:].reshape(KV_LORA, MLA_HEADS * V_HEAD).astype(bf16)
    w_out_b = w_out[0].astype(bf16)
    w_query_b = w_query[0].astype(bf16)
    u_b = u_experts[0].astype(bf16)
    v_b = v_experts[0].astype(bf16)
    cwx, cwbc = conv_w[0][:, :D_INNER], conv_w[0][:, D_INNER:]
    cbx, cbbc = conv_b[0][None, :D_INNER], conv_b[0][None, D_INNER:]
    rep3 = lambda v, fill: jnp.concatenate([v, v, v, jnp.full((LANE - 3 * SSD_HEADS,), fill, f32)])[None, :]
    dtb = rep3(dt_bias[0], 0.0)
    alog = rep3(a_log[0], 0.0)
    dskip = jnp.repeat(d_skip[0], SSD_HEAD_DIM)[None, :]
    e64 = _expand_matrix(SSD_HEAD_DIM)
    e128 = _expand_matrix(CHUNK)

    lead_rows = jnp.concatenate([jnp.zeros((LEAD_PAD, d), x.dtype), meta_tokens.astype(x.dtype)], axis=0)
    cc_l, ss_l = _rope_tables(jnp.arange(LEAD, dtype=f32) - LEAD_PAD)
    cc_r, ss_r = _rope_tables(jnp.arange(seq, dtype=f32) + N_META)

    g_attn = attn_norm[0][None, :]
    proj_l = _inproj(lead_rows, g_attn, w_in_b, tm=LEAD)
    proj_r = _inproj(x2d, g_attn, w_in_b, tm=_tile(n_tok, 512))

    qn, kvn = q_norm[0][None, :], kv_norm[0][None, :]
    _, k_l, v_l = _mla_up(proj_l, cc_l, ss_l, qn, kvn, wq_b, wk_b, wv_b, tm=LEAD, n_pos_blocks=1)
    tmu = _tile(seq, 256)
    q_r, k_r, v_r = _mla_up(proj_r, cc_r, ss_r, qn, kvn, wq_b, wk_b, wv_b, tm=tmu, n_pos_blocks=seq // tmu)
    a_out = _attention(q_r, k_r, v_r, k_l, v_l, bsz=bsz, seq=seq, tq=_tile(seq, 512))

    s_out = _ssd(proj_l, proj_r, cwx, cwbc, cbx, cbbc, dtb, alog, dskip, ssd_norm[0][None, :], e64, e128,
                 bsz=bsz, seq=seq)

    h1 = _outproj(a_out, s_out, w_out_b, x2d, tm=_tile(n_tok, 1024))

    xnt, e1, thr, e2b = _peer_q(h1, ffn_norm[0][None, :], w_query_b, sub_keys[0], tm=_tile(n_tok, 256))
    ffn = _peer_dense(xnt, u_b, v_b, e1, thr, e2b, tm=_tile(n_tok, 512), te=512)
    out = _final(ffn, h1, final_norm[None, :], tm=_tile(n_tok, 256))
    return out.reshape(bsz, seq, d)
```

```python
import functools

import jax
import jax.numpy as jnp
import numpy as np
from jax import lax
from jax.experimental import pallas as pl
from jax.experimental.pallas import tpu as pltpu

f32 = jnp.float32
bf16 = jnp.bfloat16

D_MODEL = 4096
N_META = 16
CHUNK = 128
LEAD = CHUNK
LEAD_PAD = CHUNK - N_META
MLA_HEADS = 16
QK_NOPE = 128
QK_ROPE = 64
V_HEAD = 128
Q_LORA = 1024
KV_LORA = 512
ROPE_THETA = 10000.0
D_INNER = 2048
SSD_HEAD_DIM = 64
SSD_HEADS = 32
SSD_GROUPS = 4
SSD_STATE = 128
CONV_W = 4
PEER_HEADS = 8
PEER_TOPK = 16
N_KEYS = 128
HALF_KEY = 128
EPS = 1e-6
NEG_INF = -1e30

LANE = 128
QK_SLAB = 2 * LANE
GROUP_W = D_INNER // SSD_GROUPS

P_Z = 0
P_XS = 2048
P_CQ = 4096
P_B = 5120
P_C = 5632
P_CKV = 6144
P_KR = 6656
P_DT = 6784
NP = 6912

VMEM_LIMIT = 56 * 1024 * 1024


def _cparams(sem, vmem=VMEM_LIMIT, flags=None):
    return pltpu.CompilerParams(dimension_semantics=sem, vmem_limit_bytes=vmem, flags=flags)


def _dot(a, b):
    return jnp.dot(a, b, preferred_element_type=f32)


def _dot_nt(a, b):
    return lax.dot_general(a, b, (((1,), (1,)), ((), ())), preferred_element_type=f32)


def _rms(x, g):
    return x * lax.rsqrt(jnp.mean(x * x, axis=-1, keepdims=True) + EPS) * g


def _inproj_kernel(x_ref, g_ref, w_ref, o_ref, xn_ref, *, sub):
    @pl.when(pl.program_id(1) == 0)
    def _():
        def body(r, c):
            rows = pl.ds(pl.multiple_of(r * sub, sub), sub)
            xn_ref[rows, :] = _rms(x_ref[rows, :], g_ref[...]).astype(bf16)
            return c
        lax.fori_loop(0, x_ref.shape[0] // sub, body, 0)

    o_ref[...] = _dot(xn_ref[...], w_ref[...])


def _inproj(x2d, g, w, *, tm, tn=768):
    m = x2d.shape[0]
    assert m % tm == 0 and NP % tn == 0
    sub = min(64, tm)
    return pl.pallas_call(
        functools.partial(_inproj_kernel, sub=sub),
        grid=(m // tm, NP // tn),
        in_specs=[pl.BlockSpec((tm, D_MODEL), lambda i, j: (i, 0)),
                  pl.BlockSpec((1, D_MODEL), lambda i, j: (0, 0)),
                  pl.BlockSpec((D_MODEL, tn), lambda i, j: (0, j))],
        out_specs=pl.BlockSpec((tm, tn), lambda i, j: (i, j)),
        out_shape=jax.ShapeDtypeStruct((m, NP), f32),
        scratch_shapes=[pltpu.VMEM((tm, D_MODEL), bf16)],
        compiler_params=_cparams(("parallel", "arbitrary")),
        name="inproj",
    )(x2d, g, w)


def _mla_up_kernel(cq_ref, ckv_ref, kr_ref, cc_ref, ss_ref, qn_ref, kvn_ref, wq_ref, wk_ref, wv_ref,
                   q_ref, k_ref, v_ref, *, scale):
    cc = cc_ref[...]
    ss = ss_ref[...]

    def rope(t):
        return t * cc + pltpu.roll(t, 64, 1) * ss

    cqn = _rms(cq_ref[...], qn_ref[...]).astype(bf16)
    q = _dot(cqn, wq_ref[...])
    for h in range(MLA_HEADS):
        lo = h * QK_SLAB
        q_ref[:, lo:lo + LANE] = (q[:, lo:lo + LANE] * scale).astype(bf16)
        q_ref[:, lo + LANE:lo + QK_SLAB] = (rope(q[:, lo + LANE:lo + QK_SLAB]) * scale).astype(bf16)

    kvn = _rms(ckv_ref[...], kvn_ref[...]).astype(bf16)
    kn = _dot(kvn, wk_ref[...])
    kpe = rope(kr_ref[...]).astype(bf16)
    for h in range(MLA_HEADS):
        lo = h * QK_SLAB
        k_ref[:, lo:lo + LANE] = kn[:, h * LANE:(h + 1) * LANE].astype(bf16)
        k_ref[:, lo + LANE:lo + QK_SLAB] = kpe
    v_ref[...] = _dot(kvn, wv_ref[...]).astype(bf16)


def _mla_up(proj, cc, ss, qn, kvn, wq, wk, wv, *, tm, n_pos_blocks):
    m = proj.shape[0]
    assert m % tm == 0
    scale = float((QK_NOPE + QK_ROPE) ** -0.5 * np.log2(np.e))
    hq = MLA_HEADS * QK_SLAB
    hv = MLA_HEADS * V_HEAD
    const = lambda i: (0, 0)
    return pl.pallas_call(
        functools.partial(_mla_up_kernel, scale=scale),
        grid=(m // tm,),
        in_specs=[pl.BlockSpec((tm, Q_LORA), lambda i: (i, P_CQ // Q_LORA)),
                  pl.BlockSpec((tm, KV_LORA), lambda i: (i, P_CKV // KV_LORA)),
                  pl.BlockSpec((tm, LANE), lambda i: (i, P_KR // LANE)),
                  pl.BlockSpec((tm, LANE), lambda i: (i % n_pos_blocks, 0)),
                  pl.BlockSpec((tm, LANE), lambda i: (i % n_pos_blocks, 0)),
                  pl.BlockSpec((1, Q_LORA), const),
                  pl.BlockSpec((1, KV_LORA), const),
                  pl.BlockSpec((Q_LORA, hq), const),
                  pl.BlockSpec((KV_LORA, hv), const),
                  pl.BlockSpec((KV_LORA, hv), const)],
        out_specs=[pl.BlockSpec((tm, hq), lambda i: (i, 0)),
                   pl.BlockSpec((tm, hq), lambda i: (i, 0)),
                   pl.BlockSpec((tm, hv), lambda i: (i, 0))],
        out_shape=[jax.ShapeDtypeStruct((m, hq), bf16),
                   jax.ShapeDtypeStruct((m, hq), bf16),
                   jax.ShapeDtypeStruct((m, hv), bf16)],
        compiler_params=_cparams(("parallel",)),
        name="mla_up",
    )(proj, proj, proj, cc, ss, qn, kvn, wq, wk, wv)


def _attn_kernel(q_ref, k_ref, v_ref, kl_ref, vl_ref, o_ref, *, tq, nh):
    i = pl.program_id(2)
    qs = [q_ref[:, h * QK_SLAB:(h + 1) * QK_SLAB] for h in range(nh)]
    ksl = lambda h: slice(h * QK_SLAB, (h + 1) * QK_SLAB)
    vsl = lambda h: slice(h * V_HEAD, (h + 1) * V_HEAD)

    diag = pl.ds(pl.multiple_of(i * tq, tq), tq)
    col = lax.broadcasted_iota(jnp.int32, (tq, LEAD), 1)
    r_id = lax.broadcasted_iota(jnp.int32, (tq, tq), 0)
    c_id = lax.broadcasted_iota(jnp.int32, (tq, tq), 1)
    carry = []
    for h in range(nh):
        s_l = jnp.where(col >= LEAD_PAD, _dot_nt(qs[h], kl_ref[:, ksl(h)]), NEG_INF)
        s_d = jnp.where(c_id <= r_id, _dot_nt(qs[h], k_ref[diag, ksl(h)]), NEG_INF)
        m = jnp.maximum(jnp.max(s_l, axis=-1, keepdims=True), jnp.max(s_d, axis=-1, keepdims=True))
        p_l = jnp.exp2(s_l - m)
        p_d = jnp.exp2(s_d - m)
        l = jnp.sum(p_l, axis=-1, keepdims=True) + jnp.sum(p_d, axis=-1, keepdims=True)
        acc = _dot(p_l.astype(bf16), vl_ref[:, vsl(h)]) + _dot(p_d.astype(bf16), v_ref[diag, vsl(h)])
        carry += [m, l, acc]

    def update(s, vb, m, l, acc):
        m_new = jnp.maximum(m, jnp.max(s, axis=-1, keepdims=True))
        alpha = jnp.exp2(m - m_new)
        p = jnp.exp2(s - m_new)
        l = alpha * l + jnp.sum(p, axis=-1, keepdims=True)
        acc = alpha * acc + _dot(p.astype(bf16), vb)
        return [m_new, l, acc]

    def body(j, carry):
        rows = pl.ds(pl.multiple_of(j * tq, tq), tq)
        out = []
        for h in range(nh):
            out += update(_dot_nt(qs[h], k_ref[rows, ksl(h)]), v_ref[rows, vsl(h)], *carry[3 * h:3 * h + 3])
        return tuple(out)

    carry = lax.fori_loop(0, i, body, tuple(carry))
    for h in range(nh):
        o_ref[:, vsl(h)] = (carry[3 * h + 2] / carry[3 * h + 1]).astype(bf16)


def _attention(q, k, v, kl, vl, *, bsz, seq, tq, nh=2):
    assert seq % tq == 0 and MLA_HEADS % nh == 0
    nq = seq // tq
    kw, vw = nh * QK_SLAB, nh * V_HEAD
    return pl.pallas_call(
        functools.partial(_attn_kernel, tq=tq, nh=nh),
        grid=(bsz, MLA_HEADS // nh, nq),
        in_specs=[pl.BlockSpec((tq, kw), lambda b, h, i: (b * nq + i, h)),
                  pl.BlockSpec((seq, kw), lambda b, h, i: (b, h)),
                  pl.BlockSpec((seq, vw), lambda b, h, i: (b, h)),
                  pl.BlockSpec((LEAD, kw), lambda b, h, i: (0, h)),
                  pl.BlockSpec((LEAD, vw), lambda b, h, i: (0, h))],
        out_specs=pl.BlockSpec((tq, vw), lambda b, h, i: (b * nq + i, h)),
        out_shape=jax.ShapeDtypeStruct((bsz * seq, MLA_HEADS * V_HEAD), bf16),
        compiler_params=_cparams(("parallel", "parallel", "arbitrary")),
        name="mla_attn",
    )(q, k, v, kl, vl)


def _split3(v):
    b0 = v.astype(bf16).astype(f32)
    r1 = v - b0
    b1 = r1.astype(bf16).astype(f32)
    b2 = (r1 - b1).astype(bf16).astype(f32)
    return b0, b1, b2


def _ssd_kernel(zx_l, bc_l, dt_l, zx_r, bc_r, dt_r, cwx, cwbc, cbx, cbbc, dtb, alog, dskip, ng, e64, e128,
                o_ref, st_ref, px_ref, pbc_ref):
    c = pl.program_id(1)
    is_lead = c == 0

    @pl.when(is_lead)
    def _():
        st_ref[...] = jnp.zeros_like(st_ref)
        px_ref[...] = jnp.zeros_like(px_ref)
        pbc_ref[...] = jnp.zeros_like(pbc_ref)

    zx = jnp.where(is_lead, zx_l[...], zx_r[...])
    bcraw = jnp.where(is_lead, bc_l[...], bc_r[...])
    dtraw = jnp.where(is_lead, dt_l[...], dt_r[...])
    z = zx[:, :D_INNER]
    xraw = zx[:, D_INNER:]

    row1 = lax.broadcasted_iota(jnp.int32, (CHUNK, 1), 0)
    r_id = lax.broadcasted_iota(jnp.int32, (CHUNK, CHUNK), 0)
    c_id = lax.broadcasted_iota(jnp.int32, (CHUNK, CHUNK), 1)
    tril = r_id >= c_id

    def conv_silu(cur, prev_ref, w_ref, b_ref):
        prev = prev_ref[...]
        acc = cur * w_ref[CONV_W - 1:CONV_W, :] + b_ref[...]
        for k in range(1, CONV_W):
            comb = jnp.where(row1 >= CHUNK - k, prev, cur)
            acc = acc + pltpu.roll(comb, k, 0) * w_ref[CONV_W - 1 - k:CONV_W - k, :]
        prev_ref[...] = cur
        return acc * jax.nn.sigmoid(acc)

    xs = conv_silu(xraw, px_ref, cwx, cbx)
    bcc = conv_silu(bcraw, pbc_ref, cwbc, cbbc)
    b_all = bcc[:, :SSD_GROUPS * SSD_STATE]
    c_all = bcc[:, SSD_GROUPS * SSD_STATE:].astype(bf16)

    pre = dtraw + dtb[...]
    dt = jnp.maximum(pre, 0.0) + jnp.log1p(jnp.exp(-jnp.abs(pre)))
    dt = jnp.where(jnp.logical_and(is_lead, row1 < LEAD_PAD), 0.0, dt)
    da = dt * (-jnp.exp(alog[...]))
    tri = tril.astype(f32).astype(bf16)
    d0, d1, d2 = _split3(da)
    cs = _dot(tri, d0.astype(bf16)) + _dot(tri, d1.astype(bf16)) + _dot(tri, d2.astype(bf16))
    cs_t = cs.T
    ecs = jnp.exp(cs)
    dec = jnp.exp(cs[CHUNK - 1:CHUNK, :] - cs)

    def kcat(v):
        b0, b1, b2 = _split3(v)
        return jnp.where(c_id < 32, b0, jnp.where(c_id < 64, b1, jnp.where(c_id < 96, b2, 0.0))).astype(bf16)

    ex = _dot(jnp.concatenate([kcat(dt), kcat(ecs), kcat(dec)], axis=0), e64[...])
    dt_x = ex[0:CHUNK]
    ecs_x = ex[CHUNK:2 * CHUNK]
    dec_x = ex[2 * CHUNK:3 * CHUNK]
    cs_bc = _dot(kcat(cs), e128[...])

    xdt = xs * dt_x
    xdt_b = xdt.astype(bf16)
    xd_b = (xdt * dec_x).astype(bf16)

    ys = []
    for g in range(SSD_GROUPS):
        gl = slice(g * GROUP_W, (g + 1) * GROUP_W)
        b_g = b_all[:, g * SSD_STATE:(g + 1) * SSD_STATE]
        c_g = c_all[:, g * SSD_STATE:(g + 1) * SSD_STATE]
        cb = _dot_nt(c_g, b_g.astype(bf16))
        st_g = st_ref[:, gl]
        y_g = _dot(c_g, st_g.astype(bf16)) * ecs_x[:, gl]
        pieces = []
        for pr in range(GROUP_W // LANE):
            h0 = g * (GROUP_W // SSD_HEAD_DIM) + 2 * pr
            xpair = xdt_b[:, h0 * SSD_HEAD_DIM:(h0 + 2) * SSD_HEAD_DIM]
            yh = []
            for h in (h0, h0 + 1):
                seg = cs_bc[:, h * CHUNK:(h + 1) * CHUNK] - cs_t[h:h + 1, :]
                lm = jnp.where(tril, jnp.exp(jnp.minimum(seg, 0.0)), 0.0)
                yh.append(_dot((cb * lm).astype(bf16), xpair))
            pieces.append(jnp.where(c_id < SSD_HEAD_DIM, yh[0], yh[1]))
        y_g = y_g + jnp.concatenate(pieces, axis=1)
        st_ref[:, gl] = st_g * ecs_x[CHUNK - 1:CHUNK, gl] + _dot(b_g.T.astype(bf16), xd_b[:, gl])
        ys.append(y_g)

    y = jnp.concatenate(ys, axis=1) + xs * dskip[...]
    gated = y * (z * jax.nn.sigmoid(z))
    outs = []
    for g in range(SSD_GROUPS):
        gg = gated[:, g * GROUP_W:(g + 1) * GROUP_W]
        outs.append(gg * lax.rsqrt(jnp.mean(gg * gg, axis=-1, keepdims=True) + EPS))
    o_ref[...] = (jnp.concatenate(outs, axis=1) * ng[...]).astype(bf16)


def _ssd(proj_l, proj_r, cwx, cwbc, cbx, cbbc, dtb, alog, dskip, ng, e64, e128, *, bsz, seq):
    nch = seq // CHUNK
    real = lambda w: (lambda b, c: (b * nch + jnp.maximum(c - 1, 0), w))
    lead = lambda w: (lambda b, c: (0, w))
    const = lambda b, c: (0, 0)
    zx_w, bc_w, dt_w = 2 * D_INNER, 2 * SSD_GROUPS * SSD_STATE, LANE
    return pl.pallas_call(
        _ssd_kernel,
        grid=(bsz, nch + 1),
        in_specs=[pl.BlockSpec((CHUNK, zx_w), lead(P_Z // zx_w)),
                  pl.BlockSpec((CHUNK, bc_w), lead(P_B // bc_w)),
                  pl.BlockSpec((CHUNK, dt_w), lead(P_DT // dt_w)),
                  pl.BlockSpec((CHUNK, zx_w), real(P_Z // zx_w)),
                  pl.BlockSpec((CHUNK, bc_w), real(P_B // bc_w)),
                  pl.BlockSpec((CHUNK, dt_w), real(P_DT // dt_w)),
                  pl.BlockSpec((CONV_W, D_INNER), const),
                  pl.BlockSpec((CONV_W, bc_w), const),
                  pl.BlockSpec((1, D_INNER), const),
                  pl.BlockSpec((1, bc_w), const),
                  pl.BlockSpec((1, LANE), const),
                  pl.BlockSpec((1, LANE), const),
                  pl.BlockSpec((1, D_INNER), const),
                  pl.BlockSpec((1, D_INNER), const),
                  pl.BlockSpec((LANE, D_INNER), const),
                  pl.BlockSpec((LANE, SSD_HEADS * CHUNK), const)],
        out_specs=pl.BlockSpec((CHUNK, D_INNER), real(0)),
        out_shape=jax.ShapeDtypeStruct((bsz * seq, D_INNER), bf16),
        scratch_shapes=[pltpu.VMEM((SSD_STATE, D_INNER), f32),
                        pltpu.VMEM((CHUNK, D_INNER), f32),
                        pltpu.VMEM((CHUNK, bc_w), f32)],
        compiler_params=_cparams(("parallel", "arbitrary")),
        name="ssd",
    )(proj_l, proj_l, proj_l, proj_r, proj_r, proj_r, cwx, cwbc, cbx, cbbc, dtb, alog, dskip, ng, e64, e128)


def _outproj_kernel(a_ref, s_ref, wa_ref, ws_ref, x_ref, o_ref):
    o_ref[...] = _dot(a_ref[...], wa_ref[...]) + _dot(s_ref[...], ws_ref[...]) + x_ref[...]


def _outproj(a, s, w, x2d, *, tm, tn=512):
    m = a.shape[0]
    ka = a.shape[1]
    assert m % tm == 0 and D_MODEL % tn == 0
    return pl.pallas_call(
        _outproj_kernel,
        grid=(m // tm, D_MODEL // tn),
        in_specs=[pl.BlockSpec((tm, ka), lambda i, j: (i, 0)),
                  pl.BlockSpec((tm, D_INNER), lambda i, j: (i, 0)),
                  pl.BlockSpec((ka, tn), lambda i, j: (0, j)),
                  pl.BlockSpec((D_INNER, tn), lambda i, j: (1, j)),
                  pl.BlockSpec((tm, tn), lambda i, j: (i, j))],
        out_specs=pl.BlockSpec((tm, tn), lambda i, j: (i, j)),
        out_shape=jax.ShapeDtypeStruct((m, D_MODEL), f32),
        compiler_params=_cparams(("parallel", "arbitrary")),
        name="outproj",
    )(a, s, w, w, x2d)


def _extract_topk(x, k):
    vals = []
    for _ in range(k):
        m = jnp.max(x, axis=0, keepdims=True)
        x = jnp.where(x == m, -jnp.inf, x)
        vals.append(m)
    return vals


def _stack_rows(vals):
    n = len(vals)
    rid = lax.broadcasted_iota(jnp.int32, (n, vals[0].shape[1]), 0)
    out = jnp.zeros((n, vals[0].shape[1]), f32)
    for r, v in enumerate(vals):
        out = jnp.where(rid == r, v, out)
    return out


N_CAND = PEER_TOPK + (PEER_TOPK // 2 - 1) * (PEER_TOPK // 2) + PEER_TOPK


def _peer_q_kernel(h_ref, g_ref, wq_ref, sk_ref, xnt_ref, e1_ref, thr_ref, e2_ref, cand_ref):
    xn = _rms(h_ref[...], g_ref[...])
    xnt_ref[...] = xn.T.astype(bf16)
    q = _dot(xn.astype(bf16), wq_ref[...]).astype(bf16)
    sk1 = sk_ref[0].astype(bf16)
    sk2 = sk_ref[1].astype(bf16)
    k = PEER_TOPK
    half = k // 2
    for h in range(PEER_HEADS):
        lo = h * 2 * HALF_KEY
        s1 = _dot_nt(sk1, q[:, lo:lo + HALF_KEY])
        s2 = _dot_nt(sk2, q[:, lo + HALF_KEY:lo + 2 * HALF_KEY])
        v1 = _extract_topk(s1, k + 1)
        v2 = _extract_topk(s2, k + 1)
        v2_top = _stack_rows(v2[:k])
        cand_ref[0:k, :] = v1[0] + v2_top
        for a in range(1, half):
            cand_ref[k + (a - 1) * half:k + a * half, :] = v1[a] + v2_top[0:half, :]
        cand_ref[k + (half - 1) * half:k + half * half, :] = _stack_rows(v1[half:k]) + v2[0]
        ninf = jnp.full_like(v1[0], -jnp.inf)
        cand_ref[k + half * half:, :] = _stack_rows([v1[0] + v2[k], v1[k] + v2[0]] + [ninf] * (half - 2))
        top = _extract_topk(cand_ref[...], k + 1)
        zsum = jnp.zeros_like(top[0])
        for t in top[:k]:
            zsum = zsum + jnp.exp(t - top[0])
        tau = 0.5 * (top[k - 1] + top[k])
        scale = 0.5 / zsum
        e1_ref[h] = jnp.exp(s1 - v1[0])
        thr_ref[h] = jnp.exp((tau - v2[0]) - s1) * scale
        e2_ref[h] = pltpu.bitcast((jnp.exp(s2 - v2[0]) * scale).astype(bf16), jnp.uint32)


def _peer_q(h1, g, wq, sub_keys, *, tm):
    n = h1.shape[0]
    assert n % tm == 0
    nq = PEER_HEADS * 2 * HALF_KEY
    per_head = lambda rows, dt: (pl.BlockSpec((PEER_HEADS, rows, tm), lambda i: (0, 0, i)),
                                 jax.ShapeDtypeStruct((PEER_HEADS, rows, n), dt))
    specs, shapes = zip((pl.BlockSpec((D_MODEL, tm), lambda i: (0, i)), jax.ShapeDtypeStruct((D_MODEL, n), bf16)),
                        per_head(N_KEYS, f32), per_head(N_KEYS, f32), per_head(N_KEYS // 2, jnp.uint32))
    return pl.pallas_call(
        _peer_q_kernel,
        grid=(n // tm,),
        in_specs=[pl.BlockSpec((tm, D_MODEL), lambda i: (i, 0)),
                  pl.BlockSpec((1, D_MODEL), lambda i: (0, 0)),
                  pl.BlockSpec((D_MODEL, nq), lambda i: (0, 0)),
                  pl.BlockSpec((2, N_KEYS, HALF_KEY), lambda i: (0, 0, 0))],
        out_specs=list(specs),
        out_shape=list(shapes),
        scratch_shapes=[pltpu.VMEM((N_CAND, tm), f32)],
        compiler_params=_cparams(("parallel",)),
        name="peer_query",
    )(h1, g, wq, sub_keys)


def _peer_dense_kernel(xnt_ref, u_ref, v_ref, e1_ref, thr_ref, e2_ref, h_ref, g_ref, o_ref, ht_ref, act_ref, *, te):
    j = pl.program_id(1)
    tm = xnt_ref.shape[1]
    pk = N_KEYS // 2

    @pl.when(j == 0)
    def _():
        o_ref[...] = jnp.zeros_like(o_ref)

    ht_ref[...] = _dot(u_ref[...], xnt_ref[...])
    for ii in range(te // N_KEYS):
        i1 = j * (te // N_KEYS) + ii
        e1rows = [e1_ref[h, pl.ds(i1, 1), :] for h in range(PEER_HEADS)]
        thrrows = [thr_ref[h, pl.ds(i1, 1), :] for h in range(PEER_HEADS)]
        for c in range(tm // LANE):
            cl = slice(c * LANE, (c + 1) * LANE)
            w = jnp.zeros((N_KEYS, LANE), bf16)
            for h in range(PEER_HEADS):
                e1b = jnp.broadcast_to(e1rows[h][:, cl], (N_KEYS, LANE)).astype(bf16)
                thrb = jnp.broadcast_to(thrrows[h][:, cl], (N_KEYS, LANE)).astype(bf16)
                e2 = pltpu.bitcast(e2_ref[h, :, cl], bf16)
                w = w + jnp.where(e2 >= thrb, e2, jnp.zeros_like(e2)) * e1b
            hh = ht_ref[ii * N_KEYS:(ii + 1) * N_KEYS, cl]
            g = hh * (1.0 + lax.erf(hh * np.float32(1.0 / np.sqrt(2.0))))
            act_ref[ii * pk:(ii + 1) * pk, cl] = pltpu.bitcast(g.astype(bf16) * w, jnp.uint32)
    o_ref[...] += _dot(pltpu.bitcast(act_ref[...], bf16).T, v_ref[...])

    @pl.when(j == pl.num_programs(1) - 1)
    def _():
        sub = 64

        def body(r, c):
            rows = pl.ds(pl.multiple_of(r * sub, sub), sub)
            o_ref[rows, :] = _rms(o_ref[rows, :] + h_ref[rows, :], g_ref[...])
            return c
        lax.fori_loop(0, tm // sub, body, 0)


def _peer_dense(xnt, u_b, v_b, e1, thr, e2b, h1, g, *, tm, te):
    n = xnt.shape[1]
    n_exp = u_b.shape[0]
    assert n % tm == 0 and n_exp % te == 0 and te % N_KEYS == 0 and tm % LANE == 0
    once = pl.Buffered(1)
    return pl.pallas_call(
        functools.partial(_peer_dense_kernel, te=te),
        grid=(n // tm, n_exp // te),
        in_specs=[pl.BlockSpec((D_MODEL, tm), lambda i, j: (0, i), pipeline_mode=once),
                  pl.BlockSpec((te, D_MODEL), lambda i, j: (j, 0)),
                  pl.BlockSpec((te, D_MODEL), lambda i, j: (j, 0)),
                  pl.BlockSpec((PEER_HEADS, N_KEYS, tm), lambda i, j: (0, 0, i), pipeline_mode=once),
                  pl.BlockSpec((PEER_HEADS, N_KEYS, tm), lambda i, j: (0, 0, i), pipeline_mode=once),
                  pl.BlockSpec((PEER_HEADS, N_KEYS // 2, tm), lambda i, j: (0, 0, i), pipeline_mode=once),
                  pl.BlockSpec((tm, D_MODEL), lambda i, j: (i, 0), pipeline_mode=once),
                  pl.BlockSpec((1, D_MODEL), lambda i, j: (0, 0))],
        out_specs=pl.BlockSpec((tm, D_MODEL), lambda i, j: (i, 0)),
        out_shape=jax.ShapeDtypeStruct((n, D_MODEL), f32),
        scratch_shapes=[pltpu.VMEM((te, tm), f32),
                        pltpu.VMEM((te // 2, tm), jnp.uint32)],
        compiler_params=_cparams(("parallel", "arbitrary")),
        name="peer_dense",
    )(xnt, u_b, v_b, e1, thr, e2b, h1, g)


_W_IN_SRC = ((1600, 3648), (3648, 5696), (0, 1024), (5696, 6208), (6208, 6720), (1024, 1536),
             (1536, 1600), (1568, 1600), (1536, 1568),
             (6720, 6752), (6720, 6752), (6720, 6752))


def _prep_w_in_kernel(w_ref, o_ref):
    col = 0
    tail = []
    for lo, hi in _W_IN_SRC:
        piece = w_ref[:, lo:hi]
        if (hi - lo) % LANE == 0:
            o_ref[:, col:col + hi - lo] = piece.astype(bf16)
            col += hi - lo
        else:
            tail.append(piece)
    rows = w_ref.shape[0]
    tail.append(jnp.zeros((rows, NP - col - sum(t.shape[1] for t in tail)), f32))
    o_ref[:, col:] = jnp.concatenate(tail, axis=1).astype(bf16)


def _prep_w_in(w, *, tr=256):
    k, n_in = w.shape
    assert k % tr == 0
    return pl.pallas_call(
        _prep_w_in_kernel,
        grid=(k // tr,),
        in_specs=[pl.BlockSpec((tr, n_in), lambda i: (i, 0))],
        out_specs=pl.BlockSpec((tr, NP), lambda i: (i, 0)),
        out_shape=jax.ShapeDtypeStruct((k, NP), bf16),
        compiler_params=_cparams(("parallel",)),
        name="prep_w_in",
    )(w)


def _prep_w_uq(w):
    w = w.reshape(Q_LORA, MLA_HEADS, QK_NOPE + QK_ROPE)
    nope, x1, x2 = w[..., :QK_NOPE], w[..., QK_NOPE:QK_NOPE + 32], w[..., QK_NOPE + 32:]
    return jnp.concatenate([nope, x1, x2, x2, x1], axis=-1).reshape(Q_LORA, MLA_HEADS * QK_SLAB).astype(bf16)


def _rope_tables(pos):
    inv_freq = ROPE_THETA ** (-jnp.arange(0, QK_ROPE, 2, dtype=f32) / QK_ROPE)
    ang = pos[:, None] * inv_freq[None, :]
    c, s = jnp.cos(ang), jnp.sin(ang)
    zero = jnp.zeros((pos.shape[0], LANE - QK_ROPE), f32)
    return jnp.concatenate([c, c, zero], axis=1), jnp.concatenate([-s, s, zero], axis=1)


def _expand_matrix(width):
    r = np.arange(LANE)[:, None]
    col = np.arange(SSD_HEADS * width)[None, :]
    return jnp.asarray(((r % SSD_HEADS) == (col // width)) & (r < 3 * SSD_HEADS), dtype=bf16)


def _tile(m, pref):
    t = min(pref, m)
    assert m % t == 0
    return t


def kernel(x, meta_tokens, attn_norm, w_in, q_norm, w_uq, kv_norm, w_ukv, conv_w, conv_b, dt_bias, a_log, d_skip,
           ssd_norm, w_out, ffn_norm, w_query, sub_keys, u_experts, v_experts, final_norm):
    bsz, seq, d = x.shape
    assert d == D_MODEL and seq % CHUNK == 0
    n_tok = bsz * seq
    x2d = x.reshape(n_tok, d)

    w_in_b = _prep_w_in(w_in[0])
    wq_b = _prep_w_uq(w_uq[0])
    wkv = w_ukv[0].reshape(KV_LORA, MLA_HEADS, QK_NOPE + V_HEAD)
    wk_b = wkv[..., :QK_NOPE].reshape(KV_LORA, MLA_HEADS * QK_NOPE).astype(bf16)
    wv_b = wkv[..., QK_NOPE:].reshape(KV_LORA, MLA_HEADS * V_HEAD).astype(bf16)
    w_out_b = w_out[0].astype(bf16)
    w_query_b = w_query[0].astype(bf16)
    u_b = u_experts[0].astype(bf16)
    v_b = v_experts[0].astype(bf16)
    cwx, cwbc = conv_w[0][:, :D_INNER], conv_w[0][:, D_INNER:]
    cbx, cbbc = conv_b[0][None, :D_INNER], conv_b[0][None, D_INNER:]
    rep3 = lambda v, fill: jnp.concatenate([v, v, v, jnp.full((LANE - 3 * SSD_HEADS,), fill, f32)])[None, :]
    dtb = rep3(dt_bias[0], 0.0)
    alog = rep3(a_log[0], 0.0)
    dskip = jnp.repeat(d_skip[0], SSD_HEAD_DIM)[None, :]
    e64 = _expand_matrix(SSD_HEAD_DIM)
    e128 = _expand_matrix(CHUNK)

    lead_rows = jnp.concatenate([jnp.zeros((LEAD_PAD, d), x.dtype), meta_tokens.astype(x.dtype)], axis=0)
    cc_l, ss_l = _rope_tables(jnp.arange(LEAD, dtype=f32) - LEAD_PAD)
    cc_r, ss_r = _rope_tables(jnp.arange(seq, dtype=f32) + N_META)

    g_attn = attn_norm[0][None, :]
    proj_l = _inproj(lead_rows, g_attn, w_in_b, tm=LEAD)
    proj_r = _inproj(x2d, g_attn, w_in_b, tm=_tile(n_tok, 512))

    qn, kvn = q_norm[0][None, :], kv_norm[0][None, :]
    _, k_l, v_l = _mla_up(proj_l, cc_l, ss_l, qn, kvn, wq_b, wk_b, wv_b, tm=LEAD, n_pos_blocks=1)
    tmu = _tile(seq, 256)
    q_r, k_r, v_r = _mla_up(proj_r, cc_r, ss_r, qn, kvn, wq_b, wk_b, wv_b, tm=tmu, n_pos_blocks=seq // tmu)
    a_out = _attention(q_r, k_r, v_r, k_l, v_l, bsz=bsz, seq=seq, tq=_tile(seq, 512))

    s_out = _ssd(proj_l, proj_r, cwx, cwbc, cbx, cbbc, dtb, alog, dskip, ssd_norm[0][None, :], e64, e128,
                 bsz=bsz, seq=seq)

    h1 = _outproj(a_out, s_out, w_out_b, x2d, tm=_tile(n_tok, 1024))

    xnt, e1, thr, e2b = _peer_q(h1, ffn_norm[0][None, :], w_query_b, sub_keys[0], tm=_tile(n_tok, 256))
    out = _peer_dense(xnt, u_b, v_b, e1, thr, e2b, h1, final_norm[None, :], tm=_tile(n_tok, 512), te=512)
    return out.reshape(bsz, seq, d)
```

```python
import functools

import jax
import jax.numpy as jnp
import numpy as np
from jax import lax
from jax.experimental import pallas as pl
from jax.experimental.pallas import tpu as pltpu

f32 = jnp.float32
bf16 = jnp.bfloat16

D_MODEL = 4096
N_META = 16
CHUNK = 128
LEAD = CHUNK
LEAD_PAD = CHUNK - N_META
MLA_HEADS = 16
QK_NOPE = 128
QK_ROPE = 64
V_HEAD = 128
Q_LORA = 1024
KV_LORA = 512
ROPE_THETA = 10000.0
D_INNER = 2048
SSD_HEAD_DIM = 64
SSD_HEADS = 32
SSD_GROUPS = 4
SSD_STATE = 128
CONV_W = 4
PEER_HEADS = 8
PEER_TOPK = 16
N_KEYS = 128
HALF_KEY = 128
EPS = 1e-6
NEG_INF = -1e30

LANE = 128
QK_SLAB = 2 * LANE
GROUP_W = D_INNER // SSD_GROUPS

P_Z = 0
P_XS = 2048
P_CQ = 4096
P_B = 5120
P_C = 5632
P_CKV = 6144
P_KR = 6656
P_DT = 6784
NP = 6912

VMEM_LIMIT = 56 * 1024 * 1024


def _cparams(sem, vmem=VMEM_LIMIT, flags=None):
    return pltpu.CompilerParams(dimension_semantics=sem, vmem_limit_bytes=vmem, flags=flags)


def _dot(a, b):
    return jnp.dot(a, b, preferred_element_type=f32)


def _dot_nt(a, b):
    return lax.dot_general(a, b, (((1,), (1,)), ((), ())), preferred_element_type=f32)


def _rms(x, g):
    return x * lax.rsqrt(jnp.mean(x * x, axis=-1, keepdims=True) + EPS) * g


def _inproj_kernel(x_ref, g_ref, w_ref, o_ref, xn_ref, *, sub):
    @pl.when(pl.program_id(1) == 0)
    def _():
        def body(r, c):
            rows = pl.ds(pl.multiple_of(r * sub, sub), sub)
            xn_ref[rows, :] = _rms(x_ref[rows, :], g_ref[...]).astype(bf16)
            return c
        lax.fori_loop(0, x_ref.shape[0] // sub, body, 0)

    o_ref[...] = _dot(xn_ref[...], w_ref[...])


def _inproj(x2d, g, w, *, tm, tn=1152):
    m = x2d.shape[0]
    assert m % tm == 0 and NP % tn == 0
    sub = min(64, tm)
    return pl.pallas_call(
        functools.partial(_inproj_kernel, sub=sub),
        grid=(m // tm, NP // tn),
        in_specs=[pl.BlockSpec((tm, D_MODEL), lambda i, j: (i, 0)),
                  pl.BlockSpec((1, D_MODEL), lambda i, j: (0, 0)),
                  pl.BlockSpec((D_MODEL, tn), lambda i, j: (0, j))],
        out_specs=pl.BlockSpec((tm, tn), lambda i, j: (i, j)),
        out_shape=jax.ShapeDtypeStruct((m, NP), f32),
        scratch_shapes=[pltpu.VMEM((tm, D_MODEL), bf16)],
        compiler_params=_cparams(("parallel", "arbitrary")),
        name="inproj",
    )(x2d, g, w)


def _mla_up_kernel(cq_ref, ckv_ref, kr_ref, cc_ref, ss_ref, qn_ref, kvn_ref, wq_ref, wk_ref, wv_ref,
                   q_ref, k_ref, v_ref, *, scale):
    cc = cc_ref[...]
    ss = ss_ref[...]

    def rope(t):
        return t * cc + pltpu.roll(t, 64, 1) * ss

    cqn = _rms(cq_ref[...], qn_ref[...]).astype(bf16)
    q = _dot(cqn, wq_ref[...])
    for h in range(MLA_HEADS):
        lo = h * QK_SLAB
        q_ref[:, lo:lo + LANE] = (q[:, lo:lo + LANE] * scale).astype(bf16)
        q_ref[:, lo + LANE:lo + QK_SLAB] = (rope(q[:, lo + LANE:lo + QK_SLAB]) * scale).astype(bf16)

    kvn = _rms(ckv_ref[...], kvn_ref[...]).astype(bf16)
    kn = _dot(kvn, wk_ref[...])
    kpe = rope(kr_ref[...]).astype(bf16)
    for h in range(MLA_HEADS):
        lo = h * QK_SLAB
        k_ref[:, lo:lo + LANE] = kn[:, h * LANE:(h + 1) * LANE].astype(bf16)
        k_ref[:, lo + LANE:lo + QK_SLAB] = kpe
    v_ref[...] = _dot(kvn, wv_ref[...]).astype(bf16)


def _mla_up(proj, cc, ss, qn, kvn, wq, wk, wv, *, tm, n_pos_blocks):
    m = proj.shape[0]
    assert m % tm == 0
    scale = float((QK_NOPE + QK_ROPE) ** -0.5 * np.log2(np.e))
    hq = MLA_HEADS * QK_SLAB
    hv = MLA_HEADS * V_HEAD
    const = lambda i: (0, 0)
    return pl.pallas_call(
        functools.partial(_mla_up_kernel, scale=scale),
        grid=(m // tm,),
        in_specs=[pl.BlockSpec((tm, Q_LORA), lambda i: (i, P_CQ // Q_LORA)),
                  pl.BlockSpec((tm, KV_LORA), lambda i: (i, P_CKV // KV_LORA)),
                  pl.BlockSpec((tm, LANE), lambda i: (i, P_KR // LANE)),
                  pl.BlockSpec((tm, LANE), lambda i: (i % n_pos_blocks, 0)),
                  pl.BlockSpec((tm, LANE), lambda i: (i % n_pos_blocks, 0)),
                  pl.BlockSpec((1, Q_LORA), const),
                  pl.BlockSpec((1, KV_LORA), const),
                  pl.BlockSpec((Q_LORA, hq), const),
                  pl.BlockSpec((KV_LORA, hv), const),
                  pl.BlockSpec((KV_LORA, hv), const)],
        out_specs=[pl.BlockSpec((tm, hq), lambda i: (i, 0)),
                   pl.BlockSpec((tm, hq), lambda i: (i, 0)),
                   pl.BlockSpec((tm, hv), lambda i: (i, 0))],
        out_shape=[jax.ShapeDtypeStruct((m, hq), bf16),
                   jax.ShapeDtypeStruct((m, hq), bf16),
                   jax.ShapeDtypeStruct((m, hv), bf16)],
        compiler_params=_cparams(("parallel",)),
        name="mla_up",
    )(proj, proj, proj, cc, ss, qn, kvn, wq, wk, wv)


def _attn_kernel(q_ref, k_ref, v_ref, kl_ref, vl_ref, o_ref, *, tq, nh):
    i = pl.program_id(2)
    qs = [q_ref[:, h * QK_SLAB:(h + 1) * QK_SLAB] for h in range(nh)]
    ksl = lambda h: slice(h * QK_SLAB, (h + 1) * QK_SLAB)
    vsl = lambda h: slice(h * V_HEAD, (h + 1) * V_HEAD)

    diag = pl.ds(pl.multiple_of(i * tq, tq), tq)
    col = lax.broadcasted_iota(jnp.int32, (tq, LEAD), 1)
    r_id = lax.broadcasted_iota(jnp.int32, (tq, tq), 0)
    c_id = lax.broadcasted_iota(jnp.int32, (tq, tq), 1)
    carry = []
    for h in range(nh):
        s_l = jnp.where(col >= LEAD_PAD, _dot_nt(qs[h], kl_ref[:, ksl(h)]), NEG_INF)
        s_d = jnp.where(c_id <= r_id, _dot_nt(qs[h], k_ref[diag, ksl(h)]), NEG_INF)
        m = jnp.maximum(jnp.max(s_l, axis=-1, keepdims=True), jnp.max(s_d, axis=-1, keepdims=True))
        p_l = jnp.exp2(s_l - m)
        p_d = jnp.exp2(s_d - m)
        l = jnp.sum(p_l, axis=-1, keepdims=True) + jnp.sum(p_d, axis=-1, keepdims=True)
        acc = _dot(p_l.astype(bf16), vl_ref[:, vsl(h)]) + _dot(p_d.astype(bf16), v_ref[diag, vsl(h)])
        carry += [m, l, acc]

    def update(s, vb, m, l, acc):
        m_new = jnp.maximum(m, jnp.max(s, axis=-1, keepdims=True))
        alpha = jnp.exp2(m - m_new)
        p = jnp.exp2(s - m_new)
        l = alpha * l + jnp.sum(p, axis=-1, keepdims=True)
        acc = alpha * acc + _dot(p.astype(bf16), vb)
        return [m_new, l, acc]

    def body(j, carry):
        rows = pl.ds(pl.multiple_of(j * tq, tq), tq)
        out = []
        for h in range(nh):
            out += update(_dot_nt(qs[h], k_ref[rows, ksl(h)]), v_ref[rows, vsl(h)], *carry[3 * h:3 * h + 3])
        return tuple(out)

    carry = lax.fori_loop(0, i, body, tuple(carry))
    for h in range(nh):
        o_ref[:, vsl(h)] = (carry[3 * h + 2] / carry[3 * h + 1]).astype(bf16)


def _attention(q, k, v, kl, vl, *, bsz, seq, tq, nh=2):
    assert seq % tq == 0 and MLA_HEADS % nh == 0
    nq = seq // tq
    kw, vw = nh * QK_SLAB, nh * V_HEAD
    return pl.pallas_call(
        functools.partial(_attn_kernel, tq=tq, nh=nh),
        grid=(bsz, MLA_HEADS // nh, nq),
        in_specs=[pl.BlockSpec((tq, kw), lambda b, h, i: (b * nq + i, h)),
                  pl.BlockSpec((seq, kw), lambda b, h, i: (b, h)),
                  pl.BlockSpec((seq, vw), lambda b, h, i: (b, h)),
                  pl.BlockSpec((LEAD, kw), lambda b, h, i: (0, h)),
                  pl.BlockSpec((LEAD, vw), lambda b, h, i: (0, h))],
        out_specs=pl.BlockSpec((tq, vw), lambda b, h, i: (b * nq + i, h)),
        out_shape=jax.ShapeDtypeStruct((bsz * seq, MLA_HEADS * V_HEAD), bf16),
        compiler_params=_cparams(("parallel", "parallel", "arbitrary")),
        name="mla_attn",
    )(q, k, v, kl, vl)


def _split3(v):
    b0 = v.astype(bf16).astype(f32)
    r1 = v - b0
    b1 = r1.astype(bf16).astype(f32)
    b2 = (r1 - b1).astype(bf16).astype(f32)
    return b0, b1, b2


def _ssd_kernel(zx_l, bc_l, dt_l, zx_r, bc_r, dt_r, cwx, cwbc, cbx, cbbc, dtb, alog, dskip, ng, e64, e128,
                o_ref, st_ref, px_ref, pbc_ref):
    c = pl.program_id(1)
    is_lead = c == 0

    @pl.when(is_lead)
    def _():
        st_ref[...] = jnp.zeros_like(st_ref)
        px_ref[...] = jnp.zeros_like(px_ref)
        pbc_ref[...] = jnp.zeros_like(pbc_ref)

    zx = jnp.where(is_lead, zx_l[...], zx_r[...])
    bcraw = jnp.where(is_lead, bc_l[...], bc_r[...])
    dtraw = jnp.where(is_lead, dt_l[...], dt_r[...])
    z = zx[:, :D_INNER]
    xraw = zx[:, D_INNER:]

    row1 = lax.broadcasted_iota(jnp.int32, (CHUNK, 1), 0)
    r_id = lax.broadcasted_iota(jnp.int32, (CHUNK, CHUNK), 0)
    c_id = lax.broadcasted_iota(jnp.int32, (CHUNK, CHUNK), 1)
    tril = r_id >= c_id

    def conv_silu(cur, prev_ref, w_ref, b_ref):
        prev = prev_ref[...]
        acc = cur * w_ref[CONV_W - 1:CONV_W, :] + b_ref[...]
        for k in range(1, CONV_W):
            comb = jnp.where(row1 >= CHUNK - k, prev, cur)
            acc = acc + pltpu.roll(comb, k, 0) * w_ref[CONV_W - 1 - k:CONV_W - k, :]
        prev_ref[...] = cur
        return acc * jax.nn.sigmoid(acc)

    xs = conv_silu(xraw, px_ref, cwx, cbx)
    bcc = conv_silu(bcraw, pbc_ref, cwbc, cbbc)
    b_all = bcc[:, :SSD_GROUPS * SSD_STATE]
    c_all = bcc[:, SSD_GROUPS * SSD_STATE:].astype(bf16)

    pre = dtraw + dtb[...]
    dt = jnp.maximum(pre, 0.0) + jnp.log1p(jnp.exp(-jnp.abs(pre)))
    dt = jnp.where(jnp.logical_and(is_lead, row1 < LEAD_PAD), 0.0, dt)
    da = dt * (-jnp.exp(alog[...]))
    tri = tril.astype(f32).astype(bf16)
    d0, d1, d2 = _split3(da)
    cs = _dot(tri, d0.astype(bf16)) + _dot(tri, d1.astype(bf16)) + _dot(tri, d2.astype(bf16))
    cs_t = cs.T
    ecs = jnp.exp(cs)
    dec = jnp.exp(cs[CHUNK - 1:CHUNK, :] - cs)

    def kcat(v):
        b0, b1, b2 = _split3(v)
        return jnp.where(c_id < 32, b0, jnp.where(c_id < 64, b1, jnp.where(c_id < 96, b2, 0.0))).astype(bf16)

    ex = _dot(jnp.concatenate([kcat(dt), kcat(ecs), kcat(dec)], axis=0), e64[...])
    dt_x = ex[0:CHUNK]
    ecs_x = ex[CHUNK:2 * CHUNK]
    dec_x = ex[2 * CHUNK:3 * CHUNK]
    cs_bc = _dot(kcat(cs), e128[...])

    xdt = xs * dt_x
    xdt_b = xdt.astype(bf16)
    xd_b = (xdt * dec_x).astype(bf16)

    ys = []
    for g in range(SSD_GROUPS):
        gl = slice(g * GROUP_W, (g + 1) * GROUP_W)
        b_g = b_all[:, g * SSD_STATE:(g + 1) * SSD_STATE]
        c_g = c_all[:, g * SSD_STATE:(g + 1) * SSD_STATE]
        cb = _dot_nt(c_g, b_g.astype(bf16))
        st_g = st_ref[:, gl]
        y_g = _dot(c_g, st_g.astype(bf16)) * ecs_x[:, gl]
        pieces = []
        for pr in range(GROUP_W // LANE):
            h0 = g * (GROUP_W // SSD_HEAD_DIM) + 2 * pr
            xpair = xdt_b[:, h0 * SSD_HEAD_DIM:(h0 + 2) * SSD_HEAD_DIM]
            yh = []
            for h in (h0, h0 + 1):
                seg = cs_bc[:, h * CHUNK:(h + 1) * CHUNK] - cs_t[h:h + 1, :]
                lm = jnp.where(tril, jnp.exp(jnp.minimum(seg, 0.0)), 0.0)
                yh.append(_dot((cb * lm).astype(bf16), xpair))
            pieces.append(jnp.where(c_id < SSD_HEAD_DIM, yh[0], yh[1]))
        y_g = y_g + jnp.concatenate(pieces, axis=1)
        st_ref[:, gl] = st_g * ecs_x[CHUNK - 1:CHUNK, gl] + _dot(b_g.T.astype(bf16), xd_b[:, gl])
        ys.append(y_g)

    y = jnp.concatenate(ys, axis=1) + xs * dskip[...]
    gated = y * (z * jax.nn.sigmoid(z))
    outs = []
    for g in range(SSD_GROUPS):
        gg = gated[:, g * GROUP_W:(g + 1) * GROUP_W]
        outs.append(gg * lax.rsqrt(jnp.mean(gg * gg, axis=-1, keepdims=True) + EPS))
    o_ref[...] = (jnp.concatenate(outs, axis=1) * ng[...]).astype(bf16)


def _ssd(proj_l, proj_r, cwx, cwbc, cbx, cbbc, dtb, alog, dskip, ng, e64, e128, *, bsz, seq):
    nch = seq // CHUNK
    real = lambda w: (lambda b, c: (b * nch + jnp.maximum(c - 1, 0), w))
    lead = lambda w: (lambda b, c: (0, w))
    const = lambda b, c: (0, 0)
    zx_w, bc_w, dt_w = 2 * D_INNER, 2 * SSD_GROUPS * SSD_STATE, LANE
    return pl.pallas_call(
        _ssd_kernel,
        grid=(bsz, nch + 1),
        in_specs=[pl.BlockSpec((CHUNK, zx_w), lead(P_Z // zx_w)),
                  pl.BlockSpec((CHUNK, bc_w), lead(P_B // bc_w)),
                  pl.BlockSpec((CHUNK, dt_w), lead(P_DT // dt_w)),
                  pl.BlockSpec((CHUNK, zx_w), real(P_Z // zx_w)),
                  pl.BlockSpec((CHUNK, bc_w), real(P_B // bc_w)),
                  pl.BlockSpec((CHUNK, dt_w), real(P_DT // dt_w)),
                  pl.BlockSpec((CONV_W, D_INNER), const),
                  pl.BlockSpec((CONV_W, bc_w), const),
                  pl.BlockSpec((1, D_INNER), const),
                  pl.BlockSpec((1, bc_w), const),
                  pl.BlockSpec((1, LANE), const),
                  pl.BlockSpec((1, LANE), const),
                  pl.BlockSpec((1, D_INNER), const),
                  pl.BlockSpec((1, D_INNER), const),
                  pl.BlockSpec((LANE, D_INNER), const),
                  pl.BlockSpec((LANE, SSD_HEADS * CHUNK), const)],
        out_specs=pl.BlockSpec((CHUNK, D_INNER), real(0)),
        out_shape=jax.ShapeDtypeStruct((bsz * seq, D_INNER), bf16),
        scratch_shapes=[pltpu.VMEM((SSD_STATE, D_INNER), f32),
                        pltpu.VMEM((CHUNK, D_INNER), f32),
                        pltpu.VMEM((CHUNK, bc_w), f32)],
        compiler_params=_cparams(("parallel", "arbitrary")),
        name="ssd",
    )(proj_l, proj_l, proj_l, proj_r, proj_r, proj_r, cwx, cwbc, cbx, cbbc, dtb, alog, dskip, ng, e64, e128)


def _outproj_kernel(a_ref, s_ref, wa_ref, ws_ref, x_ref, o_ref):
    o_ref[...] = _dot(a_ref[...], wa_ref[...]) + _dot(s_ref[...], ws_ref[...]) + x_ref[...]


def _outproj(a, s, w, x2d, *, tm, tn=512):
    m = a.shape[0]
    ka = a.shape[1]
    assert m % tm == 0 and D_MODEL % tn == 0
    return pl.pallas_call(
        _outproj_kernel,
        grid=(m // tm, D_MODEL // tn),
        in_specs=[pl.BlockSpec((tm, ka), lambda i, j: (i, 0)),
                  pl.BlockSpec((tm, D_INNER), lambda i, j: (i, 0)),
                  pl.BlockSpec((ka, tn), lambda i, j: (0, j)),
                  pl.BlockSpec((D_INNER, tn), lambda i, j: (1, j)),
                  pl.BlockSpec((tm, tn), lambda i, j: (i, j))],
        out_specs=pl.BlockSpec((tm, tn), lambda i, j: (i, j)),
        out_shape=jax.ShapeDtypeStruct((m, D_MODEL), f32),
        compiler_params=_cparams(("parallel", "arbitrary")),
        name="outproj",
    )(a, s, w, w, x2d)


def _extract_topk(x, k):
    vals = []
    for _ in range(k):
        m = jnp.max(x, axis=0, keepdims=True)
        x = jnp.where(x == m, -jnp.inf, x)
        vals.append(m)
    return vals


def _stack_rows(vals):
    n = len(vals)
    rid = lax.broadcasted_iota(jnp.int32, (n, vals[0].shape[1]), 0)
    out = jnp.zeros((n, vals[0].shape[1]), f32)
    for r, v in enumerate(vals):
        out = jnp.where(rid == r, v, out)
    return out


N_CAND = PEER_TOPK + (PEER_TOPK // 2 - 1) * (PEER_TOPK // 2) + PEER_TOPK


def _peer_q_kernel(h_ref, g_ref, wq_ref, sk_ref, xnt_ref, e1_ref, thr_ref, e2_ref, cand_ref):
    xn = _rms(h_ref[...], g_ref[...])
    xnt_ref[...] = xn.T.astype(bf16)
    q = _dot(xn.astype(bf16), wq_ref[...]).astype(bf16)
    sk1 = sk_ref[0].astype(bf16)
    sk2 = sk_ref[1].astype(bf16)
    k = PEER_TOPK
    half = k // 2
    for h in range(PEER_HEADS):
        lo = h * 2 * HALF_KEY
        s1 = _dot_nt(sk1, q[:, lo:lo + HALF_KEY])
        s2 = _dot_nt(sk2, q[:, lo + HALF_KEY:lo + 2 * HALF_KEY])
        v1 = _extract_topk(s1, k + 1)
        v2 = _extract_topk(s2, k + 1)
        v2_top = _stack_rows(v2[:k])
        cand_ref[0:k, :] = v1[0] + v2_top
        for a in range(1, half):
            cand_ref[k + (a - 1) * half:k + a * half, :] = v1[a] + v2_top[0:half, :]
        cand_ref[k + (half - 1) * half:k + half * half, :] = _stack_rows(v1[half:k]) + v2[0]
        ninf = jnp.full_like(v1[0], -jnp.inf)
        cand_ref[k + half * half:, :] = _stack_rows([v1[0] + v2[k], v1[k] + v2[0]] + [ninf] * (half - 2))
        top = _extract_topk(cand_ref[...], k + 1)
        zsum = jnp.zeros_like(top[0])
        for t in top[:k]:
            zsum = zsum + jnp.exp(t - top[0])
        tau = 0.5 * (top[k - 1] + top[k])
        scale = 0.5 / zsum
        e1_ref[h] = jnp.exp(s1 - v1[0])
        thr_ref[h] = jnp.exp((tau - v2[0]) - s1) * scale
        e2_ref[h] = pltpu.bitcast((jnp.exp(s2 - v2[0]) * scale).astype(bf16), jnp.uint32)


def _peer_q(h1, g, wq, sub_keys, *, tm):
    n = h1.shape[0]
    assert n % tm == 0
    nq = PEER_HEADS * 2 * HALF_KEY
    per_head = lambda rows, dt: (pl.BlockSpec((PEER_HEADS, rows, tm), lambda i: (0, 0, i)),
                                 jax.ShapeDtypeStruct((PEER_HEADS, rows, n), dt))
    specs, shapes = zip((pl.BlockSpec((D_MODEL, tm), lambda i: (0, i)), jax.ShapeDtypeStruct((D_MODEL, n), bf16)),
                        per_head(N_KEYS, f32), per_head(N_KEYS, f32), per_head(N_KEYS // 2, jnp.uint32))
    return pl.pallas_call(
        _peer_q_kernel,
        grid=(n // tm,),
        in_specs=[pl.BlockSpec((tm, D_MODEL), lambda i: (i, 0)),
                  pl.BlockSpec((1, D_MODEL), lambda i: (0, 0)),
                  pl.BlockSpec((D_MODEL, nq), lambda i: (0, 0)),
                  pl.BlockSpec((2, N_KEYS, HALF_KEY), lambda i: (0, 0, 0))],
        out_specs=list(specs),
        out_shape=list(shapes),
        scratch_shapes=[pltpu.VMEM((N_CAND, tm), f32)],
        compiler_params=_cparams(("parallel",)),
        name="peer_query",
    )(h1, g, wq, sub_keys)


def _peer_dense_kernel(xnt_ref, u_ref, v_ref, e1_ref, thr_ref, e2_ref, h_ref, g_ref, o_ref, ht_ref, act_ref, *, te):
    j = pl.program_id(1)
    tm = xnt_ref.shape[1]
    pk = N_KEYS // 2

    @pl.when(j == 0)
    def _():
        o_ref[...] = jnp.zeros_like(o_ref)

    ht_ref[...] = _dot(u_ref[...], xnt_ref[...])
    for ii in range(te // N_KEYS):
        i1 = j * (te // N_KEYS) + ii
        e1rows = [e1_ref[h, pl.ds(i1, 1), :] for h in range(PEER_HEADS)]
        thrrows = [thr_ref[h, pl.ds(i1, 1), :] for h in range(PEER_HEADS)]
        for c in range(tm // LANE):
            cl = slice(c * LANE, (c + 1) * LANE)
            w = jnp.zeros((N_KEYS, LANE), bf16)
            for h in range(PEER_HEADS):
                e1b = jnp.broadcast_to(e1rows[h][:, cl], (N_KEYS, LANE)).astype(bf16)
                thrb = jnp.broadcast_to(thrrows[h][:, cl], (N_KEYS, LANE)).astype(bf16)
                e2 = pltpu.bitcast(e2_ref[h, :, cl], bf16)
                w = w + jnp.where(e2 >= thrb, e2, jnp.zeros_like(e2)) * e1b
            hh = ht_ref[ii * N_KEYS:(ii + 1) * N_KEYS, cl]
            g = hh * (1.0 + lax.erf(hh * np.float32(1.0 / np.sqrt(2.0))))
            act_ref[ii * pk:(ii + 1) * pk, cl] = pltpu.bitcast(g.astype(bf16) * w, jnp.uint32)
    o_ref[...] += _dot(pltpu.bitcast(act_ref[...], bf16).T, v_ref[...])

    hr = h_ref.shape[0]
    slab = pl.ds(pl.multiple_of(j * hr, hr), hr)
    o_ref[slab, :] += h_ref[...]

    @pl.when(j == pl.num_programs(1) - 1)
    def _():
        sub = 64

        def body(r, c):
            rows = pl.ds(pl.multiple_of(r * sub, sub), sub)
            o_ref[rows, :] = _rms(o_ref[rows, :], g_ref[...])
            return c
        lax.fori_loop(0, tm // sub, body, 0)


def _peer_dense(xnt, u_b, v_b, e1, thr, e2b, h1, g, *, tm, te):
    n = xnt.shape[1]
    n_exp = u_b.shape[0]
    nj = n_exp // te
    assert n % tm == 0 and n_exp % te == 0 and te % N_KEYS == 0 and tm % LANE == 0 and tm % (8 * nj) == 0
    hr = tm // nj
    return pl.pallas_call(
        functools.partial(_peer_dense_kernel, te=te),
        grid=(n // tm, nj),
        in_specs=[pl.BlockSpec((D_MODEL, tm), lambda i, j: (0, i)),
                  pl.BlockSpec((te, D_MODEL), lambda i, j: (j, 0)),
                  pl.BlockSpec((te, D_MODEL), lambda i, j: (j, 0)),
                  pl.BlockSpec((PEER_HEADS, N_KEYS, tm), lambda i, j: (0, 0, i)),
                  pl.BlockSpec((PEER_HEADS, N_KEYS, tm), lambda i, j: (0, 0, i)),
                  pl.BlockSpec((PEER_HEADS, N_KEYS // 2, tm), lambda i, j: (0, 0, i)),
                  pl.BlockSpec((hr, D_MODEL), lambda i, j: (i * nj + j, 0)),
                  pl.BlockSpec((1, D_MODEL), lambda i, j: (0, 0))],
        out_specs=pl.BlockSpec((tm, D_MODEL), lambda i, j: (i, 0)),
        out_shape=jax.ShapeDtypeStruct((n, D_MODEL), f32),
        scratch_shapes=[pltpu.VMEM((te, tm), f32),
                        pltpu.VMEM((te // 2, tm), jnp.uint32)],
        compiler_params=_cparams(("parallel", "arbitrary")),
        name="peer_dense",
    )(xnt, u_b, v_b, e1, thr, e2b, h1, g)


_W_IN_SRC = ((1600, 3648), (3648, 5696), (0, 1024), (5696, 6208), (6208, 6720), (1024, 1536),
             (1536, 1600), (1568, 1600), (1536, 1568),
             (6720, 6752), (6720, 6752), (6720, 6752))


def _prep_w_in_kernel(w_ref, o_ref):
    col = 0
    tail = []
    for lo, hi in _W_IN_SRC:
        piece = w_ref[:, lo:hi]
        if (hi - lo) % LANE == 0:
            o_ref[:, col:col + hi - lo] = piece.astype(bf16)
            col += hi - lo
        else:
            tail.append(piece)
    rows = w_ref.shape[0]
    tail.append(jnp.zeros((rows, NP - col - sum(t.shape[1] for t in tail)), f32))
    o_ref[:, col:] = jnp.concatenate(tail, axis=1).astype(bf16)


def _prep_w_in(w, *, tr=256):
    _, k, n_in = w.shape
    assert k % tr == 0
    return pl.pallas_call(
        _prep_w_in_kernel,
        grid=(k // tr,),
        in_specs=[pl.BlockSpec((None, tr, n_in), lambda i: (0, i, 0))],
        out_specs=pl.BlockSpec((tr, NP), lambda i: (i, 0)),
        out_shape=jax.ShapeDtypeStruct((k, NP), bf16),
        compiler_params=_cparams(("parallel",)),
        name="prep_w_in",
    )(w)


def _prep_w_uq(w):
    w = w.reshape(Q_LORA, MLA_HEADS, QK_NOPE + QK_ROPE)
    nope, x1, x2 = w[..., :QK_NOPE], w[..., QK_NOPE:QK_NOPE + 32], w[..., QK_NOPE + 32:]
    return jnp.concatenate([nope, x1, x2, x2, x1], axis=-1).reshape(Q_LORA, MLA_HEADS * QK_SLAB).astype(bf16)


def _rope_tables(pos):
    inv_freq = ROPE_THETA ** (-jnp.arange(0, QK_ROPE, 2, dtype=f32) / QK_ROPE)
    ang = pos[:, None] * inv_freq[None, :]
    c, s = jnp.cos(ang), jnp.sin(ang)
    zero = jnp.zeros((pos.shape[0], LANE - QK_ROPE), f32)
    return jnp.concatenate([c, c, zero], axis=1), jnp.concatenate([-s, s, zero], axis=1)


def _expand_matrix(width):
    r = np.arange(LANE)[:, None]
    col = np.arange(SSD_HEADS * width)[None, :]
    return jnp.asarray(((r % SSD_HEADS) == (col // width)) & (r < 3 * SSD_HEADS), dtype=bf16)


def _tile(m, pref):
    t = min(pref, m)
    assert m % t == 0
    return t


def kernel(x, meta_tokens, attn_norm, w_in, q_norm, w_uq, kv_norm, w_ukv, conv_w, conv_b, dt_bias, a_log, d_skip,
           ssd_norm, w_out, ffn_norm, w_query, sub_keys, u_experts, v_experts, final_norm):
    bsz, seq, d = x.shape
    assert d == D_MODEL and seq % CHUNK == 0
    n_tok = bsz * seq
    x2d = x.reshape(n_tok, d)

    w_in_b = _prep_w_in(w_in)
    wq_b = _prep_w_uq(w_uq[0])
    wkv = w_ukv[0].reshape(KV_LORA, MLA_HEADS, QK_NOPE + V_HEAD)
    wk_b = wkv[..., :QK_NOPE].reshape(KV_LORA, MLA_HEADS * QK_NOPE).astype(bf16)
    wv_b = wkv[..., QK_NOPE:].reshape(KV_LORA, MLA_HEADS * V_HEAD).astype(bf16)
    w_out_b = w_out[0].astype(bf16)
    w_query_b = w_query[0].astype(bf16)
    u_b = u_experts[0].astype(bf16)
    v_b = v_experts[0].astype(bf16)
    cwx, cwbc = conv_w[0][:, :D_INNER], conv_w[0][:, D_INNER:]
    cbx, cbbc = conv_b[0][None, :D_INNER], conv_b[0][None, D_INNER:]
    rep3 = lambda v, fill: jnp.concatenate([v, v, v, jnp.full((LANE - 3 * SSD_HEADS,), fill, f32)])[None, :]
    dtb = rep3(dt_bias[0], 0.0)
    alog = rep3(a_log[0], 0.0)
    dskip = jnp.repeat(d_skip[0], SSD_HEAD_DIM)[None, :]
    e64 = _expand_matrix(SSD_HEAD_DIM)
    e128 = _expand_matrix(CHUNK)

    lead_rows = jnp.concatenate([jnp.zeros((LEAD_PAD, d), x.dtype), meta_tokens.astype(x.dtype)], axis=0)
    cc_l, ss_l = _rope_tables(jnp.arange(LEAD, dtype=f32) - LEAD_PAD)
    cc_r, ss_r = _rope_tables(jnp.arange(seq, dtype=f32) + N_META)

    g_attn = attn_norm[0][None, :]
    proj_l = _inproj(lead_rows, g_attn, w_in_b, tm=LEAD)
    proj_r = _inproj(x2d, g_attn, w_in_b, tm=_tile(n_tok, 512))

    qn, kvn = q_norm[0][None, :], kv_norm[0][None, :]
    _, k_l, v_l = _mla_up(proj_l, cc_l, ss_l, qn, kvn, wq_b, wk_b, wv_b, tm=LEAD, n_pos_blocks=1)
    tmu = _tile(seq, 256)
    q_r, k_r, v_r = _mla_up(proj_r, cc_r, ss_r, qn, kvn, wq_b, wk_b, wv_b, tm=tmu, n_pos_blocks=seq // tmu)
    a_out = _attention(q_r, k_r, v_r, k_l, v_l, bsz=bsz, seq=seq, tq=_tile(seq, 512))

    s_out = _ssd(proj_l, proj_r, cwx, cwbc, cbx, cbbc, dtb, alog, dskip, ssd_norm[0][None, :], e64, e128,
                 bsz=bsz, seq=seq)

    h1 = _outproj(a_out, s_out, w_out_b, x2d, tm=_tile(n_tok, 1024))

    xnt, e1, thr, e2b = _peer_q(h1, ffn_norm[0][None, :], w_query_b, sub_keys[0], tm=_tile(n_tok, 256))
    out = _peer_dense(xnt, u_b, v_b, e1, thr, e2b, h1, final_norm[None, :], tm=_tile(n_tok, 512), te=512)
    return out.reshape(bsz, seq, d)
```

```python
import functools

import jax
import jax.numpy as jnp
import numpy as np
from jax import lax
from jax.experimental import pallas as pl
from jax.experimental.pallas import tpu as pltpu

f32 = jnp.float32
bf16 = jnp.bfloat16

D_MODEL = 4096
N_META = 16
CHUNK = 128
LEAD = CHUNK
LEAD_PAD = CHUNK - N_META
MLA_HEADS = 16
QK_NOPE = 128
QK_ROPE = 64
V_HEAD = 128
Q_LORA = 1024
KV_LORA = 512
ROPE_THETA = 10000.0
D_INNER = 2048
SSD_HEAD_DIM = 64
SSD_HEADS = 32
SSD_GROUPS = 4
SSD_STATE = 128
CONV_W = 4
PEER_HEADS = 8
PEER_TOPK = 16
N_KEYS = 128
HALF_KEY = 128
EPS = 1e-6
NEG_INF = -1e30

LANE = 128
QK_SLAB = 2 * LANE
GROUP_W = D_INNER // SSD_GROUPS

P_Z = 0
P_XS = 2048
P_CQ = 4096
P_B = 5120
P_C = 5632
P_CKV = 6144
P_KR = 6656
P_DT = 6784
NP = 6912

VMEM_LIMIT = 56 * 1024 * 1024


def _cparams(sem, vmem=VMEM_LIMIT, flags=None):
    return pltpu.CompilerParams(dimension_semantics=sem, vmem_limit_bytes=vmem, flags=flags)


def _dot(a, b):
    return jnp.dot(a, b, preferred_element_type=f32)


def _dot_nt(a, b):
    return lax.dot_general(a, b, (((1,), (1,)), ((), ())), preferred_element_type=f32)


def _rms(x, g):
    return x * lax.rsqrt(jnp.mean(x * x, axis=-1, keepdims=True) + EPS) * g


def _inproj_kernel(x_ref, g_ref, w_ref, o_ref, xn_ref, *, sub):
    @pl.when(pl.program_id(1) == 0)
    def _():
        def body(r, c):
            rows = pl.ds(pl.multiple_of(r * sub, sub), sub)
            xn_ref[rows, :] = _rms(x_ref[rows, :], g_ref[...]).astype(bf16)
            return c
        lax.fori_loop(0, x_ref.shape[0] // sub, body, 0)

    o_ref[...] = _dot(xn_ref[...], w_ref[...])


def _inproj(x2d, g, w, *, tm, tn=768):
    m = x2d.shape[0]
    assert m % tm == 0 and NP % tn == 0
    sub = min(64, tm)
    return pl.pallas_call(
        functools.partial(_inproj_kernel, sub=sub),
        grid=(m // tm, NP // tn),
        in_specs=[pl.BlockSpec((tm, D_MODEL), lambda i, j: (i, 0)),
                  pl.BlockSpec((1, D_MODEL), lambda i, j: (0, 0)),
                  pl.BlockSpec((D_MODEL, tn), lambda i, j: (0, j))],
        out_specs=pl.BlockSpec((tm, tn), lambda i, j: (i, j)),
        out_shape=jax.ShapeDtypeStruct((m, NP), f32),
        scratch_shapes=[pltpu.VMEM((tm, D_MODEL), bf16)],
        compiler_params=_cparams(("parallel", "arbitrary")),
        name="inproj",
    )(x2d, g, w)


def _mla_up_kernel(cq_ref, ckv_ref, kr_ref, cc_ref, ss_ref, qn_ref, kvn_ref, wq_ref, wk_ref, wv_ref,
                   q_ref, k_ref, v_ref, *, scale):
    cc = cc_ref[...]
    ss = ss_ref[...]

    def rope(t):
        return t * cc + pltpu.roll(t, 64, 1) * ss

    cqn = _rms(cq_ref[...], qn_ref[...]).astype(bf16)
    q = _dot(cqn, wq_ref[...])
    for h in range(MLA_HEADS):
        lo = h * QK_SLAB
        q_ref[:, lo:lo + LANE] = (q[:, lo:lo + LANE] * scale).astype(bf16)
        q_ref[:, lo + LANE:lo + QK_SLAB] = (rope(q[:, lo + LANE:lo + QK_SLAB]) * scale).astype(bf16)

    kvn = _rms(ckv_ref[...], kvn_ref[...]).astype(bf16)
    kn = _dot(kvn, wk_ref[...])
    kpe = rope(kr_ref[...]).astype(bf16)
    for h in range(MLA_HEADS):
        lo = h * QK_SLAB
        k_ref[:, lo:lo + LANE] = kn[:, h * LANE:(h + 1) * LANE].astype(bf16)
        k_ref[:, lo + LANE:lo + QK_SLAB] = kpe
    v_ref[...] = _dot(kvn, wv_ref[...]).astype(bf16)


def _mla_up(proj, cc, ss, qn, kvn, wq, wk, wv, *, tm, n_pos_blocks):
    m = proj.shape[0]
    assert m % tm == 0
    scale = float((QK_NOPE + QK_ROPE) ** -0.5 * np.log2(np.e))
    hq = MLA_HEADS * QK_SLAB
    hv = MLA_HEADS * V_HEAD
    const = lambda i: (0, 0)
    return pl.pallas_call(
        functools.partial(_mla_up_kernel, scale=scale),
        grid=(m // tm,),
        in_specs=[pl.BlockSpec((tm, Q_LORA), lambda i: (i, P_CQ // Q_LORA)),
                  pl.BlockSpec((tm, KV_LORA), lambda i: (i, P_CKV // KV_LORA)),
                  pl.BlockSpec((tm, LANE), lambda i: (i, P_KR // LANE)),
                  pl.BlockSpec((tm, LANE), lambda i: (i % n_pos_blocks, 0)),
                  pl.BlockSpec((tm, LANE), lambda i: (i % n_pos_blocks, 0)),
                  pl.BlockSpec((1, Q_LORA), const),
                  pl.BlockSpec((1, KV_LORA), const),
                  pl.BlockSpec((Q_LORA, hq), const),
                  pl.BlockSpec((KV_LORA, hv), const),
                  pl.BlockSpec((KV_LORA, hv), const)],
        out_specs=[pl.BlockSpec((tm, hq), lambda i: (i, 0)),
                   pl.BlockSpec((tm, hq), lambda i: (i, 0)),
                   pl.BlockSpec((tm, hv), lambda i: (i, 0))],
        out_shape=[jax.ShapeDtypeStruct((m, hq), bf16),
                   jax.ShapeDtypeStruct((m, hq), bf16),
                   jax.ShapeDtypeStruct((m, hv), bf16)],
        compiler_params=_cparams(("parallel",)),
        name="mla_up",
    )(proj, proj, proj, cc, ss, qn, kvn, wq, wk, wv)


def _attn_kernel(q_ref, k_ref, v_ref, kl_ref, vl_ref, o_ref, *, tq, nh):
    i = pl.program_id(2)
    qs = [q_ref[:, h * QK_SLAB:(h + 1) * QK_SLAB] for h in range(nh)]
    ksl = lambda h: slice(h * QK_SLAB, (h + 1) * QK_SLAB)
    vsl = lambda h: slice(h * V_HEAD, (h + 1) * V_HEAD)

    diag = pl.ds(pl.multiple_of(i * tq, tq), tq)
    col = lax.broadcasted_iota(jnp.int32, (tq, LEAD), 1)
    r_id = lax.broadcasted_iota(jnp.int32, (tq, tq), 0)
    c_id = lax.broadcasted_iota(jnp.int32, (tq, tq), 1)
    carry = []
    for h in range(nh):
        s_l = jnp.where(col >= LEAD_PAD, _dot_nt(qs[h], kl_ref[:, ksl(h)]), NEG_INF)
        s_d = jnp.where(c_id <= r_id, _dot_nt(qs[h], k_ref[diag, ksl(h)]), NEG_INF)
        m = jnp.maximum(jnp.max(s_l, axis=-1, keepdims=True), jnp.max(s_d, axis=-1, keepdims=True))
        p_l = jnp.exp2(s_l - m)
        p_d = jnp.exp2(s_d - m)
        l = jnp.sum(p_l, axis=-1, keepdims=True) + jnp.sum(p_d, axis=-1, keepdims=True)
        acc = _dot(p_l.astype(bf16), vl_ref[:, vsl(h)]) + _dot(p_d.astype(bf16), v_ref[diag, vsl(h)])
        carry += [m, l, acc]

    def update(s, vb, m, l, acc):
        m_new = jnp.maximum(m, jnp.max(s, axis=-1, keepdims=True))
        alpha = jnp.exp2(m - m_new)
        p = jnp.exp2(s - m_new)
        l = alpha * l + jnp.sum(p, axis=-1, keepdims=True)
        acc = alpha * acc + _dot(p.astype(bf16), vb)
        return [m_new, l, acc]

    def body(j, carry):
        rows = pl.ds(pl.multiple_of(j * tq, tq), tq)
        out = []
        for h in range(nh):
            out += update(_dot_nt(qs[h], k_ref[rows, ksl(h)]), v_ref[rows, vsl(h)], *carry[3 * h:3 * h + 3])
        return tuple(out)

    carry = lax.fori_loop(0, i, body, tuple(carry))
    for h in range(nh):
        o_ref[:, vsl(h)] = (carry[3 * h + 2] / carry[3 * h + 1]).astype(bf16)


def _attention(q, k, v, kl, vl, *, bsz, seq, tq, nh=4):
    assert seq % tq == 0 and MLA_HEADS % nh == 0
    nq = seq // tq
    kw, vw = nh * QK_SLAB, nh * V_HEAD
    return pl.pallas_call(
        functools.partial(_attn_kernel, tq=tq, nh=nh),
        grid=(bsz, MLA_HEADS // nh, nq),
        in_specs=[pl.BlockSpec((tq, kw), lambda b, h, i: (b * nq + i, h)),
                  pl.BlockSpec((seq, kw), lambda b, h, i: (b, h)),
                  pl.BlockSpec((seq, vw), lambda b, h, i: (b, h)),
                  pl.BlockSpec((LEAD, kw), lambda b, h, i: (0, h)),
                  pl.BlockSpec((LEAD, vw), lambda b, h, i: (0, h))],
        out_specs=pl.BlockSpec((tq, vw), lambda b, h, i: (b * nq + i, h)),
        out_shape=jax.ShapeDtypeStruct((bsz * seq, MLA_HEADS * V_HEAD), bf16),
        compiler_params=_cparams(("parallel", "parallel", "arbitrary")),
        name="mla_attn",
    )(q, k, v, kl, vl)


def _split3(v):
    b0 = v.astype(bf16).astype(f32)
    r1 = v - b0
    b1 = r1.astype(bf16).astype(f32)
    b2 = (r1 - b1).astype(bf16).astype(f32)
    return b0, b1, b2


def _ssd_kernel(zx_l, bc_l, dt_l, zx_r, bc_r, dt_r, cwx, cwbc, cbx, cbbc, dtb, alog, dskip, ng, e64, e128,
                o_ref, st_ref, px_ref, pbc_ref):
    c = pl.program_id(1)
    is_lead = c == 0

    @pl.when(is_lead)
    def _():
        st_ref[...] = jnp.zeros_like(st_ref)
        px_ref[...] = jnp.zeros_like(px_ref)
        pbc_ref[...] = jnp.zeros_like(pbc_ref)

    zx = jnp.where(is_lead, zx_l[...], zx_r[...])
    bcraw = jnp.where(is_lead, bc_l[...], bc_r[...])
    dtraw = jnp.where(is_lead, dt_l[...], dt_r[...])
    z = zx[:, :D_INNER]
    xraw = zx[:, D_INNER:]

    row1 = lax.broadcasted_iota(jnp.int32, (CHUNK, 1), 0)
    r_id = lax.broadcasted_iota(jnp.int32, (CHUNK, CHUNK), 0)
    c_id = lax.broadcasted_iota(jnp.int32, (CHUNK, CHUNK), 1)
    tril = r_id >= c_id

    def conv_silu(cur, prev_ref, w_ref, b_ref):
        prev = prev_ref[...]
        acc = cur * w_ref[CONV_W - 1:CONV_W, :] + b_ref[...]
        for k in range(1, CONV_W):
            comb = jnp.where(row1 >= CHUNK - k, prev, cur)
            acc = acc + pltpu.roll(comb, k, 0) * w_ref[CONV_W - 1 - k:CONV_W - k, :]
        prev_ref[...] = cur
        return acc * jax.nn.sigmoid(acc)

    xs = conv_silu(xraw, px_ref, cwx, cbx)
    bcc = conv_silu(bcraw, pbc_ref, cwbc, cbbc)
    b_all = bcc[:, :SSD_GROUPS * SSD_STATE]
    c_all = bcc[:, SSD_GROUPS * SSD_STATE:].astype(bf16)

    pre = dtraw + dtb[...]
    dt = jnp.maximum(pre, 0.0) + jnp.log1p(jnp.exp(-jnp.abs(pre)))
    dt = jnp.where(jnp.logical_and(is_lead, row1 < LEAD_PAD), 0.0, dt)
    da = dt * (-jnp.exp(alog[...]))
    tri = tril.astype(f32).astype(bf16)
    d0, d1, d2 = _split3(da)
    cs = _dot(tri, d0.astype(bf16)) + _dot(tri, d1.astype(bf16)) + _dot(tri, d2.astype(bf16))
    cs_t = cs.T
    ecs = jnp.exp(cs)
    dec = jnp.exp(cs[CHUNK - 1:CHUNK, :] - cs)

    def kcat(v):
        b0, b1, b2 = _split3(v)
        return jnp.where(c_id < 32, b0, jnp.where(c_id < 64, b1, jnp.where(c_id < 96, b2, 0.0))).astype(bf16)

    ex = _dot(jnp.concatenate([kcat(dt), kcat(ecs), kcat(dec)], axis=0), e64[...])
    dt_x = ex[0:CHUNK]
    ecs_x = ex[CHUNK:2 * CHUNK]
    dec_x = ex[2 * CHUNK:3 * CHUNK]
    cs_bc = _dot(kcat(cs), e128[...])

    xdt = xs * dt_x
    xdt_b = xdt.astype(bf16)
    xd_b = (xdt * dec_x).astype(bf16)

    ys = []
    for g in range(SSD_GROUPS):
        gl = slice(g * GROUP_W, (g + 1) * GROUP_W)
        b_g = b_all[:, g * SSD_STATE:(g + 1) * SSD_STATE]
        c_g = c_all[:, g * SSD_STATE:(g + 1) * SSD_STATE]
        cb = _dot_nt(c_g, b_g.astype(bf16))
        st_g = st_ref[:, gl]
        y_g = _dot(c_g, st_g.astype(bf16)) * ecs_x[:, gl]
        pieces = []
        for pr in range(GROUP_W // LANE):
            h0 = g * (GROUP_W // SSD_HEAD_DIM) + 2 * pr
            xpair = xdt_b[:, h0 * SSD_HEAD_DIM:(h0 + 2) * SSD_HEAD_DIM]
            yh = []
            for h in (h0, h0 + 1):
                seg = cs_bc[:, h * CHUNK:(h + 1) * CHUNK] - cs_t[h:h + 1, :]
                lm = jnp.where(tril, jnp.exp(jnp.minimum(seg, 0.0)), 0.0)
                yh.append(_dot((cb * lm).astype(bf16), xpair))
            pieces.append(jnp.where(c_id < SSD_HEAD_DIM, yh[0], yh[1]))
        y_g = y_g + jnp.concatenate(pieces, axis=1)
        st_ref[:, gl] = st_g * ecs_x[CHUNK - 1:CHUNK, gl] + _dot(b_g.T.astype(bf16), xd_b[:, gl])
        ys.append(y_g)

    y = jnp.concatenate(ys, axis=1) + xs * dskip[...]
    gated = y * (z * jax.nn.sigmoid(z))
    outs = []
    for g in range(SSD_GROUPS):
        gg = gated[:, g * GROUP_W:(g + 1) * GROUP_W]
        outs.append(gg * lax.rsqrt(jnp.mean(gg * gg, axis=-1, keepdims=True) + EPS))
    o_ref[...] = (jnp.concatenate(outs, axis=1) * ng[...]).astype(bf16)


def _ssd(proj_l, proj_r, cwx, cwbc, cbx, cbbc, dtb, alog, dskip, ng, e64, e128, *, bsz, seq):
    nch = seq // CHUNK
    real = lambda w: (lambda b, c: (b * nch + jnp.maximum(c - 1, 0), w))
    lead = lambda w: (lambda b, c: (0, w))
    const = lambda b, c: (0, 0)
    zx_w, bc_w, dt_w = 2 * D_INNER, 2 * SSD_GROUPS * SSD_STATE, LANE
    return pl.pallas_call(
        _ssd_kernel,
        grid=(bsz, nch + 1),
        in_specs=[pl.BlockSpec((CHUNK, zx_w), lead(P_Z // zx_w)),
                  pl.BlockSpec((CHUNK, bc_w), lead(P_B // bc_w)),
                  pl.BlockSpec((CHUNK, dt_w), lead(P_DT // dt_w)),
                  pl.BlockSpec((CHUNK, zx_w), real(P_Z // zx_w)),
                  pl.BlockSpec((CHUNK, bc_w), real(P_B // bc_w)),
                  pl.BlockSpec((CHUNK, dt_w), real(P_DT // dt_w)),
                  pl.BlockSpec((CONV_W, D_INNER), const),
                  pl.BlockSpec((CONV_W, bc_w), const),
                  pl.BlockSpec((1, D_INNER), const),
                  pl.BlockSpec((1, bc_w), const),
                  pl.BlockSpec((1, LANE), const),
                  pl.BlockSpec((1, LANE), const),
                  pl.BlockSpec((1, D_INNER), const),
                  pl.BlockSpec((1, D_INNER), const),
                  pl.BlockSpec((LANE, D_INNER), const),
                  pl.BlockSpec((LANE, SSD_HEADS * CHUNK), const)],
        out_specs=pl.BlockSpec((CHUNK, D_INNER), real(0)),
        out_shape=jax.ShapeDtypeStruct((bsz * seq, D_INNER), bf16),
        scratch_shapes=[pltpu.VMEM((SSD_STATE, D_INNER), f32),
                        pltpu.VMEM((CHUNK, D_INNER), f32),
                        pltpu.VMEM((CHUNK, bc_w), f32)],
        compiler_params=_cparams(("parallel", "arbitrary")),
        name="ssd",
    )(proj_l, proj_l, proj_l, proj_r, proj_r, proj_r, cwx, cwbc, cbx, cbbc, dtb, alog, dskip, ng, e64, e128)


def _outproj_kernel(a_ref, s_ref, wa_ref, ws_ref, x_ref, o_ref):
    o_ref[...] = _dot(a_ref[...], wa_ref[...]) + _dot(s_ref[...], ws_ref[...]) + x_ref[...]


def _outproj(a, s, w, x2d, *, tm, tn=512):
    m = a.shape[0]
    ka = a.shape[1]
    assert m % tm == 0 and D_MODEL % tn == 0
    return pl.pallas_call(
        _outproj_kernel,
        grid=(m // tm, D_MODEL // tn),
        in_specs=[pl.BlockSpec((tm, ka), lambda i, j: (i, 0)),
                  pl.BlockSpec((tm, D_INNER), lambda i, j: (i, 0)),
                  pl.BlockSpec((ka, tn), lambda i, j: (0, j)),
                  pl.BlockSpec((D_INNER, tn), lambda i, j: (1, j)),
                  pl.BlockSpec((tm, tn), lambda i, j: (i, j))],
        out_specs=pl.BlockSpec((tm, tn), lambda i, j: (i, j)),
        out_shape=jax.ShapeDtypeStruct((m, D_MODEL), f32),
        compiler_params=_cparams(("parallel", "arbitrary")),
        name="outproj",
    )(a, s, w, w, x2d)


SUBLANES = 8


def _oddeven_mergesort_pairs(n):
    pairs = []
    p = 1
    while p < n:
        k = p
        while k >= 1:
            for j in range(k % p, n - k, 2 * k):
                for i in range(min(k, n - j - k)):
                    if (i + j) // (2 * p) == (i + j + k) // (2 * p):
                        pairs.append((i + j, i + j + k))
            k //= 2
        p *= 2
    return pairs


def _sorted_top17(x):
    n_grp = x.shape[0] // SUBLANES
    assert n_grp == 16
    r = [x[g * SUBLANES:(g + 1) * SUBLANES, :] for g in range(n_grp)]

    def cmpx(a, b):
        r[a], r[b] = jnp.maximum(r[a], r[b]), jnp.minimum(r[a], r[b])

    for a, b in _oddeven_mergesort_pairs(n_grp):
        cmpx(a, b)
    for shift in (4, 2, 1):
        other = [pltpu.roll(v, shift, 0) for v in r]
        r = [jnp.maximum(r[i], other[n_grp - 1 - i]) for i in range(n_grp)]
        k = n_grp // 2
        while k >= 1:
            for i in range(n_grp):
                if i & k == 0:
                    cmpx(i, i + k)
            k //= 2
    below = [jnp.where(x[g * SUBLANES:(g + 1) * SUBLANES, :] < r[n_grp - 1], x[g * SUBLANES:(g + 1) * SUBLANES, :],
                       -jnp.inf) for g in range(n_grp)]
    while len(below) > 1:
        below = [jnp.maximum(below[i], below[i + 1]) for i in range(0, len(below), 2)]
    return [v[0:1, :] for v in r] + [jnp.max(below[0], axis=0, keepdims=True)]


def _stack_rows(vals):
    n = len(vals)
    rid = lax.broadcasted_iota(jnp.int32, (n, vals[0].shape[1]), 0)
    out = jnp.zeros((n, vals[0].shape[1]), f32)
    for r, v in enumerate(vals):
        out = jnp.where(rid == r, v, out)
    return out


N_CAND = PEER_TOPK + (PEER_TOPK // 2 - 1) * (PEER_TOPK // 2) + PEER_TOPK


def _peer_q_kernel(h_ref, g_ref, wq_ref, sk_ref, xnt_ref, e1_ref, thr_ref, e2_ref, cand_ref):
    xn = _rms(h_ref[...], g_ref[...])
    xnt_ref[...] = xn.T.astype(bf16)
    q = _dot(xn.astype(bf16), wq_ref[...]).astype(bf16)
    sk1 = sk_ref[0].astype(bf16)
    sk2 = sk_ref[1].astype(bf16)
    k = PEER_TOPK
    half = k // 2
    for h in range(PEER_HEADS):
        lo = h * 2 * HALF_KEY
        s1 = _dot_nt(sk1, q[:, lo:lo + HALF_KEY])
        s2 = _dot_nt(sk2, q[:, lo + HALF_KEY:lo + 2 * HALF_KEY])
        v1 = _sorted_top17(s1)
        v2 = _sorted_top17(s2)
        v2_top = _stack_rows(v2[:k])
        cand_ref[0:k, :] = v1[0] + v2_top
        for a in range(1, half):
            cand_ref[k + (a - 1) * half:k + a * half, :] = v1[a] + v2_top[0:half, :]
        cand_ref[k + (half - 1) * half:k + half * half, :] = _stack_rows(v1[half:k]) + v2[0]
        ninf = jnp.full_like(v1[0], -jnp.inf)
        cand_ref[k + half * half:N_CAND, :] = _stack_rows([v1[0] + v2[k], v1[k] + v2[0]] + [ninf] * (half - 2))
        cand_ref[N_CAND:, :] = jnp.full((N_KEYS - N_CAND, s1.shape[1]), -jnp.inf, f32)
        top = _sorted_top17(cand_ref[...])
        zsum = jnp.zeros_like(top[0])
        for t in top[:k]:
            zsum = zsum + jnp.exp(t - top[0])
        tau = 0.5 * (top[k - 1] + top[k])
        scale = 0.5 / zsum
        e1_ref[h] = jnp.exp(s1 - v1[0])
        thr_ref[h] = jnp.exp((tau - v2[0]) - s1) * scale
        e2_ref[h] = pltpu.bitcast((jnp.exp(s2 - v2[0]) * scale).astype(bf16), jnp.uint32)


def _peer_q(h1, g, wq, sub_keys, *, tm):
    n = h1.shape[0]
    assert n % tm == 0
    nq = PEER_HEADS * 2 * HALF_KEY
    per_head = lambda rows, dt: (pl.BlockSpec((PEER_HEADS, rows, tm), lambda i: (0, 0, i)),
                                 jax.ShapeDtypeStruct((PEER_HEADS, rows, n), dt))
    specs, shapes = zip((pl.BlockSpec((D_MODEL, tm), lambda i: (0, i)), jax.ShapeDtypeStruct((D_MODEL, n), bf16)),
                        per_head(N_KEYS, f32), per_head(N_KEYS, f32), per_head(N_KEYS // 2, jnp.uint32))
    return pl.pallas_call(
        _peer_q_kernel,
        grid=(n // tm,),
        in_specs=[pl.BlockSpec((tm, D_MODEL), lambda i: (i, 0)),
                  pl.BlockSpec((1, D_MODEL), lambda i: (0, 0)),
                  pl.BlockSpec((D_MODEL, nq), lambda i: (0, 0), pipeline_mode=pl.Buffered(1)),
                  pl.BlockSpec((2, N_KEYS, HALF_KEY), lambda i: (0, 0, 0))],
        out_specs=list(specs),
        out_shape=list(shapes),
        scratch_shapes=[pltpu.VMEM((N_KEYS, tm), f32)],
        compiler_params=_cparams(("parallel",)),
        name="peer_query",
    )(h1, g, wq, sub_keys)


def _peer_dense_kernel(xnt_ref, u_ref, v_ref, e1_ref, thr_ref, e2_ref, h_ref, g_ref, o_ref, ht_ref, act_ref, *, te):
    j = pl.program_id(1)
    tm = xnt_ref.shape[1]
    pk = N_KEYS // 2

    @pl.when(j == 0)
    def _():
        o_ref[...] = jnp.zeros_like(o_ref)

    ht_ref[...] = _dot(u_ref[...], xnt_ref[...])
    for ii in range(te // N_KEYS):
        i1 = j * (te // N_KEYS) + ii
        e1rows = [e1_ref[h, pl.ds(i1, 1), :] for h in range(PEER_HEADS)]
        thrrows = [thr_ref[h, pl.ds(i1, 1), :] for h in range(PEER_HEADS)]
        for c in range(tm // LANE):
            cl = slice(c * LANE, (c + 1) * LANE)
            w = jnp.zeros((N_KEYS, LANE), bf16)
            for h in range(PEER_HEADS):
                e1b = jnp.broadcast_to(e1rows[h][:, cl], (N_KEYS, LANE)).astype(bf16)
                thrb = jnp.broadcast_to(thrrows[h][:, cl], (N_KEYS, LANE)).astype(bf16)
                e2 = pltpu.bitcast(e2_ref[h, :, cl], bf16)
                w = w + jnp.where(e2 >= thrb, e2, jnp.zeros_like(e2)) * e1b
            hh = ht_ref[ii * N_KEYS:(ii + 1) * N_KEYS, cl]
            g = hh * (1.0 + lax.erf(hh * np.float32(1.0 / np.sqrt(2.0))))
            act_ref[ii * pk:(ii + 1) * pk, cl] = pltpu.bitcast(g.astype(bf16) * w, jnp.uint32)
    o_ref[...] += _dot(pltpu.bitcast(act_ref[...], bf16).T, v_ref[...])

    hr = h_ref.shape[0]
    slab = pl.ds(pl.multiple_of(j * hr, hr), hr)
    o_ref[slab, :] += h_ref[...]

    @pl.when(j == pl.num_programs(1) - 1)
    def _():
        sub = 64

        def body(r, c):
            rows = pl.ds(pl.multiple_of(r * sub, sub), sub)
            o_ref[rows, :] = _rms(o_ref[rows, :], g_ref[...])
            return c
        lax.fori_loop(0, tm // sub, body, 0)


def _peer_dense(xnt, u_b, v_b, e1, thr, e2b, h1, g, *, tm, te):
    n = xnt.shape[1]
    n_exp = u_b.shape[0]
    nj = n_exp // te
    assert n % tm == 0 and n_exp % te == 0 and te % N_KEYS == 0 and tm % LANE == 0 and tm % (8 * nj) == 0
    hr = tm // nj
    return pl.pallas_call(
        functools.partial(_peer_dense_kernel, te=te),
        grid=(n // tm, nj),
        in_specs=[pl.BlockSpec((D_MODEL, tm), lambda i, j: (0, i)),
                  pl.BlockSpec((te, D_MODEL), lambda i, j: (j, 0)),
                  pl.BlockSpec((te, D_MODEL), lambda i, j: (j, 0)),
                  pl.BlockSpec((PEER_HEADS, N_KEYS, tm), lambda i, j: (0, 0, i)),
                  pl.BlockSpec((PEER_HEADS, N_KEYS, tm), lambda i, j: (0, 0, i)),
                  pl.BlockSpec((PEER_HEADS, N_KEYS // 2, tm), lambda i, j: (0, 0, i)),
                  pl.BlockSpec((hr, D_MODEL), lambda i, j: (i * nj + j, 0)),
                  pl.BlockSpec((1, D_MODEL), lambda i, j: (0, 0))],
        out_specs=pl.BlockSpec((tm, D_MODEL), lambda i, j: (i, 0)),
        out_shape=jax.ShapeDtypeStruct((n, D_MODEL), f32),
        scratch_shapes=[pltpu.VMEM((te, tm), f32),
                        pltpu.VMEM((te // 2, tm), jnp.uint32)],
        compiler_params=_cparams(("parallel", "arbitrary")),
        name="peer_dense",
    )(xnt, u_b, v_b, e1, thr, e2b, h1, g)


_W_IN_SRC = ((1600, 3648), (3648, 5696), (0, 1024), (5696, 6208), (6208, 6720), (1024, 1536),
             (1536, 1600), (1568, 1600), (1536, 1568),
             (6720, 6752), (6720, 6752), (6720, 6752))


def _prep_w_in_kernel(w_ref, o_ref):
    col = 0
    tail = []
    for lo, hi in _W_IN_SRC:
        piece = w_ref[:, lo:hi]
        if (hi - lo) % LANE == 0:
            o_ref[:, col:col + hi - lo] = piece.astype(bf16)
            col += hi - lo
        else:
            tail.append(piece)
    rows = w_ref.shape[0]
    tail.append(jnp.zeros((rows, NP - col - sum(t.shape[1] for t in tail)), f32))
    o_ref[:, col:] = jnp.concatenate(tail, axis=1).astype(bf16)


def _prep_w_in(w, *, tr=256):
    _, k, n_in = w.shape
    assert k % tr == 0
    return pl.pallas_call(
        _prep_w_in_kernel,
        grid=(k // tr,),
        in_specs=[pl.BlockSpec((None, tr, n_in), lambda i: (0, i, 0))],
        out_specs=pl.BlockSpec((tr, NP), lambda i: (i, 0)),
        out_shape=jax.ShapeDtypeStruct((k, NP), bf16),
        compiler_params=_cparams(("parallel",)),
        name="prep_w_in",
    )(w)


def _prep_w_uq(w):
    w = w.reshape(Q_LORA, MLA_HEADS, QK_NOPE + QK_ROPE)
    nope, x1, x2 = w[..., :QK_NOPE], w[..., QK_NOPE:QK_NOPE + 32], w[..., QK_NOPE + 32:]
    return jnp.concatenate([nope, x1, x2, x2, x1], axis=-1).reshape(Q_LORA, MLA_HEADS * QK_SLAB).astype(bf16)


def _rope_tables(pos):
    inv_freq = ROPE_THETA ** (-jnp.arange(0, QK_ROPE, 2, dtype=f32) / QK_ROPE)
    ang = pos[:, None] * inv_freq[None, :]
    c, s = jnp.cos(ang), jnp.sin(ang)
    zero = jnp.zeros((pos.shape[0], LANE - QK_ROPE), f32)
    return jnp.concatenate([c, c, zero], axis=1), jnp.concatenate([-s, s, zero], axis=1)


def _expand_matrix(width):
    r = np.arange(LANE)[:, None]
    col = np.arange(SSD_HEADS * width)[None, :]
    return jnp.asarray(((r % SSD_HEADS) == (col // width)) & (r < 3 * SSD_HEADS), dtype=bf16)


def _tile(m, pref):
    t = min(pref, m)
    assert m % t == 0
    return t


def kernel(x, meta_tokens, attn_norm, w_in, q_norm, w_uq, kv_norm, w_ukv, conv_w, conv_b, dt_bias, a_log, d_skip,
           ssd_norm, w_out, ffn_norm, w_query, sub_keys, u_experts, v_experts, final_norm):
    bsz, seq, d = x.shape
    assert d == D_MODEL and seq % CHUNK == 0
    n_tok = bsz * seq
    x2d = x.reshape(n_tok, d)

    w_in_b = _prep_w_in(w_in)
    wq_b = _prep_w_uq(w_uq[0])
    wkv = w_ukv[0].reshape(KV_LORA, MLA_HEADS, QK_NOPE + V_HEAD)
    wk_b = wkv[..., :QK_NOPE].reshape(KV_LORA, MLA_HEADS * QK_NOPE).astype(bf16)
    wv_b = wkv[..., QK_NOPE:].reshape(KV_LORA, MLA_HEADS * V_HEAD).astype(bf16)
    w_out_b = w_out[0].astype(bf16)
    w_query_b = w_query[0].astype(bf16)
    u_b = u_experts[0].astype(bf16)
    v_b = v_experts[0].astype(bf16)
    cwx, cwbc = conv_w[0][:, :D_INNER], conv_w[0][:, D_INNER:]
    cbx, cbbc = conv_b[0][None, :D_INNER], conv_b[0][None, D_INNER:]
    rep3 = lambda v, fill: jnp.concatenate([v, v, v, jnp.full((LANE - 3 * SSD_HEADS,), fill, f32)])[None, :]
    dtb = rep3(dt_bias[0], 0.0)
    alog = rep3(a_log[0], 0.0)
    dskip = jnp.repeat(d_skip[0], SSD_HEAD_DIM)[None, :]
    e64 = _expand_matrix(SSD_HEAD_DIM)
    e128 = _expand_matrix(CHUNK)

    lead_rows = jnp.concatenate([jnp.zeros((LEAD_PAD, d), x.dtype), meta_tokens.astype(x.dtype)], axis=0)
    cc_l, ss_l = _rope_tables(jnp.arange(LEAD, dtype=f32) - LEAD_PAD)
    cc_r, ss_r = _rope_tables(jnp.arange(seq, dtype=f32) + N_META)

    g_attn = attn_norm[0][None, :]
    proj_l = _inproj(lead_rows, g_attn, w_in_b, tm=LEAD)
    proj_r = _inproj(x2d, g_attn, w_in_b, tm=_tile(n_tok, 512))

    qn, kvn = q_norm[0][None, :], kv_norm[0][None, :]
    _, k_l, v_l = _mla_up(proj_l, cc_l, ss_l, qn, kvn, wq_b, wk_b, wv_b, tm=LEAD, n_pos_blocks=1)
    tmu = _tile(seq, 256)
    q_r, k_r, v_r = _mla_up(proj_r, cc_r, ss_r, qn, kvn, wq_b, wk_b, wv_b, tm=tmu, n_pos_blocks=seq // tmu)
    a_out = _attention(q_r, k_r, v_r, k_l, v_l, bsz=bsz, seq=seq, tq=_tile(seq, 512))

    s_out = _ssd(proj_l, proj_r, cwx, cwbc, cbx, cbbc, dtb, alog, dskip, ssd_norm[0][None, :], e64, e128,
                 bsz=bsz, seq=seq)

    h1 = _outproj(a_out, s_out, w_out_b, x2d, tm=_tile(n_tok, 1024))

    xnt, e1, thr, e2b = _peer_q(h1, ffn_norm[0][None, :], w_query_b, sub_keys[0], tm=_tile(n_tok, 256))
    out = _peer_dense(xnt, u_b, v_b, e1, thr, e2b, h1, final_norm[None, :], tm=_tile(n_tok, 512), te=512)
    return out.reshape(bsz, seq, d)
```

```python
import functools

import jax
import jax.numpy as jnp
import numpy as np
from jax import lax
from jax.experimental import pallas as pl
from jax.experimental.pallas import tpu as pltpu

f32 = jnp.float32
bf16 = jnp.bfloat16

D_MODEL = 4096
N_META = 16
CHUNK = 128
LEAD = CHUNK
LEAD_PAD = CHUNK - N_META
MLA_HEADS = 16
QK_NOPE = 128
QK_ROPE = 64
V_HEAD = 128
Q_LORA = 1024
KV_LORA = 512
ROPE_THETA = 10000.0
D_INNER = 2048
SSD_HEAD_DIM = 64
SSD_HEADS = 32
SSD_GROUPS = 4
SSD_STATE = 128
CONV_W = 4
PEER_HEADS = 8
PEER_TOPK = 16
N_KEYS = 128
HALF_KEY = 128
EPS = 1e-6
NEG_INF = -1e30

LANE = 128
QK_SLAB = 2 * LANE
GROUP_W = D_INNER // SSD_GROUPS

P_Z = 0
P_XS = 2048
P_CQ = 4096
P_B = 5120
P_C = 5632
P_CKV = 6144
P_KR = 6656
P_DT = 6784
NP = 6912

VMEM_LIMIT = 56 * 1024 * 1024


def _cparams(sem, vmem=VMEM_LIMIT, flags=None):
    return pltpu.CompilerParams(dimension_semantics=sem, vmem_limit_bytes=vmem, flags=flags)


def _dot(a, b):
    return jnp.dot(a, b, preferred_element_type=f32)


def _dot_nt(a, b):
    return lax.dot_general(a, b, (((1,), (1,)), ((), ())), preferred_element_type=f32)


def _rms(x, g):
    return x * lax.rsqrt(jnp.mean(x * x, axis=-1, keepdims=True) + EPS) * g


def _inproj_kernel(x_ref, g_ref, w_ref, o_ref, xn_ref, *, sub):
    @pl.when(pl.program_id(1) == 0)
    def _():
        def body(r, c):
            rows = pl.ds(pl.multiple_of(r * sub, sub), sub)
            xn_ref[rows, :] = _rms(x_ref[rows, :], g_ref[...]).astype(bf16)
            return c
        lax.fori_loop(0, x_ref.shape[0] // sub, body, 0)

    o_ref[...] = _dot(xn_ref[...], w_ref[...])


def _inproj(x2d, g, w, *, tm, tn=768):
    m = x2d.shape[0]
    assert m % tm == 0 and NP % tn == 0
    sub = min(64, tm)
    return pl.pallas_call(
        functools.partial(_inproj_kernel, sub=sub),
        grid=(m // tm, NP // tn),
        in_specs=[pl.BlockSpec((tm, D_MODEL), lambda i, j: (i, 0)),
                  pl.BlockSpec((1, D_MODEL), lambda i, j: (0, 0)),
                  pl.BlockSpec((D_MODEL, tn), lambda i, j: (0, j))],
        out_specs=pl.BlockSpec((tm, tn), lambda i, j: (i, j)),
        out_shape=jax.ShapeDtypeStruct((m, NP), f32),
        scratch_shapes=[pltpu.VMEM((tm, D_MODEL), bf16)],
        compiler_params=_cparams(("parallel", "arbitrary")),
        name="inproj",
    )(x2d, g, w)


def _mla_up_kernel(cq_ref, ckv_ref, kr_ref, cc_ref, ss_ref, qn_ref, kvn_ref, wq_ref, wk_ref, wv_ref,
                   q_ref, k_ref, v_ref, *, scale):
    cc = cc_ref[...]
    ss = ss_ref[...]

    def rope(t):
        return t * cc + pltpu.roll(t, 64, 1) * ss

    cqn = _rms(cq_ref[...], qn_ref[...]).astype(bf16)
    q = _dot(cqn, wq_ref[...])
    for h in range(MLA_HEADS):
        lo = h * QK_SLAB
        q_ref[:, lo:lo + LANE] = (q[:, lo:lo + LANE] * scale).astype(bf16)
        q_ref[:, lo + LANE:lo + QK_SLAB] = (rope(q[:, lo + LANE:lo + QK_SLAB]) * scale).astype(bf16)

    kvn = _rms(ckv_ref[...], kvn_ref[...]).astype(bf16)
    kn = _dot(kvn, wk_ref[...])
    kpe = rope(kr_ref[...]).astype(bf16)
    for h in range(MLA_HEADS):
        lo = h * QK_SLAB
        k_ref[:, lo:lo + LANE] = kn[:, h * LANE:(h + 1) * LANE].astype(bf16)
        k_ref[:, lo + LANE:lo + QK_SLAB] = kpe
    v_ref[...] = _dot(kvn, wv_ref[...]).astype(bf16)


def _mla_up(proj, cc, ss, qn, kvn, wq, wk, wv, *, tm, n_pos_blocks):
    m = proj.shape[0]
    assert m % tm == 0
    scale = float((QK_NOPE + QK_ROPE) ** -0.5 * np.log2(np.e))
    hq = MLA_HEADS * QK_SLAB
    hv = MLA_HEADS * V_HEAD
    const = lambda i: (0, 0)
    return pl.pallas_call(
        functools.partial(_mla_up_kernel, scale=scale),
        grid=(m // tm,),
        in_specs=[pl.BlockSpec((tm, Q_LORA), lambda i: (i, P_CQ // Q_LORA)),
                  pl.BlockSpec((tm, KV_LORA), lambda i: (i, P_CKV // KV_LORA)),
                  pl.BlockSpec((tm, LANE), lambda i: (i, P_KR // LANE)),
                  pl.BlockSpec((tm, LANE), lambda i: (i % n_pos_blocks, 0)),
                  pl.BlockSpec((tm, LANE), lambda i: (i % n_pos_blocks, 0)),
                  pl.BlockSpec((1, Q_LORA), const),
                  pl.BlockSpec((1, KV_LORA), const),
                  pl.BlockSpec((Q_LORA, hq), const),
                  pl.BlockSpec((KV_LORA, hv), const),
                  pl.BlockSpec((KV_LORA, hv), const)],
        out_specs=[pl.BlockSpec((tm, hq), lambda i: (i, 0)),
                   pl.BlockSpec((tm, hq), lambda i: (i, 0)),
                   pl.BlockSpec((tm, hv), lambda i: (i, 0))],
        out_shape=[jax.ShapeDtypeStruct((m, hq), bf16),
                   jax.ShapeDtypeStruct((m, hq), bf16),
                   jax.ShapeDtypeStruct((m, hv), bf16)],
        compiler_params=_cparams(("parallel",)),
        name="mla_up",
    )(proj, proj, proj, cc, ss, qn, kvn, wq, wk, wv)


def _attn_kernel(q_ref, k_ref, v_ref, kl_ref, vl_ref, o_ref, *, tq, nh):
    i = pl.program_id(2)
    qs = [q_ref[:, h * QK_SLAB:(h + 1) * QK_SLAB] for h in range(nh)]
    ksl = lambda h: slice(h * QK_SLAB, (h + 1) * QK_SLAB)
    vsl = lambda h: slice(h * V_HEAD, (h + 1) * V_HEAD)

    diag = pl.ds(pl.multiple_of(i * tq, tq), tq)
    col = lax.broadcasted_iota(jnp.int32, (tq, LEAD), 1)
    r_id = lax.broadcasted_iota(jnp.int32, (tq, tq), 0)
    c_id = lax.broadcasted_iota(jnp.int32, (tq, tq), 1)
    carry = []
    for h in range(nh):
        s_l = jnp.where(col >= LEAD_PAD, _dot_nt(qs[h], kl_ref[:, ksl(h)]), NEG_INF)
        s_d = jnp.where(c_id <= r_id, _dot_nt(qs[h], k_ref[diag, ksl(h)]), NEG_INF)
        m = jnp.maximum(jnp.max(s_l, axis=-1, keepdims=True), jnp.max(s_d, axis=-1, keepdims=True))
        p_l = jnp.exp2(s_l - m)
        p_d = jnp.exp2(s_d - m)
        l = jnp.sum(p_l, axis=-1, keepdims=True) + jnp.sum(p_d, axis=-1, keepdims=True)
        acc = _dot(p_l.astype(bf16), vl_ref[:, vsl(h)]) + _dot(p_d.astype(bf16), v_ref[diag, vsl(h)])
        carry += [m, l, acc]

    def update(s, vb, m, l, acc):
        m_new = jnp.maximum(m, jnp.max(s, axis=-1, keepdims=True))
        alpha = jnp.exp2(m - m_new)
        p = jnp.exp2(s - m_new)
        l = alpha * l + jnp.sum(p, axis=-1, keepdims=True)
        acc = alpha * acc + _dot(p.astype(bf16), vb)
        return [m_new, l, acc]

    def body(j, carry):
        rows = pl.ds(pl.multiple_of(j * tq, tq), tq)
        out = []
        for h in range(nh):
            out += update(_dot_nt(qs[h], k_ref[rows, ksl(h)]), v_ref[rows, vsl(h)], *carry[3 * h:3 * h + 3])
        return tuple(out)

    carry = lax.fori_loop(0, i, body, tuple(carry))
    for h in range(nh):
        o_ref[:, vsl(h)] = (carry[3 * h + 2] / carry[3 * h + 1]).astype(bf16)


def _attention(q, k, v, kl, vl, *, bsz, seq, tq, nh=2):
    assert seq % tq == 0 and MLA_HEADS % nh == 0
    nq = seq // tq
    kw, vw = nh * QK_SLAB, nh * V_HEAD
    return pl.pallas_call(
        functools.partial(_attn_kernel, tq=tq, nh=nh),
        grid=(bsz, MLA_HEADS // nh, nq),
        in_specs=[pl.BlockSpec((tq, kw), lambda b, h, i: (b * nq + i, h)),
                  pl.BlockSpec((seq, kw), lambda b, h, i: (b, h)),
                  pl.BlockSpec((seq, vw), lambda b, h, i: (b, h)),
                  pl.BlockSpec((LEAD, kw), lambda b, h, i: (0, h)),
                  pl.BlockSpec((LEAD, vw), lambda b, h, i: (0, h))],
        out_specs=pl.BlockSpec((tq, vw), lambda b, h, i: (b * nq + i, h)),
        out_shape=jax.ShapeDtypeStruct((bsz * seq, MLA_HEADS * V_HEAD), bf16),
        compiler_params=_cparams(("parallel", "parallel", "arbitrary")),
        name="mla_attn",
    )(q, k, v, kl, vl)


def _split3(v):
    b0 = v.astype(bf16).astype(f32)
    r1 = v - b0
    b1 = r1.astype(bf16).astype(f32)
    b2 = (r1 - b1).astype(bf16).astype(f32)
    return b0, b1, b2


def _ssd_kernel(zx_l, bc_l, dt_l, zx_r, bc_r, dt_r, cwx, cwbc, cbx, cbbc, dtb, alog, dskip, ng, e64, e128,
                o_ref, st_ref, px_ref, pbc_ref):
    c = pl.program_id(1)
    is_lead = c == 0

    @pl.when(is_lead)
    def _():
        st_ref[...] = jnp.zeros_like(st_ref)
        px_ref[...] = jnp.zeros_like(px_ref)
        pbc_ref[...] = jnp.zeros_like(pbc_ref)

    zx = jnp.where(is_lead, zx_l[...], zx_r[...])
    bcraw = jnp.where(is_lead, bc_l[...], bc_r[...])
    dtraw = jnp.where(is_lead, dt_l[...], dt_r[...])
    z = zx[:, :D_INNER]
    xraw = zx[:, D_INNER:]

    row1 = lax.broadcasted_iota(jnp.int32, (CHUNK, 1), 0)
    r_id = lax.broadcasted_iota(jnp.int32, (CHUNK, CHUNK), 0)
    c_id = lax.broadcasted_iota(jnp.int32, (CHUNK, CHUNK), 1)
    tril = r_id >= c_id

    def conv_silu(cur, prev_ref, w_ref, b_ref):
        prev = prev_ref[...]
        acc = cur * w_ref[CONV_W - 1:CONV_W, :] + b_ref[...]
        for k in range(1, CONV_W):
            comb = jnp.where(row1 >= CHUNK - k, prev, cur)
            acc = acc + pltpu.roll(comb, k, 0) * w_ref[CONV_W - 1 - k:CONV_W - k, :]
        prev_ref[...] = cur
        return acc * jax.nn.sigmoid(acc)

    xs = conv_silu(xraw, px_ref, cwx, cbx)
    bcc = conv_silu(bcraw, pbc_ref, cwbc, cbbc)
    b_all = bcc[:, :SSD_GROUPS * SSD_STATE]
    c_all = bcc[:, SSD_GROUPS * SSD_STATE:].astype(bf16)

    pre = dtraw + dtb[...]
    dt = jnp.maximum(pre, 0.0) + jnp.log1p(jnp.exp(-jnp.abs(pre)))
    dt = jnp.where(jnp.logical_and(is_lead, row1 < LEAD_PAD), 0.0, dt)
    da = dt * (-jnp.exp(alog[...]))
    tri = tril.astype(f32).astype(bf16)
    d0, d1, d2 = _split3(da)
    cs = _dot(tri, d0.astype(bf16)) + _dot(tri, d1.astype(bf16)) + _dot(tri, d2.astype(bf16))
    cs_t = cs.T
    ecs = jnp.exp(cs)
    dec = jnp.exp(cs[CHUNK - 1:CHUNK, :] - cs)

    def kcat(v):
        b0, b1, b2 = _split3(v)
        return jnp.where(c_id < 32, b0, jnp.where(c_id < 64, b1, jnp.where(c_id < 96, b2, 0.0))).astype(bf16)

    ex = _dot(jnp.concatenate([kcat(dt), kcat(ecs), kcat(dec)], axis=0), e64[...])
    dt_x = ex[0:CHUNK]
    ecs_x = ex[CHUNK:2 * CHUNK]
    dec_x = ex[2 * CHUNK:3 * CHUNK]
    cs_bc = _dot(kcat(cs), e128[...])

    xdt = xs * dt_x
    xdt_b = xdt.astype(bf16)
    xd_b = (xdt * dec_x).astype(bf16)

    ys = []
    for g in range(SSD_GROUPS):
        gl = slice(g * GROUP_W, (g + 1) * GROUP_W)
        b_g = b_all[:, g * SSD_STATE:(g + 1) * SSD_STATE]
        c_g = c_all[:, g * SSD_STATE:(g + 1) * SSD_STATE]
        cb = _dot_nt(c_g, b_g.astype(bf16))
        st_g = st_ref[:, gl]
        y_g = _dot(c_g, st_g.astype(bf16)) * ecs_x[:, gl]
        pieces = []
        for pr in range(GROUP_W // LANE):
            h0 = g * (GROUP_W // SSD_HEAD_DIM) + 2 * pr
            xpair = xdt_b[:, h0 * SSD_HEAD_DIM:(h0 + 2) * SSD_HEAD_DIM]
            yh = []
            for h in (h0, h0 + 1):
                seg = cs_bc[:, h * CHUNK:(h + 1) * CHUNK] - cs_t[h:h + 1, :]
                lm = jnp.where(tril, jnp.exp(jnp.minimum(seg, 0.0)), 0.0)
                yh.append(_dot((cb * lm).astype(bf16), xpair))
            pieces.append(jnp.where(c_id < SSD_HEAD_DIM, yh[0], yh[1]))
        y_g = y_g + jnp.concatenate(pieces, axis=1)
        st_ref[:, gl] = st_g * ecs_x[CHUNK - 1:CHUNK, gl] + _dot(b_g.T.astype(bf16), xd_b[:, gl])
        ys.append(y_g)

    y = jnp.concatenate(ys, axis=1) + xs * dskip[...]
    gated = y * (z * jax.nn.sigmoid(z))
    outs = []
    for g in range(SSD_GROUPS):
        gg = gated[:, g * GROUP_W:(g + 1) * GROUP_W]
        outs.append(gg * lax.rsqrt(jnp.mean(gg * gg, axis=-1, keepdims=True) + EPS))
    o_ref[...] = (jnp.concatenate(outs, axis=1) * ng[...]).astype(bf16)


def _ssd(proj_l, proj_r, cwx, cwbc, cbx, cbbc, dtb, alog, dskip, ng, e64, e128, *, bsz, seq):
    nch = seq // CHUNK
    real = lambda w: (lambda b, c: (b * nch + jnp.maximum(c - 1, 0), w))
    lead = lambda w: (lambda b, c: (0, w))
    const = lambda b, c: (0, 0)
    zx_w, bc_w, dt_w = 2 * D_INNER, 2 * SSD_GROUPS * SSD_STATE, LANE
    return pl.pallas_call(
        _ssd_kernel,
        grid=(bsz, nch + 1),
        in_specs=[pl.BlockSpec((CHUNK, zx_w), lead(P_Z // zx_w)),
                  pl.BlockSpec((CHUNK, bc_w), lead(P_B // bc_w)),
                  pl.BlockSpec((CHUNK, dt_w), lead(P_DT // dt_w)),
                  pl.BlockSpec((CHUNK, zx_w), real(P_Z // zx_w)),
                  pl.BlockSpec((CHUNK, bc_w), real(P_B // bc_w)),
                  pl.BlockSpec((CHUNK, dt_w), real(P_DT // dt_w)),
                  pl.BlockSpec((CONV_W, D_INNER), const),
                  pl.BlockSpec((CONV_W, bc_w), const),
                  pl.BlockSpec((1, D_INNER), const),
                  pl.BlockSpec((1, bc_w), const),
                  pl.BlockSpec((1, LANE), const),
                  pl.BlockSpec((1, LANE), const),
                  pl.BlockSpec((1, D_INNER), const),
                  pl.BlockSpec((1, D_INNER), const),
                  pl.BlockSpec((LANE, D_INNER), const),
                  pl.BlockSpec((LANE, SSD_HEADS * CHUNK), const)],
        out_specs=pl.BlockSpec((CHUNK, D_INNER), real(0)),
        out_shape=jax.ShapeDtypeStruct((bsz * seq, D_INNER), bf16),
        scratch_shapes=[pltpu.VMEM((SSD_STATE, D_INNER), f32),
                        pltpu.VMEM((CHUNK, D_INNER), f32),
                        pltpu.VMEM((CHUNK, bc_w), f32)],
        compiler_params=_cparams(("parallel", "arbitrary")),
        name="ssd",
    )(proj_l, proj_l, proj_l, proj_r, proj_r, proj_r, cwx, cwbc, cbx, cbbc, dtb, alog, dskip, ng, e64, e128)


def _outproj_kernel(a_ref, s_ref, wa_ref, ws_ref, x_ref, o_ref):
    o_ref[...] = _dot(a_ref[...], wa_ref[...]) + _dot(s_ref[...], ws_ref[...]) + x_ref[...]


def _outproj(a, s, w, x2d, *, tm, tn=512):
    m = a.shape[0]
    ka = a.shape[1]
    assert m % tm == 0 and D_MODEL % tn == 0
    return pl.pallas_call(
        _outproj_kernel,
        grid=(m // tm, D_MODEL // tn),
        in_specs=[pl.BlockSpec((tm, ka), lambda i, j: (i, 0)),
                  pl.BlockSpec((tm, D_INNER), lambda i, j: (i, 0)),
                  pl.BlockSpec((ka, tn), lambda i, j: (0, j)),
                  pl.BlockSpec((D_INNER, tn), lambda i, j: (1, j)),
                  pl.BlockSpec((tm, tn), lambda i, j: (i, j))],
        out_specs=pl.BlockSpec((tm, tn), lambda i, j: (i, j)),
        out_shape=jax.ShapeDtypeStruct((m, D_MODEL), f32),
        compiler_params=_cparams(("parallel", "arbitrary")),
        name="outproj",
    )(a, s, w, w, x2d)


SUBLANES = 8


def _oddeven_mergesort_pairs(n):
    pairs = []
    p = 1
    while p < n:
        k = p
        while k >= 1:
            for j in range(k % p, n - k, 2 * k):
                for i in range(min(k, n - j - k)):
                    if (i + j) // (2 * p) == (i + j + k) // (2 * p):
                        pairs.append((i + j, i + j + k))
            k //= 2
        p *= 2
    return pairs


def _sorted_top17(x):
    n_grp = x.shape[0] // SUBLANES
    assert n_grp == 16
    r = [x[g * SUBLANES:(g + 1) * SUBLANES, :] for g in range(n_grp)]

    def cmpx(a, b):
        r[a], r[b] = jnp.maximum(r[a], r[b]), jnp.minimum(r[a], r[b])

    for a, b in _oddeven_mergesort_pairs(n_grp):
        cmpx(a, b)
    for shift in (4, 2, 1):
        other = [pltpu.roll(v, shift, 0) for v in r]
        r = [jnp.maximum(r[i], other[n_grp - 1 - i]) for i in range(n_grp)]
        k = n_grp // 2
        while k >= 1:
            for i in range(n_grp):
                if i & k == 0:
                    cmpx(i, i + k)
            k //= 2
    below = [jnp.where(x[g * SUBLANES:(g + 1) * SUBLANES, :] < r[n_grp - 1], x[g * SUBLANES:(g + 1) * SUBLANES, :],
                       -jnp.inf) for g in range(n_grp)]
    while len(below) > 1:
        below = [jnp.maximum(below[i], below[i + 1]) for i in range(0, len(below), 2)]
    return [v[0:1, :] for v in r] + [jnp.max(below[0], axis=0, keepdims=True)]


def _stack_rows(vals):
    n = len(vals)
    rid = lax.broadcasted_iota(jnp.int32, (n, vals[0].shape[1]), 0)
    out = jnp.zeros((n, vals[0].shape[1]), f32)
    for r, v in enumerate(vals):
        out = jnp.where(rid == r, v, out)
    return out


N_CAND = PEER_TOPK + (PEER_TOPK // 2 - 1) * (PEER_TOPK // 2) + PEER_TOPK


def _peer_q_kernel(h_ref, g_ref, wq_ref, sk_ref, xnt_ref, e1_ref, thr_ref, e2_ref, cand_ref):
    xn = _rms(h_ref[...], g_ref[...])
    xnt_ref[...] = xn.T.astype(bf16)
    q = _dot(xn.astype(bf16), wq_ref[...]).astype(bf16)
    sk1 = sk_ref[0].astype(bf16)
    sk2 = sk_ref[1].astype(bf16)
    k = PEER_TOPK
    half = k // 2
    for h in range(PEER_HEADS):
        lo = h * 2 * HALF_KEY
        s1 = _dot_nt(sk1, q[:, lo:lo + HALF_KEY])
        s2 = _dot_nt(sk2, q[:, lo + HALF_KEY:lo + 2 * HALF_KEY])
        v1 = _sorted_top17(s1)
        v2 = _sorted_top17(s2)
        v2_top = _stack_rows(v2[:k])
        cand_ref[0:k, :] = v1[0] + v2_top
        for a in range(1, half):
            cand_ref[k + (a - 1) * half:k + a * half, :] = v1[a] + v2_top[0:half, :]
        cand_ref[k + (half - 1) * half:k + half * half, :] = _stack_rows(v1[half:k]) + v2[0]
        ninf = jnp.full_like(v1[0], -jnp.inf)
        cand_ref[k + half * half:N_CAND, :] = _stack_rows([v1[0] + v2[k], v1[k] + v2[0]] + [ninf] * (half - 2))
        cand_ref[N_CAND:, :] = jnp.full((N_KEYS - N_CAND, s1.shape[1]), -jnp.inf, f32)
        top = _sorted_top17(cand_ref[...])
        zsum = jnp.zeros_like(top[0])
        for t in top[:k]:
            zsum = zsum + jnp.exp(t - top[0])
        tau = 0.5 * (top[k - 1] + top[k])
        scale = 0.5 / zsum
        e1_ref[h] = jnp.exp(s1 - v1[0])
        thr_ref[h] = jnp.exp((tau - v2[0]) - s1) * scale
        e2_ref[h] = pltpu.bitcast((jnp.exp(s2 - v2[0]) * scale).astype(bf16), jnp.uint32)


def _peer_q(h1, g, wq, sub_keys, *, tm):
    n = h1.shape[0]
    assert n % tm == 0
    nq = PEER_HEADS * 2 * HALF_KEY
    per_head = lambda rows, dt: (pl.BlockSpec((PEER_HEADS, rows, tm), lambda i: (0, 0, i)),
                                 jax.ShapeDtypeStruct((PEER_HEADS, rows, n), dt))
    specs, shapes = zip((pl.BlockSpec((D_MODEL, tm), lambda i: (0, i)), jax.ShapeDtypeStruct((D_MODEL, n), bf16)),
                        per_head(N_KEYS, f32), per_head(N_KEYS, f32), per_head(N_KEYS // 2, jnp.uint32))
    return pl.pallas_call(
        _peer_q_kernel,
        grid=(n // tm,),
        in_specs=[pl.BlockSpec((tm, D_MODEL), lambda i: (i, 0)),
                  pl.BlockSpec((1, D_MODEL), lambda i: (0, 0)),
                  pl.BlockSpec((D_MODEL, nq), lambda i: (0, 0), pipeline_mode=pl.Buffered(1)),
                  pl.BlockSpec((2, N_KEYS, HALF_KEY), lambda i: (0, 0, 0))],
        out_specs=list(specs),
        out_shape=list(shapes),
        scratch_shapes=[pltpu.VMEM((N_KEYS, tm), f32)],
        compiler_params=_cparams(("parallel",)),
        name="peer_query",
    )(h1, g, wq, sub_keys)


def _peer_dense_kernel(xnt_ref, u_ref, v_ref, e1_ref, thr_ref, e2_ref, h_ref, g_ref, o_ref, ht_ref, act_ref, *, te):
    j = pl.program_id(1)
    tm = xnt_ref.shape[1]
    pk = N_KEYS // 2

    @pl.when(j == 0)
    def _():
        o_ref[...] = jnp.zeros_like(o_ref)

    ht_ref[...] = _dot(u_ref[...], xnt_ref[...])
    for ii in range(te // N_KEYS):
        i1 = j * (te // N_KEYS) + ii
        e1rows = [e1_ref[h, pl.ds(i1, 1), :] for h in range(PEER_HEADS)]
        thrrows = [thr_ref[h, pl.ds(i1, 1), :] for h in range(PEER_HEADS)]
        for c in range(tm // LANE):
            cl = slice(c * LANE, (c + 1) * LANE)
            w = jnp.zeros((N_KEYS, LANE), bf16)
            for h in range(PEER_HEADS):
                e1b = jnp.broadcast_to(e1rows[h][:, cl], (N_KEYS, LANE)).astype(bf16)
                thrb = jnp.broadcast_to(thrrows[h][:, cl], (N_KEYS, LANE)).astype(bf16)
                e2 = pltpu.bitcast(e2_ref[h, :, cl], bf16)
                w = w + jnp.where(e2 >= thrb, e2, jnp.zeros_like(e2)) * e1b
            hh = ht_ref[ii * N_KEYS:(ii + 1) * N_KEYS, cl]
            g = hh * (1.0 + lax.erf(hh * np.float32(1.0 / np.sqrt(2.0))))
            act_ref[ii * pk:(ii + 1) * pk, cl] = pltpu.bitcast(g.astype(bf16) * w, jnp.uint32)
    o_ref[...] += _dot(pltpu.bitcast(act_ref[...], bf16).T, v_ref[...])

    hr = h_ref.shape[0]
    slab = pl.ds(pl.multiple_of(j * hr, hr), hr)
    o_ref[slab, :] += h_ref[...]

    @pl.when(j == pl.num_programs(1) - 1)
    def _():
        sub = 64

        def body(r, c):
            rows = pl.ds(pl.multiple_of(r * sub, sub), sub)
            o_ref[rows, :] = _rms(o_ref[rows, :], g_ref[...])
            return c
        lax.fori_loop(0, tm // sub, body, 0)


def _peer_dense(xnt, u_b, v_b, e1, thr, e2b, h1, g, *, tm, te):
    n = xnt.shape[1]
    n_exp = u_b.shape[0]
    nj = n_exp // te
    assert n % tm == 0 and n_exp % te == 0 and te % N_KEYS == 0 and tm % LANE == 0 and tm % (8 * nj) == 0
    hr = tm // nj
    return pl.pallas_call(
        functools.partial(_peer_dense_kernel, te=te),
        grid=(n // tm, nj),
        in_specs=[pl.BlockSpec((D_MODEL, tm), lambda i, j: (0, i)),
                  pl.BlockSpec((te, D_MODEL), lambda i, j: (j, 0)),
                  pl.BlockSpec((te, D_MODEL), lambda i, j: (j, 0)),
                  pl.BlockSpec((PEER_HEADS, N_KEYS, tm), lambda i, j: (0, 0, i)),
                  pl.BlockSpec((PEER_HEADS, N_KEYS, tm), lambda i, j: (0, 0, i)),
                  pl.BlockSpec((PEER_HEADS, N_KEYS // 2, tm), lambda i, j: (0, 0, i)),
                  pl.BlockSpec((hr, D_MODEL), lambda i, j: (i * nj + j, 0)),
                  pl.BlockSpec((1, D_MODEL), lambda i, j: (0, 0))],
        out_specs=pl.BlockSpec((tm, D_MODEL), lambda i, j: (i, 0)),
        out_shape=jax.ShapeDtypeStruct((n, D_MODEL), f32),
        scratch_shapes=[pltpu.VMEM((te, tm), f32),
                        pltpu.VMEM((te // 2, tm), jnp.uint32)],
        compiler_params=_cparams(("parallel", "arbitrary")),
        name="peer_dense",
    )(xnt, u_b, v_b, e1, thr, e2b, h1, g)


_W_IN_SRC = ((1600, 3648), (3648, 5696), (0, 1024), (5696, 6208), (6208, 6720), (1024, 1536),
             (1536, 1600), (1568, 1600), (1536, 1568),
             (6720, 6752), (6720, 6752), (6720, 6752))


def _prep_w_in_kernel(w_ref, o_ref):
    col = 0
    tail = []
    for lo, hi in _W_IN_SRC:
        piece = w_ref[:, lo:hi]
        if (hi - lo) % LANE == 0:
            o_ref[:, col:col + hi - lo] = piece.astype(bf16)
            col += hi - lo
        else:
            tail.append(piece)
    rows = w_ref.shape[0]
    tail.append(jnp.zeros((rows, NP - col - sum(t.shape[1] for t in tail)), f32))
    o_ref[:, col:] = jnp.concatenate(tail, axis=1).astype(bf16)


def _prep_w_in(w, *, tr=256):
    _, k, n_in = w.shape
    assert k % tr == 0
    return pl.pallas_call(
        _prep_w_in_kernel,
        grid=(k // tr,),
        in_specs=[pl.BlockSpec((None, tr, n_in), lambda i: (0, i, 0))],
        out_specs=pl.BlockSpec((tr, NP), lambda i: (i, 0)),
        out_shape=jax.ShapeDtypeStruct((k, NP), bf16),
        compiler_params=_cparams(("parallel",)),
        name="prep_w_in",
    )(w)


def _prep_w_uq(w):
    w = w.reshape(Q_LORA, MLA_HEADS, QK_NOPE + QK_ROPE)
    nope, x1, x2 = w[..., :QK_NOPE], w[..., QK_NOPE:QK_NOPE + 32], w[..., QK_NOPE + 32:]
    return jnp.concatenate([nope, x1, x2, x2, x1], axis=-1).reshape(Q_LORA, MLA_HEADS * QK_SLAB).astype(bf16)


def _rope_tables(pos):
    inv_freq = ROPE_THETA ** (-jnp.arange(0, QK_ROPE, 2, dtype=f32) / QK_ROPE)
    ang = pos[:, None] * inv_freq[None, :]
    c, s = jnp.cos(ang), jnp.sin(ang)
    zero = jnp.zeros((pos.shape[0], LANE - QK_ROPE), f32)
    return jnp.concatenate([c, c, zero], axis=1), jnp.concatenate([-s, s, zero], axis=1)


def _expand_matrix(width):
    r = np.arange(LANE)[:, None]
    col = np.arange(SSD_HEADS * width)[None, :]
    return jnp.asarray(((r % SSD_HEADS) == (col // width)) & (r < 3 * SSD_HEADS), dtype=bf16)


def _tile(m, pref):
    t = min(pref, m)
    assert m % t == 0
    return t


def kernel(x, meta_tokens, attn_norm, w_in, q_norm, w_uq, kv_norm, w_ukv, conv_w, conv_b, dt_bias, a_log, d_skip,
           ssd_norm, w_out, ffn_norm, w_query, sub_keys, u_experts, v_experts, final_norm):
    bsz, seq, d = x.shape
    assert d == D_MODEL and seq % CHUNK == 0
    n_tok = bsz * seq
    x2d = x.reshape(n_tok, d)

    w_in_b = _prep_w_in(w_in)
    wq_b = _prep_w_uq(w_uq[0])
    wkv = w_ukv[0].reshape(KV_LORA, MLA_HEADS, QK_NOPE + V_HEAD)
    wk_b = wkv[..., :QK_NOPE].reshape(KV_LORA, MLA_HEADS * QK_NOPE).astype(bf16)
    wv_b = wkv[..., QK_NOPE:].reshape(KV_LORA, MLA_HEADS * V_HEAD).astype(bf16)
    w_out_b = w_out[0].astype(bf16)
    w_query_b = w_query[0].astype(bf16)
    u_b = u_experts[0].astype(bf16)
    v_b = v_experts[0].astype(bf16)
    cwx, cwbc = conv_w[0][:, :D_INNER], conv_w[0][:, D_INNER:]
    cbx, cbbc = conv_b[0][None, :D_INNER], conv_b[0][None, D_INNER:]
    rep3 = lambda v, fill: jnp.concatenate([v, v, v, jnp.full((LANE - 3 * SSD_HEADS,), fill, f32)])[None, :]
    dtb = rep3(dt_bias[0], 0.0)
    alog = rep3(a_log[0], 0.0)
    dskip = jnp.repeat(d_skip[0], SSD_HEAD_DIM)[None, :]
    e64 = _expand_matrix(SSD_HEAD_DIM)
    e128 = _expand_matrix(CHUNK)

    lead_rows = jnp.concatenate([jnp.zeros((LEAD_PAD, d), x.dtype), meta_tokens.astype(x.dtype)], axis=0)
    cc_l, ss_l = _rope_tables(jnp.arange(LEAD, dtype=f32) - LEAD_PAD)
    cc_r, ss_r = _rope_tables(jnp.arange(seq, dtype=f32) + N_META)

    g_attn = attn_norm[0][None, :]
    proj_l = _inproj(lead_rows, g_attn, w_in_b, tm=LEAD)
    proj_r = _inproj(x2d, g_attn, w_in_b, tm=_tile(n_tok, 512))

    qn, kvn = q_norm[0][None, :], kv_norm[0][None, :]
    _, k_l, v_l = _mla_up(proj_l, cc_l, ss_l, qn, kvn, wq_b, wk_b, wv_b, tm=LEAD, n_pos_blocks=1)
    tmu = _tile(seq, 256)
    q_r, k_r, v_r = _mla_up(proj_r, cc_r, ss_r, qn, kvn, wq_b, wk_b, wv_b, tm=tmu, n_pos_blocks=seq // tmu)
    a_out = _attention(q_r, k_r, v_r, k_l, v_l, bsz=bsz, seq=seq, tq=_tile(seq, 1024))

    s_out = _ssd(proj_l, proj_r, cwx, cwbc, cbx, cbbc, dtb, alog, dskip, ssd_norm[0][None, :], e64, e128,
                 bsz=bsz, seq=seq)

    h1 = _outproj(a_out, s_out, w_out_b, x2d, tm=_tile(n_tok, 1024))

    xnt, e1, thr, e2b = _peer_q(h1, ffn_norm[0][None, :], w_query_b, sub_keys[0], tm=_tile(n_tok, 256))
    out = _peer_dense(xnt, u_b, v_b, e1, thr, e2b, h1, final_norm[None, :], tm=_tile(n_tok, 512), te=512)
    return out.reshape(bsz, seq, d)
```

```python
import functools

import jax
import jax.numpy as jnp
import numpy as np
from jax import lax
from jax.experimental import pallas as pl
from jax.experimental.pallas import tpu as pltpu

f32 = jnp.float32
bf16 = jnp.bfloat16

D_MODEL = 4096
N_META = 16
CHUNK = 128
LEAD = CHUNK
LEAD_PAD = CHUNK - N_META
MLA_HEADS = 16
QK_NOPE = 128
QK_ROPE = 64
V_HEAD = 128
Q_LORA = 1024
KV_LORA = 512
ROPE_THETA = 10000.0
D_INNER = 2048
SSD_HEAD_DIM = 64
SSD_HEADS = 32
SSD_GROUPS = 4
SSD_STATE = 128
CONV_W = 4
PEER_HEADS = 8
PEER_TOPK = 16
N_KEYS = 128
HALF_KEY = 128
EPS = 1e-6
NEG_INF = -1e30

LANE = 128
QK_SLAB = 2 * LANE
GROUP_W = D_INNER // SSD_GROUPS

P_Z = 0
P_XS = 2048
P_CQ = 4096
P_B = 5120
P_C = 5632
P_CKV = 6144
P_KR = 6656
P_DT = 6784
NP = 6912

VMEM_LIMIT = 56 * 1024 * 1024


def _cparams(sem, vmem=VMEM_LIMIT, flags=None):
    return pltpu.CompilerParams(dimension_semantics=sem, vmem_limit_bytes=vmem, flags=flags)


def _dot(a, b):
    return jnp.dot(a, b, preferred_element_type=f32)


def _dot_nt(a, b):
    return lax.dot_general(a, b, (((1,), (1,)), ((), ())), preferred_element_type=f32)


def _rms(x, g):
    return x * lax.rsqrt(jnp.mean(x * x, axis=-1, keepdims=True) + EPS) * g


def _cast_specs(to_cast, n_steps, step_index):
    ins, outs, shapes = [], [], []
    for w in to_cast:
        rows, cols = w.shape
        assert rows % (16 * n_steps) == 0
        spec = pl.BlockSpec((rows // n_steps, cols), step_index)
        ins.append(spec)
        outs.append(spec)
        shapes.append(jax.ShapeDtypeStruct((rows, cols), bf16))
    return ins, outs, shapes


def _cast_slabs(src_refs, dst_refs):
    for src, dst in zip(src_refs, dst_refs):
        sub = min(32, src.shape[0])

        def cast_rows(r, c, src=src, dst=dst, sub=sub):
            rows = pl.ds(pl.multiple_of(r * sub, sub), sub)
            dst[rows, :] = src[rows, :].astype(bf16)
            return c
        lax.fori_loop(0, src.shape[0] // sub, cast_rows, 0)


def _inproj_kernel(x_ref, g_ref, w_ref, o_ref, xn_ref, *, sub):
    @pl.when(pl.program_id(1) == 0)
    def _():
        def body(r, c):
            rows = pl.ds(pl.multiple_of(r * sub, sub), sub)
            xn_ref[rows, :] = _rms(x_ref[rows, :], g_ref[...]).astype(bf16)
            return c
        lax.fori_loop(0, x_ref.shape[0] // sub, body, 0)

    o_ref[...] = _dot(xn_ref[...], w_ref[...])


def _inproj(x2d, g, w, *, tm, tn=768):
    m = x2d.shape[0]
    assert m % tm == 0 and NP % tn == 0
    sub = min(64, tm)
    return pl.pallas_call(
        functools.partial(_inproj_kernel, sub=sub),
        grid=(m // tm, NP // tn),
        in_specs=[pl.BlockSpec((tm, D_MODEL), lambda i, j: (i, 0)),
                  pl.BlockSpec((1, D_MODEL), lambda i, j: (0, 0)),
                  pl.BlockSpec((D_MODEL, tn), lambda i, j: (0, j))],
        out_specs=pl.BlockSpec((tm, tn), lambda i, j: (i, j)),
        out_shape=jax.ShapeDtypeStruct((m, NP), f32),
        scratch_shapes=[pltpu.VMEM((tm, D_MODEL), bf16)],
        compiler_params=_cparams(("parallel", "arbitrary")),
        name="inproj",
    )(x2d, g, w)


def _mla_up_kernel(cq_ref, ckv_ref, kr_ref, cc_ref, ss_ref, qn_ref, kvn_ref, wq_ref, wk_ref, wv_ref,
                   q_ref, k_ref, v_ref, *, scale):
    cc = cc_ref[...]
    ss = ss_ref[...]

    def rope(t):
        return t * cc + pltpu.roll(t, 64, 1) * ss

    cqn = _rms(cq_ref[...], qn_ref[...]).astype(bf16)
    q = _dot(cqn, wq_ref[...])
    for h in range(MLA_HEADS):
        lo = h * QK_SLAB
        q_ref[:, lo:lo + LANE] = (q[:, lo:lo + LANE] * scale).astype(bf16)
        q_ref[:, lo + LANE:lo + QK_SLAB] = (rope(q[:, lo + LANE:lo + QK_SLAB]) * scale).astype(bf16)

    kvn = _rms(ckv_ref[...], kvn_ref[...]).astype(bf16)
    kn = _dot(kvn, wk_ref[...])
    kpe = rope(kr_ref[...]).astype(bf16)
    for h in range(MLA_HEADS):
        lo = h * QK_SLAB
        k_ref[:, lo:lo + LANE] = kn[:, h * LANE:(h + 1) * LANE].astype(bf16)
        k_ref[:, lo + LANE:lo + QK_SLAB] = kpe
    v_ref[...] = _dot(kvn, wv_ref[...]).astype(bf16)


def _mla_up(proj, cc, ss, qn, kvn, wq, wk, wv, *, tm, n_pos_blocks):
    m = proj.shape[0]
    assert m % tm == 0
    scale = float((QK_NOPE + QK_ROPE) ** -0.5 * np.log2(np.e))
    hq = MLA_HEADS * QK_SLAB
    hv = MLA_HEADS * V_HEAD
    const = lambda i: (0, 0)
    return pl.pallas_call(
        functools.partial(_mla_up_kernel, scale=scale),
        grid=(m // tm,),
        in_specs=[pl.BlockSpec((tm, Q_LORA), lambda i: (i, P_CQ // Q_LORA)),
                  pl.BlockSpec((tm, KV_LORA), lambda i: (i, P_CKV // KV_LORA)),
                  pl.BlockSpec((tm, LANE), lambda i: (i, P_KR // LANE)),
                  pl.BlockSpec((tm, LANE), lambda i: (i % n_pos_blocks, 0)),
                  pl.BlockSpec((tm, LANE), lambda i: (i % n_pos_blocks, 0)),
                  pl.BlockSpec((1, Q_LORA), const),
                  pl.BlockSpec((1, KV_LORA), const),
                  pl.BlockSpec((Q_LORA, hq), const),
                  pl.BlockSpec((KV_LORA, hv), const),
                  pl.BlockSpec((KV_LORA, hv), const)],
        out_specs=[pl.BlockSpec((tm, hq), lambda i: (i, 0)),
                   pl.BlockSpec((tm, hq), lambda i: (i, 0)),
                   pl.BlockSpec((tm, hv), lambda i: (i, 0))],
        out_shape=[jax.ShapeDtypeStruct((m, hq), bf16),
                   jax.ShapeDtypeStruct((m, hq), bf16),
                   jax.ShapeDtypeStruct((m, hv), bf16)],
        compiler_params=_cparams(("parallel",)),
        name="mla_up",
    )(proj, proj, proj, cc, ss, qn, kvn, wq, wk, wv)


def _attn_kernel(q_ref, k_ref, v_ref, kl_ref, vl_ref, *rest, tq, nh, n_cast):
    src_refs, o_ref, dst_refs = rest[:n_cast], rest[n_cast], rest[n_cast + 1:]
    _cast_slabs(src_refs, dst_refs)

    i = pl.program_id(2)
    qs = [q_ref[:, h * QK_SLAB:(h + 1) * QK_SLAB] for h in range(nh)]
    ksl = lambda h: slice(h * QK_SLAB, (h + 1) * QK_SLAB)
    vsl = lambda h: slice(h * V_HEAD, (h + 1) * V_HEAD)

    diag = pl.ds(pl.multiple_of(i * tq, tq), tq)
    col = lax.broadcasted_iota(jnp.int32, (tq, LEAD), 1)
    r_id = lax.broadcasted_iota(jnp.int32, (tq, tq), 0)
    c_id = lax.broadcasted_iota(jnp.int32, (tq, tq), 1)
    carry = []
    for h in range(nh):
        s_l = jnp.where(col >= LEAD_PAD, _dot_nt(qs[h], kl_ref[:, ksl(h)]), NEG_INF)
        s_d = jnp.where(c_id <= r_id, _dot_nt(qs[h], k_ref[diag, ksl(h)]), NEG_INF)
        m = jnp.maximum(jnp.max(s_l, axis=-1, keepdims=True), jnp.max(s_d, axis=-1, keepdims=True))
        p_l = jnp.exp2(s_l - m)
        p_d = jnp.exp2(s_d - m)
        l = jnp.sum(p_l, axis=-1, keepdims=True) + jnp.sum(p_d, axis=-1, keepdims=True)
        acc = _dot(p_l.astype(bf16), vl_ref[:, vsl(h)]) + _dot(p_d.astype(bf16), v_ref[diag, vsl(h)])
        carry += [m, l, acc]

    def update(s, vb, m, l, acc):
        m_new = jnp.maximum(m, jnp.max(s, axis=-1, keepdims=True))
        alpha = jnp.exp2(m - m_new)
        p = jnp.exp2(s - m_new)
        l = alpha * l + jnp.sum(p, axis=-1, keepdims=True)
        acc = alpha * acc + _dot(p.astype(bf16), vb)
        return [m_new, l, acc]

    def body(j, carry):
        rows = pl.ds(pl.multiple_of(j * tq, tq), tq)
        out = []
        for h in range(nh):
            out += update(_dot_nt(qs[h], k_ref[rows, ksl(h)]), v_ref[rows, vsl(h)], *carry[3 * h:3 * h + 3])
        return tuple(out)

    carry = lax.fori_loop(0, i, body, tuple(carry))
    for h in range(nh):
        o_ref[:, vsl(h)] = (carry[3 * h + 2] / carry[3 * h + 1]).astype(bf16)


def _attention(q, k, v, kl, vl, to_cast, *, bsz, seq, tq, nh=2):
    assert seq % tq == 0 and MLA_HEADS % nh == 0
    nq = seq // tq
    nhg = MLA_HEADS // nh
    n_steps = bsz * nhg * nq
    kw, vw = nh * QK_SLAB, nh * V_HEAD
    cast_in, cast_out, cast_shape = _cast_specs(to_cast, n_steps, lambda b, h, i: ((b * nhg + h) * nq + i, 0))
    outs = pl.pallas_call(
        functools.partial(_attn_kernel, tq=tq, nh=nh, n_cast=len(to_cast)),
        grid=(bsz, nhg, nq),
        in_specs=[pl.BlockSpec((tq, kw), lambda b, h, i: (b * nq + i, h)),
                  pl.BlockSpec((seq, kw), lambda b, h, i: (b, h)),
                  pl.BlockSpec((seq, vw), lambda b, h, i: (b, h)),
                  pl.BlockSpec((LEAD, kw), lambda b, h, i: (0, h)),
                  pl.BlockSpec((LEAD, vw), lambda b, h, i: (0, h))] + cast_in,
        out_specs=[pl.BlockSpec((tq, vw), lambda b, h, i: (b * nq + i, h))] + cast_out,
        out_shape=[jax.ShapeDtypeStruct((bsz * seq, MLA_HEADS * V_HEAD), bf16)] + cast_shape,
        compiler_params=_cparams(("parallel", "parallel", "arbitrary")),
        name="mla_attn",
    )(q, k, v, kl, vl, *to_cast)
    return outs[0], outs[1:]


def _split3(v):
    b0 = v.astype(bf16).astype(f32)
    r1 = v - b0
    b1 = r1.astype(bf16).astype(f32)
    b2 = (r1 - b1).astype(bf16).astype(f32)
    return b0, b1, b2


def _ssd_kernel(zx_l, bc_l, dt_l, zx_r, bc_r, dt_r, cwx, cwbc, cbx, cbbc, dtb, alog, dskip, ng, e64, e128,
                o_ref, st_ref, px_ref, pbc_ref):
    c = pl.program_id(1)
    is_lead = c == 0

    @pl.when(is_lead)
    def _():
        st_ref[...] = jnp.zeros_like(st_ref)
        px_ref[...] = jnp.zeros_like(px_ref)
        pbc_ref[...] = jnp.zeros_like(pbc_ref)

    zx = jnp.where(is_lead, zx_l[...], zx_r[...])
    bcraw = jnp.where(is_lead, bc_l[...], bc_r[...])
    dtraw = jnp.where(is_lead, dt_l[...], dt_r[...])
    z = zx[:, :D_INNER]
    xraw = zx[:, D_INNER:]

    row1 = lax.broadcasted_iota(jnp.int32, (CHUNK, 1), 0)
    r_id = lax.broadcasted_iota(jnp.int32, (CHUNK, CHUNK), 0)
    c_id = lax.broadcasted_iota(jnp.int32, (CHUNK, CHUNK), 1)
    tril = r_id >= c_id

    def conv_silu(cur, prev_ref, w_ref, b_ref):
        prev = prev_ref[...]
        acc = cur * w_ref[CONV_W - 1:CONV_W, :] + b_ref[...]
        for k in range(1, CONV_W):
            comb = jnp.where(row1 >= CHUNK - k, prev, cur)
            acc = acc + pltpu.roll(comb, k, 0) * w_ref[CONV_W - 1 - k:CONV_W - k, :]
        prev_ref[...] = cur
        return acc * jax.nn.sigmoid(acc)

    xs = conv_silu(xraw, px_ref, cwx, cbx)
    bcc = conv_silu(bcraw, pbc_ref, cwbc, cbbc)
    b_all = bcc[:, :SSD_GROUPS * SSD_STATE]
    c_all = bcc[:, SSD_GROUPS * SSD_STATE:].astype(bf16)

    pre = dtraw + dtb[...]
    dt = jnp.maximum(pre, 0.0) + jnp.log1p(jnp.exp(-jnp.abs(pre)))
    dt = jnp.where(jnp.logical_and(is_lead, row1 < LEAD_PAD), 0.0, dt)
    da = dt * (-jnp.exp(alog[...]))
    tri = tril.astype(f32).astype(bf16)
    d0, d1, d2 = _split3(da)
    cs = _dot(tri, d0.astype(bf16)) + _dot(tri, d1.astype(bf16)) + _dot(tri, d2.astype(bf16))
    cs_t = cs.T
    ecs = jnp.exp(cs)
    dec = jnp.exp(cs[CHUNK - 1:CHUNK, :] - cs)

    def kcat(v):
        b0, b1, b2 = _split3(v)
        return jnp.where(c_id < 32, b0, jnp.where(c_id < 64, b1, jnp.where(c_id < 96, b2, 0.0))).astype(bf16)

    ex = _dot(jnp.concatenate([kcat(dt), kcat(ecs), kcat(dec)], axis=0), e64[...])
    dt_x = ex[0:CHUNK]
    ecs_x = ex[CHUNK:2 * CHUNK]
    dec_x = ex[2 * CHUNK:3 * CHUNK]
    cs_bc = _dot(kcat(cs), e128[...])

    xdt = xs * dt_x
    xdt_b = xdt.astype(bf16)
    xd_b = (xdt * dec_x).astype(bf16)

    ys = []
    for g in range(SSD_GROUPS):
        gl = slice(g * GROUP_W, (g + 1) * GROUP_W)
        b_g = b_all[:, g * SSD_STATE:(g + 1) * SSD_STATE]
        c_g = c_all[:, g * SSD_STATE:(g + 1) * SSD_STATE]
        cb = _dot_nt(c_g, b_g.astype(bf16))
        st_g = st_ref[:, gl]
        y_g = _dot(c_g, st_g.astype(bf16)) * ecs_x[:, gl]
        pieces = []
        for pr in range(GROUP_W // LANE):
            h0 = g * (GROUP_W // SSD_HEAD_DIM) + 2 * pr
            xpair = xdt_b[:, h0 * SSD_HEAD_DIM:(h0 + 2) * SSD_HEAD_DIM]
            yh = []
            for h in (h0, h0 + 1):
                seg = cs_bc[:, h * CHUNK:(h + 1) * CHUNK] - cs_t[h:h + 1, :]
                lm = jnp.where(tril, jnp.exp(jnp.minimum(seg, 0.0)), 0.0)
                yh.append(_dot((cb * lm).astype(bf16), xpair))
            pieces.append(jnp.where(c_id < SSD_HEAD_DIM, yh[0], yh[1]))
        y_g = y_g + jnp.concatenate(pieces, axis=1)
        st_ref[:, gl] = st_g * ecs_x[CHUNK - 1:CHUNK, gl] + _dot(b_g.T.astype(bf16), xd_b[:, gl])
        ys.append(y_g)

    y = jnp.concatenate(ys, axis=1) + xs * dskip[...]
    gated = y * (z * jax.nn.sigmoid(z))
    outs = []
    for g in range(SSD_GROUPS):
        gg = gated[:, g * GROUP_W:(g + 1) * GROUP_W]
        outs.append(gg * lax.rsqrt(jnp.mean(gg * gg, axis=-1, keepdims=True) + EPS))
    o_ref[...] = (jnp.concatenate(outs, axis=1) * ng[...]).astype(bf16)


def _ssd(proj_l, proj_r, cwx, cwbc, cbx, cbbc, dtb, alog, dskip, ng, e64, e128, *, bsz, seq):
    nch = seq // CHUNK
    real = lambda w: (lambda b, c: (b * nch + jnp.maximum(c - 1, 0), w))
    lead = lambda w: (lambda b, c: (0, w))
    const = lambda b, c: (0, 0)
    zx_w, bc_w, dt_w = 2 * D_INNER, 2 * SSD_GROUPS * SSD_STATE, LANE
    return pl.pallas_call(
        _ssd_kernel,
        grid=(bsz, nch + 1),
        in_specs=[pl.BlockSpec((CHUNK, zx_w), lead(P_Z // zx_w)),
                  pl.BlockSpec((CHUNK, bc_w), lead(P_B // bc_w)),
                  pl.BlockSpec((CHUNK, dt_w), lead(P_DT // dt_w)),
                  pl.BlockSpec((CHUNK, zx_w), real(P_Z // zx_w)),
                  pl.BlockSpec((CHUNK, bc_w), real(P_B // bc_w)),
                  pl.BlockSpec((CHUNK, dt_w), real(P_DT // dt_w)),
                  pl.BlockSpec((CONV_W, D_INNER), const),
                  pl.BlockSpec((CONV_W, bc_w), const),
                  pl.BlockSpec((1, D_INNER), const),
                  pl.BlockSpec((1, bc_w), const),
                  pl.BlockSpec((1, LANE), const),
                  pl.BlockSpec((1, LANE), const),
                  pl.BlockSpec((1, D_INNER), const),
                  pl.BlockSpec((1, D_INNER), const),
                  pl.BlockSpec((LANE, D_INNER), const),
                  pl.BlockSpec((LANE, SSD_HEADS * CHUNK), const)],
        out_specs=pl.BlockSpec((CHUNK, D_INNER), real(0)),
        out_shape=jax.ShapeDtypeStruct((bsz * seq, D_INNER), bf16),
        scratch_shapes=[pltpu.VMEM((SSD_STATE, D_INNER), f32),
                        pltpu.VMEM((CHUNK, D_INNER), f32),
                        pltpu.VMEM((CHUNK, bc_w), f32)],
        compiler_params=_cparams(("parallel", "arbitrary")),
        name="ssd",
    )(proj_l, proj_l, proj_l, proj_r, proj_r, proj_r, cwx, cwbc, cbx, cbbc, dtb, alog, dskip, ng, e64, e128)


def _outproj_kernel(a_ref, s_ref, wa_ref, ws_ref, x_ref, *rest, n_cast):
    src_refs, o_ref, dst_refs = rest[:n_cast], rest[n_cast], rest[n_cast + 1:]
    _cast_slabs(src_refs, dst_refs)
    o_ref[...] = _dot(a_ref[...], wa_ref[...]) + _dot(s_ref[...], ws_ref[...]) + x_ref[...]


def _outproj(a, s, w, x2d, to_cast, *, tm, tn=512):
    m = a.shape[0]
    ka = a.shape[1]
    assert m % tm == 0 and D_MODEL % tn == 0
    nj = D_MODEL // tn
    cast_in, cast_out, cast_shape = _cast_specs(to_cast, (m // tm) * nj, lambda i, j: (i * nj + j, 0))
    outs = pl.pallas_call(
        functools.partial(_outproj_kernel, n_cast=len(to_cast)),
        grid=(m // tm, nj),
        in_specs=[pl.BlockSpec((tm, ka), lambda i, j: (i, 0)),
                  pl.BlockSpec((tm, D_INNER), lambda i, j: (i, 0)),
                  pl.BlockSpec((ka, tn), lambda i, j: (0, j)),
                  pl.BlockSpec((D_INNER, tn), lambda i, j: (1, j)),
                  pl.BlockSpec((tm, tn), lambda i, j: (i, j))] + cast_in,
        out_specs=[pl.BlockSpec((tm, tn), lambda i, j: (i, j))] + cast_out,
        out_shape=[jax.ShapeDtypeStruct((m, D_MODEL), f32)] + cast_shape,
        compiler_params=_cparams(("parallel", "arbitrary")),
        name="outproj",
    )(a, s, w, w, x2d, *to_cast)
    return outs[0], outs[1:]


SUBLANES = 8


def _oddeven_mergesort_pairs(n):
    pairs = []
    p = 1
    while p < n:
        k = p
        while k >= 1:
            for j in range(k % p, n - k, 2 * k):
                for i in range(min(k, n - j - k)):
                    if (i + j) // (2 * p) == (i + j + k) // (2 * p):
                        pairs.append((i + j, i + j + k))
            k //= 2
        p *= 2
    return pairs


def _sorted_top17(x):
    n_grp = x.shape[0] // SUBLANES
    assert n_grp == 16
    r = [x[g * SUBLANES:(g + 1) * SUBLANES, :] for g in range(n_grp)]

    def cmpx(a, b):
        r[a], r[b] = jnp.maximum(r[a], r[b]), jnp.minimum(r[a], r[b])

    for a, b in _oddeven_mergesort_pairs(n_grp):
        cmpx(a, b)
    for shift in (4, 2, 1):
        other = [pltpu.roll(v, shift, 0) for v in r]
        r = [jnp.maximum(r[i], other[n_grp - 1 - i]) for i in range(n_grp)]
        k = n_grp // 2
        while k >= 1:
            for i in range(n_grp):
                if i & k == 0:
                    cmpx(i, i + k)
            k //= 2
    below = [jnp.where(x[g * SUBLANES:(g + 1) * SUBLANES, :] < r[n_grp - 1], x[g * SUBLANES:(g + 1) * SUBLANES, :],
                       -jnp.inf) for g in range(n_grp)]
    while len(below) > 1:
        below = [jnp.maximum(below[i], below[i + 1]) for i in range(0, len(below), 2)]
    return [v[0:1, :] for v in r] + [jnp.max(below[0], axis=0, keepdims=True)]


def _stack_rows(vals):
    n = len(vals)
    rid = lax.broadcasted_iota(jnp.int32, (n, vals[0].shape[1]), 0)
    out = jnp.zeros((n, vals[0].shape[1]), f32)
    for r, v in enumerate(vals):
        out = jnp.where(rid == r, v, out)
    return out


N_CAND = PEER_TOPK + (PEER_TOPK // 2 - 1) * (PEER_TOPK // 2) + PEER_TOPK


def _peer_q_kernel(h_ref, g_ref, wq_ref, sk_ref, xnt_ref, e1_ref, thr_ref, e2_ref, cand_ref):
    xn = _rms(h_ref[...], g_ref[...])
    xnt_ref[...] = xn.T.astype(bf16)
    q = _dot(xn.astype(bf16), wq_ref[...]).astype(bf16)
    sk1 = sk_ref[0].astype(bf16)
    sk2 = sk_ref[1].astype(bf16)
    k = PEER_TOPK
    half = k // 2
    for h in range(PEER_HEADS):
        lo = h * 2 * HALF_KEY
        s1 = _dot_nt(sk1, q[:, lo:lo + HALF_KEY])
        s2 = _dot_nt(sk2, q[:, lo + HALF_KEY:lo + 2 * HALF_KEY])
        v1 = _sorted_top17(s1)
        v2 = _sorted_top17(s2)
        v2_top = _stack_rows(v2[:k])
        cand_ref[0:k, :] = v1[0] + v2_top
        for a in range(1, half):
            cand_ref[k + (a - 1) * half:k + a * half, :] = v1[a] + v2_top[0:half, :]
        cand_ref[k + (half - 1) * half:k + half * half, :] = _stack_rows(v1[half:k]) + v2[0]
        ninf = jnp.full_like(v1[0], -jnp.inf)
        cand_ref[k + half * half:N_CAND, :] = _stack_rows([v1[0] + v2[k], v1[k] + v2[0]] + [ninf] * (half - 2))
        cand_ref[N_CAND:, :] = jnp.full((N_KEYS - N_CAND, s1.shape[1]), -jnp.inf, f32)
        top = _sorted_top17(cand_ref[...])
        zsum = jnp.zeros_like(top[0])
        for t in top[:k]:
            zsum = zsum + jnp.exp(t - top[0])
        tau = 0.5 * (top[k - 1] + top[k])
        scale = 0.5 / zsum
        e1_ref[h] = jnp.exp(s1 - v1[0])
        thr_ref[h] = jnp.exp((tau - v2[0]) - s1) * scale
        e2_ref[h] = pltpu.bitcast((jnp.exp(s2 - v2[0]) * scale).astype(bf16), jnp.uint32)


def _peer_q(h1, g, wq, sub_keys, *, tm):
    n = h1.shape[0]
    assert n % tm == 0
    nq = PEER_HEADS * 2 * HALF_KEY
    per_head = lambda rows, dt: (pl.BlockSpec((PEER_HEADS, rows, tm), lambda i: (0, 0, i)),
                                 jax.ShapeDtypeStruct((PEER_HEADS, rows, n), dt))
    specs, shapes = zip((pl.BlockSpec((D_MODEL, tm), lambda i: (0, i)), jax.ShapeDtypeStruct((D_MODEL, n), bf16)),
                        per_head(N_KEYS, f32), per_head(N_KEYS, f32), per_head(N_KEYS // 2, jnp.uint32))
    return pl.pallas_call(
        _peer_q_kernel,
        grid=(n // tm,),
        in_specs=[pl.BlockSpec((tm, D_MODEL), lambda i: (i, 0)),
                  pl.BlockSpec((1, D_MODEL), lambda i: (0, 0)),
                  pl.BlockSpec((D_MODEL, nq), lambda i: (0, 0), pipeline_mode=pl.Buffered(1)),
                  pl.BlockSpec((2, N_KEYS, HALF_KEY), lambda i: (0, 0, 0))],
        out_specs=list(specs),
        out_shape=list(shapes),
        scratch_shapes=[pltpu.VMEM((N_KEYS, tm), f32)],
        compiler_params=_cparams(("parallel",)),
        name="peer_query",
    )(h1, g, wq, sub_keys)


def _peer_dense_kernel(xnt_ref, u_ref, v_ref, e1_ref, thr_ref, e2_ref, h_ref, g_ref, o_ref, ht_ref, act_ref, *, te):
    j = pl.program_id(1)
    tm = xnt_ref.shape[1]
    pk = N_KEYS // 2

    @pl.when(j == 0)
    def _():
        o_ref[...] = jnp.zeros_like(o_ref)

    ht_ref[...] = _dot(u_ref[...], xnt_ref[...])
    for ii in range(te // N_KEYS):
        i1 = j * (te // N_KEYS) + ii
        e1rows = [e1_ref[h, pl.ds(i1, 1), :] for h in range(PEER_HEADS)]
        thrrows = [thr_ref[h, pl.ds(i1, 1), :] for h in range(PEER_HEADS)]
        for c in range(tm // LANE):
            cl = slice(c * LANE, (c + 1) * LANE)
            w = jnp.zeros((N_KEYS, LANE), bf16)
            for h in range(PEER_HEADS):
                e1b = jnp.broadcast_to(e1rows[h][:, cl], (N_KEYS, LANE)).astype(bf16)
                thrb = jnp.broadcast_to(thrrows[h][:, cl], (N_KEYS, LANE)).astype(bf16)
                e2 = pltpu.bitcast(e2_ref[h, :, cl], bf16)
                w = w + jnp.where(e2 >= thrb, e2, jnp.zeros_like(e2)) * e1b
            hh = ht_ref[ii * N_KEYS:(ii + 1) * N_KEYS, cl]
            g = hh * (1.0 + lax.erf(hh * np.float32(1.0 / np.sqrt(2.0))))
            act_ref[ii * pk:(ii + 1) * pk, cl] = pltpu.bitcast(g.astype(bf16) * w, jnp.uint32)
    o_ref[...] += _dot(pltpu.bitcast(act_ref[...], bf16).T, v_ref[...])

    hr = h_ref.shape[0]
    slab = pl.ds(pl.multiple_of(j * hr, hr), hr)
    o_ref[slab, :] += h_ref[...]

    @pl.when(j == pl.num_programs(1) - 1)
    def _():
        sub = 64

        def body(r, c):
            rows = pl.ds(pl.multiple_of(r * sub, sub), sub)
            o_ref[rows, :] = _rms(o_ref[rows, :], g_ref[...])
            return c
        lax.fori_loop(0, tm // sub, body, 0)


def _peer_dense(xnt, u_b, v_b, e1, thr, e2b, h1, g, *, tm, te):
    n = xnt.shape[1]
    n_exp = u_b.shape[0]
    nj = n_exp // te
    assert n % tm == 0 and n_exp % te == 0 and te % N_KEYS == 0 and tm % LANE == 0 and tm % (8 * nj) == 0
    hr = tm // nj
    return pl.pallas_call(
        functools.partial(_peer_dense_kernel, te=te),
        grid=(n // tm, nj),
        in_specs=[pl.BlockSpec((D_MODEL, tm), lambda i, j: (0, i)),
                  pl.BlockSpec((te, D_MODEL), lambda i, j: (j, 0)),
                  pl.BlockSpec((te, D_MODEL), lambda i, j: (j, 0)),
                  pl.BlockSpec((PEER_HEADS, N_KEYS, tm), lambda i, j: (0, 0, i)),
                  pl.BlockSpec((PEER_HEADS, N_KEYS, tm), lambda i, j: (0, 0, i)),
                  pl.BlockSpec((PEER_HEADS, N_KEYS // 2, tm), lambda i, j: (0, 0, i)),
                  pl.BlockSpec((hr, D_MODEL), lambda i, j: (i * nj + j, 0)),
                  pl.BlockSpec((1, D_MODEL), lambda i, j: (0, 0))],
        out_specs=pl.BlockSpec((tm, D_MODEL), lambda i, j: (i, 0)),
        out_shape=jax.ShapeDtypeStruct((n, D_MODEL), f32),
        scratch_shapes=[pltpu.VMEM((te, tm), f32),
                        pltpu.VMEM((te // 2, tm), jnp.uint32)],
        compiler_params=_cparams(("parallel", "arbitrary")),
        name="peer_dense",
    )(xnt, u_b, v_b, e1, thr, e2b, h1, g)


_W_IN_SRC = ((1600, 3648), (3648, 5696), (0, 1024), (5696, 6208), (6208, 6720), (1024, 1536),
             (1536, 1600), (1568, 1600), (1536, 1568),
             (6720, 6752), (6720, 6752), (6720, 6752))


def _prep_w_in_kernel(w_ref, o_ref):
    col = 0
    tail = []
    for lo, hi in _W_IN_SRC:
        piece = w_ref[:, lo:hi]
        if (hi - lo) % LANE == 0:
            o_ref[:, col:col + hi - lo] = piece.astype(bf16)
            col += hi - lo
        else:
            tail.append(piece)
    rows = w_ref.shape[0]
    tail.append(jnp.zeros((rows, NP - col - sum(t.shape[1] for t in tail)), f32))
    o_ref[:, col:] = jnp.concatenate(tail, axis=1).astype(bf16)


def _prep_w_in(w, *, tr=256):
    _, k, n_in = w.shape
    assert k % tr == 0
    return pl.pallas_call(
        _prep_w_in_kernel,
        grid=(k // tr,),
        in_specs=[pl.BlockSpec((None, tr, n_in), lambda i: (0, i, 0))],
        out_specs=pl.BlockSpec((tr, NP), lambda i: (i, 0)),
        out_shape=jax.ShapeDtypeStruct((k, NP), bf16),
        compiler_params=_cparams(("parallel",)),
        name="prep_w_in",
    )(w)


def _prep_w_uq(w):
    w = w.reshape(Q_LORA, MLA_HEADS, QK_NOPE + QK_ROPE)
    nope, x1, x2 = w[..., :QK_NOPE], w[..., QK_NOPE:QK_NOPE + 32], w[..., QK_NOPE + 32:]
    return jnp.concatenate([nope, x1, x2, x2, x1], axis=-1).reshape(Q_LORA, MLA_HEADS * QK_SLAB).astype(bf16)


def _rope_tables(pos):
    inv_freq = ROPE_THETA ** (-jnp.arange(0, QK_ROPE, 2, dtype=f32) / QK_ROPE)
    ang = pos[:, None] * inv_freq[None, :]
    c, s = jnp.cos(ang), jnp.sin(ang)
    zero = jnp.zeros((pos.shape[0], LANE - QK_ROPE), f32)
    return jnp.concatenate([c, c, zero], axis=1), jnp.concatenate([-s, s, zero], axis=1)


def _expand_matrix(width):
    r = np.arange(LANE)[:, None]
    col = np.arange(SSD_HEADS * width)[None, :]
    return jnp.asarray(((r % SSD_HEADS) == (col // width)) & (r < 3 * SSD_HEADS), dtype=bf16)


def _tile(m, pref):
    t = min(pref, m)
    assert m % t == 0
    return t


def kernel(x, meta_tokens, attn_norm, w_in, q_norm, w_uq, kv_norm, w_ukv, conv_w, conv_b, dt_bias, a_log, d_skip,
           ssd_norm, w_out, ffn_norm, w_query, sub_keys, u_experts, v_experts, final_norm):
    bsz, seq, d = x.shape
    assert d == D_MODEL and seq % CHUNK == 0
    n_tok = bsz * seq
    x2d = x.reshape(n_tok, d)

    w_in_b = _prep_w_in(w_in)
    wq_b = _prep_w_uq(w_uq[0])
    wkv = w_ukv[0].reshape(KV_LORA, MLA_HEADS, QK_NOPE + V_HEAD)
    wk_b = wkv[..., :QK_NOPE].reshape(KV_LORA, MLA_HEADS * QK_NOPE).astype(bf16)
    wv_b = wkv[..., QK_NOPE:].reshape(KV_LORA, MLA_HEADS * V_HEAD).astype(bf16)
    cwx, cwbc = conv_w[0][:, :D_INNER], conv_w[0][:, D_INNER:]
    cbx, cbbc = conv_b[0][None, :D_INNER], conv_b[0][None, D_INNER:]
    rep3 = lambda v, fill: jnp.concatenate([v, v, v, jnp.full((LANE - 3 * SSD_HEADS,), fill, f32)])[None, :]
    dtb = rep3(dt_bias[0], 0.0)
    alog = rep3(a_log[0], 0.0)
    dskip = jnp.repeat(d_skip[0], SSD_HEAD_DIM)[None, :]
    e64 = _expand_matrix(SSD_HEAD_DIM)
    e128 = _expand_matrix(CHUNK)

    lead_rows = jnp.concatenate([jnp.zeros((LEAD_PAD, d), x.dtype), meta_tokens.astype(x.dtype)], axis=0)
    cc_l, ss_l = _rope_tables(jnp.arange(LEAD, dtype=f32) - LEAD_PAD)
    cc_r, ss_r = _rope_tables(jnp.arange(seq, dtype=f32) + N_META)

    g_attn = attn_norm[0][None, :]
    proj_l = _inproj(lead_rows, g_attn, w_in_b, tm=LEAD)
    proj_r = _inproj(x2d, g_attn, w_in_b, tm=_tile(n_tok, 512))

    qn, kvn = q_norm[0][None, :], kv_norm[0][None, :]
    _, k_l, v_l = _mla_up(proj_l, cc_l, ss_l, qn, kvn, wq_b, wk_b, wv_b, tm=LEAD, n_pos_blocks=1)
    tmu = _tile(seq, 256)
    q_r, k_r, v_r = _mla_up(proj_r, cc_r, ss_r, qn, kvn, wq_b, wk_b, wv_b, tm=tmu, n_pos_blocks=seq // tmu)
    a_out, (w_out_b, w_query_b, u_b) = _attention(
        q_r, k_r, v_r, k_l, v_l, (w_out[0], w_query[0], u_experts[0]), bsz=bsz, seq=seq, tq=_tile(seq, 1024))

    s_out = _ssd(proj_l, proj_r, cwx, cwbc, cbx, cbbc, dtb, alog, dskip, ssd_norm[0][None, :], e64, e128,
                 bsz=bsz, seq=seq)

    h1, (v_b,) = _outproj(a_out, s_out, w_out_b, x2d, (v_experts[0],), tm=_tile(n_tok, 1024))

    xnt, e1, thr, e2b = _peer_q(h1, ffn_norm[0][None, :], w_query_b, sub_keys[0], tm=_tile(n_tok, 256))
    out = _peer_dense(xnt, u_b, v_b, e1, thr, e2b, h1, final_norm[None, :], tm=_tile(n_tok, 512), te=512)
    return out.reshape(bsz, seq, d)
```

```python
import functools

import jax
import jax.numpy as jnp
import numpy as np
from jax import lax
from jax.experimental import pallas as pl
from jax.experimental.pallas import tpu as pltpu

f32 = jnp.float32
bf16 = jnp.bfloat16

D_MODEL = 4096
N_META = 16
CHUNK = 128
LEAD = CHUNK
LEAD_PAD = CHUNK - N_META
MLA_HEADS = 16
QK_NOPE = 128
QK_ROPE = 64
V_HEAD = 128
Q_LORA = 1024
KV_LORA = 512
ROPE_THETA = 10000.0
D_INNER = 2048
SSD_HEAD_DIM = 64
SSD_HEADS = 32
SSD_GROUPS = 4
SSD_STATE = 128
CONV_W = 4
PEER_HEADS = 8
PEER_TOPK = 16
N_KEYS = 128
HALF_KEY = 128
EPS = 1e-6
NEG_INF = -1e30

LANE = 128
QK_SLAB = 2 * LANE
GROUP_W = D_INNER // SSD_GROUPS

P_Z = 0
P_XS = 2048
P_CQ = 4096
P_B = 5120
P_C = 5632
P_CKV = 6144
P_KR = 6656
P_DT = 6784
NP = 6912

VMEM_LIMIT = 56 * 1024 * 1024


def _cparams(sem, vmem=VMEM_LIMIT, flags=None):
    return pltpu.CompilerParams(dimension_semantics=sem, vmem_limit_bytes=vmem, flags=flags)


def _dot(a, b):
    return jnp.dot(a, b, preferred_element_type=f32)


def _dot_nt(a, b):
    return lax.dot_general(a, b, (((1,), (1,)), ((), ())), preferred_element_type=f32)


def _rms(x, g):
    return x * lax.rsqrt(jnp.mean(x * x, axis=-1, keepdims=True) + EPS) * g


def _cast_specs(to_cast, n_steps, step_index):
    ins, outs, shapes = [], [], []
    for w in to_cast:
        rows, cols = w.shape
        assert rows % (16 * n_steps) == 0
        spec = pl.BlockSpec((rows // n_steps, cols), step_index)
        ins.append(spec)
        outs.append(spec)
        shapes.append(jax.ShapeDtypeStruct((rows, cols), bf16))
    return ins, outs, shapes


def _cast_slabs(src_refs, dst_refs):
    for src, dst in zip(src_refs, dst_refs):
        sub = min(32, src.shape[0])

        def cast_rows(r, c, src=src, dst=dst, sub=sub):
            rows = pl.ds(pl.multiple_of(r * sub, sub), sub)
            dst[rows, :] = src[rows, :].astype(bf16)
            return c
        lax.fori_loop(0, src.shape[0] // sub, cast_rows, 0)


def _inproj_kernel(x_ref, g_ref, w_ref, o_ref, xn_ref, *, sub):
    @pl.when(pl.program_id(1) == 0)
    def _():
        def body(r, c):
            rows = pl.ds(pl.multiple_of(r * sub, sub), sub)
            xn_ref[rows, :] = _rms(x_ref[rows, :], g_ref[...]).astype(bf16)
            return c
        lax.fori_loop(0, x_ref.shape[0] // sub, body, 0)

    o_ref[...] = _dot(xn_ref[...], w_ref[...])


def _inproj(x2d, g, w, *, tm, tn=768):
    m = x2d.shape[0]
    assert m % tm == 0 and NP % tn == 0
    sub = min(64, tm)
    return pl.pallas_call(
        functools.partial(_inproj_kernel, sub=sub),
        grid=(m // tm, NP // tn),
        in_specs=[pl.BlockSpec((tm, D_MODEL), lambda i, j: (i, 0)),
                  pl.BlockSpec((1, D_MODEL), lambda i, j: (0, 0)),
                  pl.BlockSpec((D_MODEL, tn), lambda i, j: (0, j))],
        out_specs=pl.BlockSpec((tm, tn), lambda i, j: (i, j)),
        out_shape=jax.ShapeDtypeStruct((m, NP), f32),
        scratch_shapes=[pltpu.VMEM((tm, D_MODEL), bf16)],
        compiler_params=_cparams(("parallel", "arbitrary")),
        name="inproj",
    )(x2d, g, w)


def _mla_up_kernel(cq_ref, ckv_ref, kr_ref, cc_ref, ss_ref, qn_ref, kvn_ref, wq_ref, wk_ref, wv_ref,
                   q_ref, k_ref, v_ref, *, scale):
    cc = cc_ref[...]
    ss = ss_ref[...]

    def rope(t):
        return t * cc + pltpu.roll(t, 64, 1) * ss

    cqn = _rms(cq_ref[...], qn_ref[...]).astype(bf16)
    q = _dot(cqn, wq_ref[...])
    for h in range(MLA_HEADS):
        lo = h * QK_SLAB
        q_ref[:, lo:lo + LANE] = (q[:, lo:lo + LANE] * scale).astype(bf16)
        q_ref[:, lo + LANE:lo + QK_SLAB] = (rope(q[:, lo + LANE:lo + QK_SLAB]) * scale).astype(bf16)

    kvn = _rms(ckv_ref[...], kvn_ref[...]).astype(bf16)
    kn = _dot(kvn, wk_ref[...])
    kpe = rope(kr_ref[...]).astype(bf16)
    for h in range(MLA_HEADS):
        lo = h * QK_SLAB
        k_ref[:, lo:lo + LANE] = kn[:, h * LANE:(h + 1) * LANE].astype(bf16)
        k_ref[:, lo + LANE:lo + QK_SLAB] = kpe
    v_ref[...] = _dot(kvn, wv_ref[...]).astype(bf16)


def _mla_up(proj, cc, ss, qn, kvn, wq, wk, wv, *, tm, n_pos_blocks):
    m = proj.shape[0]
    assert m % tm == 0
    scale = float((QK_NOPE + QK_ROPE) ** -0.5 * np.log2(np.e))
    hq = MLA_HEADS * QK_SLAB
    hv = MLA_HEADS * V_HEAD
    const = lambda i: (0, 0)
    return pl.pallas_call(
        functools.partial(_mla_up_kernel, scale=scale),
        grid=(m // tm,),
        in_specs=[pl.BlockSpec((tm, Q_LORA), lambda i: (i, P_CQ // Q_LORA)),
                  pl.BlockSpec((tm, KV_LORA), lambda i: (i, P_CKV // KV_LORA)),
                  pl.BlockSpec((tm, LANE), lambda i: (i, P_KR // LANE)),
                  pl.BlockSpec((tm, LANE), lambda i: (i % n_pos_blocks, 0)),
                  pl.BlockSpec((tm, LANE), lambda i: (i % n_pos_blocks, 0)),
                  pl.BlockSpec((1, Q_LORA), const),
                  pl.BlockSpec((1, KV_LORA), const),
                  pl.BlockSpec((Q_LORA, hq), const),
                  pl.BlockSpec((KV_LORA, hv), const),
                  pl.BlockSpec((KV_LORA, hv), const)],
        out_specs=[pl.BlockSpec((tm, hq), lambda i: (i, 0)),
                   pl.BlockSpec((tm, hq), lambda i: (i, 0)),
                   pl.BlockSpec((tm, hv), lambda i: (i, 0))],
        out_shape=[jax.ShapeDtypeStruct((m, hq), bf16),
                   jax.ShapeDtypeStruct((m, hq), bf16),
                   jax.ShapeDtypeStruct((m, hv), bf16)],
        compiler_params=_cparams(("parallel",)),
        name="mla_up",
    )(proj, proj, proj, cc, ss, qn, kvn, wq, wk, wv)


def _attn_kernel(q_ref, k_ref, v_ref, kl_ref, vl_ref, *rest, tq, nh, n_cast):
    src_refs, o_ref, dst_refs = rest[:n_cast], rest[n_cast], rest[n_cast + 1:]
    _cast_slabs(src_refs, dst_refs)

    i = pl.program_id(2)
    qs = [q_ref[:, h * QK_SLAB:(h + 1) * QK_SLAB] for h in range(nh)]
    ksl = lambda h: slice(h * QK_SLAB, (h + 1) * QK_SLAB)
    vsl = lambda h: slice(h * V_HEAD, (h + 1) * V_HEAD)

    diag = pl.ds(pl.multiple_of(i * tq, tq), tq)
    col = lax.broadcasted_iota(jnp.int32, (tq, LEAD), 1)
    r_id = lax.broadcasted_iota(jnp.int32, (tq, tq), 0)
    c_id = lax.broadcasted_iota(jnp.int32, (tq, tq), 1)
    carry = []
    for h in range(nh):
        s_l = jnp.where(col >= LEAD_PAD, _dot_nt(qs[h], kl_ref[:, ksl(h)]), NEG_INF)
        s_d = jnp.where(c_id <= r_id, _dot_nt(qs[h], k_ref[diag, ksl(h)]), NEG_INF)
        m = jnp.maximum(jnp.max(s_l, axis=-1, keepdims=True), jnp.max(s_d, axis=-1, keepdims=True))
        p_l = jnp.exp2(s_l - m)
        p_d = jnp.exp2(s_d - m)
        l = jnp.sum(p_l, axis=-1, keepdims=True) + jnp.sum(p_d, axis=-1, keepdims=True)
        acc = _dot(p_l.astype(bf16), vl_ref[:, vsl(h)]) + _dot(p_d.astype(bf16), v_ref[diag, vsl(h)])
        carry += [m, l, acc]

    def update(s, vb, m, l, acc):
        m_new = jnp.maximum(m, jnp.max(s, axis=-1, keepdims=True))
        alpha = jnp.exp2(m - m_new)
        p = jnp.exp2(s - m_new)
        l = alpha * l + jnp.sum(p, axis=-1, keepdims=True)
        acc = alpha * acc + _dot(p.astype(bf16), vb)
        return [m_new, l, acc]

    def body(j, carry):
        rows = pl.ds(pl.multiple_of(j * tq, tq), tq)
        out = []
        for h in range(nh):
            out += update(_dot_nt(qs[h], k_ref[rows, ksl(h)]), v_ref[rows, vsl(h)], *carry[3 * h:3 * h + 3])
        return tuple(out)

    carry = lax.fori_loop(0, i, body, tuple(carry))
    for h in range(nh):
        o_ref[:, vsl(h)] = (carry[3 * h + 2] / carry[3 * h + 1]).astype(bf16)


def _attention(q, k, v, kl, vl, to_cast, *, bsz, seq, tq, nh=2):
    assert seq % tq == 0 and MLA_HEADS % nh == 0
    nq = seq // tq
    nhg = MLA_HEADS // nh
    n_steps = bsz * nhg * nq
    kw, vw = nh * QK_SLAB, nh * V_HEAD
    cast_in, cast_out, cast_shape = _cast_specs(to_cast, n_steps, lambda b, h, i: ((b * nhg + h) * nq + i, 0))
    outs = pl.pallas_call(
        functools.partial(_attn_kernel, tq=tq, nh=nh, n_cast=len(to_cast)),
        grid=(bsz, nhg, nq),
        in_specs=[pl.BlockSpec((tq, kw), lambda b, h, i: (b * nq + i, h)),
                  pl.BlockSpec((seq, kw), lambda b, h, i: (b, h)),
                  pl.BlockSpec((seq, vw), lambda b, h, i: (b, h)),
                  pl.BlockSpec((LEAD, kw), lambda b, h, i: (0, h)),
                  pl.BlockSpec((LEAD, vw), lambda b, h, i: (0, h))] + cast_in,
        out_specs=[pl.BlockSpec((tq, vw), lambda b, h, i: (b * nq + i, h))] + cast_out,
        out_shape=[jax.ShapeDtypeStruct((bsz * seq, MLA_HEADS * V_HEAD), bf16)] + cast_shape,
        compiler_params=_cparams(("parallel", "parallel", "arbitrary")),
        name="mla_attn",
    )(q, k, v, kl, vl, *to_cast)
    return outs[0], outs[1:]


def _split3(v):
    b0 = v.astype(bf16).astype(f32)
    r1 = v - b0
    b1 = r1.astype(bf16).astype(f32)
    b2 = (r1 - b1).astype(bf16).astype(f32)
    return b0, b1, b2


def _ssd_kernel(zx_l, bc_l, dt_l, zx_r, bc_r, dt_r, cwx, cwbc, cbx, cbbc, dtb, alog, dskip, ng, e64, e128,
                o_ref, st_ref, px_ref, pbc_ref):
    c = pl.program_id(1)
    is_lead = c == 0

    @pl.when(is_lead)
    def _():
        st_ref[...] = jnp.zeros_like(st_ref)
        px_ref[...] = jnp.zeros_like(px_ref)
        pbc_ref[...] = jnp.zeros_like(pbc_ref)

    zx = jnp.where(is_lead, zx_l[...], zx_r[...])
    bcraw = jnp.where(is_lead, bc_l[...], bc_r[...])
    dtraw = jnp.where(is_lead, dt_l[...], dt_r[...])
    z = zx[:, :D_INNER]
    xraw = zx[:, D_INNER:]

    row1 = lax.broadcasted_iota(jnp.int32, (CHUNK, 1), 0)
    r_id = lax.broadcasted_iota(jnp.int32, (CHUNK, CHUNK), 0)
    c_id = lax.broadcasted_iota(jnp.int32, (CHUNK, CHUNK), 1)
    tril = r_id >= c_id

    def conv_silu(cur, prev_ref, w_ref, b_ref):
        prev = prev_ref[...]
        acc = cur * w_ref[CONV_W - 1:CONV_W, :] + b_ref[...]
        for k in range(1, CONV_W):
            comb = jnp.where(row1 >= CHUNK - k, prev, cur)
            acc = acc + pltpu.roll(comb, k, 0) * w_ref[CONV_W - 1 - k:CONV_W - k, :]
        prev_ref[...] = cur
        return acc * jax.nn.sigmoid(acc)

    xs = conv_silu(xraw, px_ref, cwx, cbx)
    bcc = conv_silu(bcraw, pbc_ref, cwbc, cbbc)
    b_all = bcc[:, :SSD_GROUPS * SSD_STATE]
    c_all = bcc[:, SSD_GROUPS * SSD_STATE:].astype(bf16)

    pre = dtraw + dtb[...]
    dt = jnp.maximum(pre, 0.0) + jnp.log1p(jnp.exp(-jnp.abs(pre)))
    dt = jnp.where(jnp.logical_and(is_lead, row1 < LEAD_PAD), 0.0, dt)
    da = dt * (-jnp.exp(alog[...]))
    tri = tril.astype(f32).astype(bf16)
    d0, d1, d2 = _split3(da)
    cs = _dot(tri, d0.astype(bf16)) + _dot(tri, d1.astype(bf16)) + _dot(tri, d2.astype(bf16))
    cs_t = cs.T
    ecs = jnp.exp(cs)
    dec = jnp.exp(cs[CHUNK - 1:CHUNK, :] - cs)

    def kcat(v):
        b0, b1, b2 = _split3(v)
        return jnp.where(c_id < 32, b0, jnp.where(c_id < 64, b1, jnp.where(c_id < 96, b2, 0.0))).astype(bf16)

    ex = _dot(jnp.concatenate([kcat(dt), kcat(ecs), kcat(dec)], axis=0), e64[...])
    dt_x = ex[0:CHUNK]
    ecs_x = ex[CHUNK:2 * CHUNK]
    dec_x = ex[2 * CHUNK:3 * CHUNK]
    cs_bc = _dot(kcat(cs), e128[...])

    xdt = xs * dt_x
    xdt_b = xdt.astype(bf16)
    xd_b = (xdt * dec_x).astype(bf16)

    ys = []
    for g in range(SSD_GROUPS):
        gl = slice(g * GROUP_W, (g + 1) * GROUP_W)
        b_g = b_all[:, g * SSD_STATE:(g + 1) * SSD_STATE]
        c_g = c_all[:, g * SSD_STATE:(g + 1) * SSD_STATE]
        cb = _dot_nt(c_g, b_g.astype(bf16))
        st_g = st_ref[:, gl]
        y_g = _dot(c_g, st_g.astype(bf16)) * ecs_x[:, gl]
        pieces = []
        for pr in range(GROUP_W // LANE):
            h0 = g * (GROUP_W // SSD_HEAD_DIM) + 2 * pr
            xpair = xdt_b[:, h0 * SSD_HEAD_DIM:(h0 + 2) * SSD_HEAD_DIM]
            yh = []
            for h in (h0, h0 + 1):
                seg = cs_bc[:, h * CHUNK:(h + 1) * CHUNK] - cs_t[h:h + 1, :]
                lm = jnp.where(tril, jnp.exp(jnp.minimum(seg, 0.0)), 0.0)
                yh.append(_dot((cb * lm).astype(bf16), xpair))
            pieces.append(jnp.where(c_id < SSD_HEAD_DIM, yh[0], yh[1]))
        y_g = y_g + jnp.concatenate(pieces, axis=1)
        st_ref[:, gl] = st_g * ecs_x[CHUNK - 1:CHUNK, gl] + _dot(b_g.T.astype(bf16), xd_b[:, gl])
        ys.append(y_g)

    y = jnp.concatenate(ys, axis=1) + xs * dskip[...]
    gated = y * (z * jax.nn.sigmoid(z))
    outs = []
    for g in range(SSD_GROUPS):
        gg = gated[:, g * GROUP_W:(g + 1) * GROUP_W]
        outs.append(gg * lax.rsqrt(jnp.mean(gg * gg, axis=-1, keepdims=True) + EPS))
    o_ref[...] = (jnp.concatenate(outs, axis=1) * ng[...]).astype(bf16)


def _ssd(proj_l, proj_r, cwx, cwbc, cbx, cbbc, dtb, alog, dskip, ng, e64, e128, *, bsz, seq):
    nch = seq // CHUNK
    real = lambda w: (lambda b, c: (b * nch + jnp.maximum(c - 1, 0), w))
    lead = lambda w: (lambda b, c: (0, w))
    const = lambda b, c: (0, 0)
    zx_w, bc_w, dt_w = 2 * D_INNER, 2 * SSD_GROUPS * SSD_STATE, LANE
    return pl.pallas_call(
        _ssd_kernel,
        grid=(bsz, nch + 1),
        in_specs=[pl.BlockSpec((CHUNK, zx_w), lead(P_Z // zx_w)),
                  pl.BlockSpec((CHUNK, bc_w), lead(P_B // bc_w)),
                  pl.BlockSpec((CHUNK, dt_w), lead(P_DT // dt_w)),
                  pl.BlockSpec((CHUNK, zx_w), real(P_Z // zx_w)),
                  pl.BlockSpec((CHUNK, bc_w), real(P_B // bc_w)),
                  pl.BlockSpec((CHUNK, dt_w), real(P_DT // dt_w)),
                  pl.BlockSpec((CONV_W, D_INNER), const),
                  pl.BlockSpec((CONV_W, bc_w), const),
                  pl.BlockSpec((1, D_INNER), const),
                  pl.BlockSpec((1, bc_w), const),
                  pl.BlockSpec((1, LANE), const),
                  pl.BlockSpec((1, LANE), const),
                  pl.BlockSpec((1, D_INNER), const),
                  pl.BlockSpec((1, D_INNER), const),
                  pl.BlockSpec((LANE, D_INNER), const),
                  pl.BlockSpec((LANE, SSD_HEADS * CHUNK), const)],
        out_specs=pl.BlockSpec((CHUNK, D_INNER), real(0)),
        out_shape=jax.ShapeDtypeStruct((bsz * seq, D_INNER), bf16),
        scratch_shapes=[pltpu.VMEM((SSD_STATE, D_INNER), f32),
                        pltpu.VMEM((CHUNK, D_INNER), f32),
                        pltpu.VMEM((CHUNK, bc_w), f32)],
        compiler_params=_cparams(("parallel", "arbitrary")),
        name="ssd",
    )(proj_l, proj_l, proj_l, proj_r, proj_r, proj_r, cwx, cwbc, cbx, cbbc, dtb, alog, dskip, ng, e64, e128)


def _outproj_kernel(a_ref, s_ref, wa_ref, ws_ref, x_ref, *rest, n_cast):
    src_refs, o_ref, dst_refs = rest[:n_cast], rest[n_cast], rest[n_cast + 1:]
    _cast_slabs(src_refs, dst_refs)
    o_ref[...] = _dot(a_ref[...], wa_ref[...]) + _dot(s_ref[...], ws_ref[...]) + x_ref[...]


def _outproj(a, s, w, x2d, to_cast, *, tm, tn=512):
    m = a.shape[0]
    ka = a.shape[1]
    assert m % tm == 0 and D_MODEL % tn == 0
    nj = D_MODEL // tn
    cast_in, cast_out, cast_shape = _cast_specs(to_cast, (m // tm) * nj, lambda i, j: (i * nj + j, 0))
    outs = pl.pallas_call(
        functools.partial(_outproj_kernel, n_cast=len(to_cast)),
        grid=(m // tm, nj),
        in_specs=[pl.BlockSpec((tm, ka), lambda i, j: (i, 0)),
                  pl.BlockSpec((tm, D_INNER), lambda i, j: (i, 0)),
                  pl.BlockSpec((ka, tn), lambda i, j: (0, j)),
                  pl.BlockSpec((D_INNER, tn), lambda i, j: (1, j)),
                  pl.BlockSpec((tm, tn), lambda i, j: (i, j))] + cast_in,
        out_specs=[pl.BlockSpec((tm, tn), lambda i, j: (i, j))] + cast_out,
        out_shape=[jax.ShapeDtypeStruct((m, D_MODEL), f32)] + cast_shape,
        compiler_params=_cparams(("parallel", "arbitrary")),
        name="outproj",
    )(a, s, w, w, x2d, *to_cast)
    return outs[0], outs[1:]


SUBLANES = 8


def _oddeven_mergesort_pairs(n):
    pairs = []
    p = 1
    while p < n:
        k = p
        while k >= 1:
            for j in range(k % p, n - k, 2 * k):
                for i in range(min(k, n - j - k)):
                    if (i + j) // (2 * p) == (i + j + k) // (2 * p):
                        pairs.append((i + j, i + j + k))
            k //= 2
        p *= 2
    return pairs


def _sorted_top17(x):
    n_grp = x.shape[0] // SUBLANES
    assert n_grp == 16
    r = [x[g * SUBLANES:(g + 1) * SUBLANES, :] for g in range(n_grp)]

    def cmpx(a, b):
        r[a], r[b] = jnp.maximum(r[a], r[b]), jnp.minimum(r[a], r[b])

    for a, b in _oddeven_mergesort_pairs(n_grp):
        cmpx(a, b)
    for shift in (4, 2, 1):
        other = [pltpu.roll(v, shift, 0) for v in r]
        r = [jnp.maximum(r[i], other[n_grp - 1 - i]) for i in range(n_grp)]
        k = n_grp // 2
        while k >= 1:
            for i in range(n_grp):
                if i & k == 0:
                    cmpx(i, i + k)
            k //= 2
    below = [jnp.where(x[g * SUBLANES:(g + 1) * SUBLANES, :] < r[n_grp - 1], x[g * SUBLANES:(g + 1) * SUBLANES, :],
                       -jnp.inf) for g in range(n_grp)]
    while len(below) > 1:
        below = [jnp.maximum(below[i], below[i + 1]) for i in range(0, len(below), 2)]
    return [v[0:1, :] for v in r] + [jnp.max(below[0], axis=0, keepdims=True)]


def _stack_rows(vals):
    n = len(vals)
    rid = lax.broadcasted_iota(jnp.int32, (n, vals[0].shape[1]), 0)
    out = jnp.zeros((n, vals[0].shape[1]), f32)
    for r, v in enumerate(vals):
        out = jnp.where(rid == r, v, out)
    return out


N_CAND = PEER_TOPK + (PEER_TOPK // 2 - 1) * (PEER_TOPK // 2) + PEER_TOPK


def _peer_q_kernel(h_ref, g_ref, wq_ref, sk_ref, xnt_ref, e1_ref, thr_ref, e2_ref, cand_ref):
    xn = _rms(h_ref[...], g_ref[...])
    xnt_ref[...] = xn.T.astype(bf16)
    q = _dot(xn.astype(bf16), wq_ref[...]).astype(bf16)
    sk1 = sk_ref[0].astype(bf16)
    sk2 = sk_ref[1].astype(bf16)
    k = PEER_TOPK
    half = k // 2
    for h in range(PEER_HEADS):
        lo = h * 2 * HALF_KEY
        s1 = _dot_nt(sk1, q[:, lo:lo + HALF_KEY])
        s2 = _dot_nt(sk2, q[:, lo + HALF_KEY:lo + 2 * HALF_KEY])
        v1 = _sorted_top17(s1)
        v2 = _sorted_top17(s2)
        v2_top = _stack_rows(v2[:k])
        cand_ref[0:k, :] = v1[0] + v2_top
        for a in range(1, half):
            cand_ref[k + (a - 1) * half:k + a * half, :] = v1[a] + v2_top[0:half, :]
        cand_ref[k + (half - 1) * half:k + half * half, :] = _stack_rows(v1[half:k]) + v2[0]
        ninf = jnp.full_like(v1[0], -jnp.inf)
        cand_ref[k + half * half:N_CAND, :] = _stack_rows([v1[0] + v2[k], v1[k] + v2[0]] + [ninf] * (half - 2))
        cand_ref[N_CAND:, :] = jnp.full((N_KEYS - N_CAND, s1.shape[1]), -jnp.inf, f32)
        top = _sorted_top17(cand_ref[...])
        zsum = jnp.zeros_like(top[0])
        for t in top[:k]:
            zsum = zsum + jnp.exp(t - top[0])
        tau = 0.5 * (top[k - 1] + top[k])
        scale = 0.5 / zsum
        e1_ref[h] = jnp.exp(s1 - v1[0])
        thr_ref[h] = jnp.exp((tau - v2[0]) - s1) * scale
        e2_ref[h] = pltpu.bitcast((jnp.exp(s2 - v2[0]) * scale).astype(bf16), jnp.uint32)


def _peer_q(h1, g, wq, sub_keys, *, tm):
    n = h1.shape[0]
    assert n % tm == 0
    nq = PEER_HEADS * 2 * HALF_KEY
    per_head = lambda rows, dt: (pl.BlockSpec((PEER_HEADS, rows, tm), lambda i: (0, 0, i)),
                                 jax.ShapeDtypeStruct((PEER_HEADS, rows, n), dt))
    specs, shapes = zip((pl.BlockSpec((D_MODEL, tm), lambda i: (0, i)), jax.ShapeDtypeStruct((D_MODEL, n), bf16)),
                        per_head(N_KEYS, f32), per_head(N_KEYS, f32), per_head(N_KEYS // 2, jnp.uint32))
    return pl.pallas_call(
        _peer_q_kernel,
        grid=(n // tm,),
        in_specs=[pl.BlockSpec((tm, D_MODEL), lambda i: (i, 0)),
                  pl.BlockSpec((1, D_MODEL), lambda i: (0, 0)),
                  pl.BlockSpec((D_MODEL, nq), lambda i: (0, 0), pipeline_mode=pl.Buffered(1)),
                  pl.BlockSpec((2, N_KEYS, HALF_KEY), lambda i: (0, 0, 0))],
        out_specs=list(specs),
        out_shape=list(shapes),
        scratch_shapes=[pltpu.VMEM((N_KEYS, tm), f32)],
        compiler_params=_cparams(("parallel",)),
        name="peer_query",
    )(h1, g, wq, sub_keys)


def _peer_dense_kernel(xnt_ref, u_ref, v_ref, e1_ref, thr_ref, e2_ref, h_ref, g_ref, o_ref, ht_ref, act_ref, *, te):
    j = pl.program_id(1)
    tm = xnt_ref.shape[1]
    pk = N_KEYS // 2

    @pl.when(j == 0)
    def _():
        o_ref[...] = jnp.zeros_like(o_ref)

    ht_ref[...] = _dot(u_ref[...], xnt_ref[...])
    for ii in range(te // N_KEYS):
        i1 = j * (te // N_KEYS) + ii
        e1rows = [e1_ref[h, pl.ds(i1, 1), :] for h in range(PEER_HEADS)]
        thrrows = [thr_ref[h, pl.ds(i1, 1), :] for h in range(PEER_HEADS)]
        for c in range(tm // LANE):
            cl = slice(c * LANE, (c + 1) * LANE)
            w = jnp.zeros((N_KEYS, LANE), bf16)
            for h in range(PEER_HEADS):
                e1b = jnp.broadcast_to(e1rows[h][:, cl], (N_KEYS, LANE)).astype(bf16)
                thrb = jnp.broadcast_to(thrrows[h][:, cl], (N_KEYS, LANE)).astype(bf16)
                e2 = pltpu.bitcast(e2_ref[h, :, cl], bf16)
                w = w + jnp.where(e2 >= thrb, e2, jnp.zeros_like(e2)) * e1b
            hh = ht_ref[ii * N_KEYS:(ii + 1) * N_KEYS, cl]
            g = hh * (1.0 + lax.erf(hh * np.float32(1.0 / np.sqrt(2.0))))
            act_ref[ii * pk:(ii + 1) * pk, cl] = pltpu.bitcast(g.astype(bf16) * w, jnp.uint32)
    o_ref[...] += _dot(pltpu.bitcast(act_ref[...], bf16).T, v_ref[...])

    hr = h_ref.shape[0]
    slab = pl.ds(pl.multiple_of(j * hr, hr), hr)
    o_ref[slab, :] += h_ref[...]

    @pl.when(j == pl.num_programs(1) - 1)
    def _():
        sub = 64

        def body(r, c):
            rows = pl.ds(pl.multiple_of(r * sub, sub), sub)
            o_ref[rows, :] = _rms(o_ref[rows, :], g_ref[...])
            return c
        lax.fori_loop(0, tm // sub, body, 0)


def _peer_dense(xnt, u_b, v_b, e1, thr, e2b, h1, g, *, tm, te):
    n = xnt.shape[1]
    n_exp = u_b.shape[0]
    nj = n_exp // te
    assert n % tm == 0 and n_exp % te == 0 and te % N_KEYS == 0 and tm % LANE == 0 and tm % (8 * nj) == 0
    hr = tm // nj
    return pl.pallas_call(
        functools.partial(_peer_dense_kernel, te=te),
        grid=(n // tm, nj),
        in_specs=[pl.BlockSpec((D_MODEL, tm), lambda i, j: (0, i)),
                  pl.BlockSpec((te, D_MODEL), lambda i, j: (j, 0)),
                  pl.BlockSpec((te, D_MODEL), lambda i, j: (j, 0)),
                  pl.BlockSpec((PEER_HEADS, N_KEYS, tm), lambda i, j: (0, 0, i)),
                  pl.BlockSpec((PEER_HEADS, N_KEYS, tm), lambda i, j: (0, 0, i)),
                  pl.BlockSpec((PEER_HEADS, N_KEYS // 2, tm), lambda i, j: (0, 0, i)),
                  pl.BlockSpec((hr, D_MODEL), lambda i, j: (i * nj + j, 0)),
                  pl.BlockSpec((1, D_MODEL), lambda i, j: (0, 0))],
        out_specs=pl.BlockSpec((tm, D_MODEL), lambda i, j: (i, 0)),
        out_shape=jax.ShapeDtypeStruct((n, D_MODEL), f32),
        scratch_shapes=[pltpu.VMEM((te, tm), f32),
                        pltpu.VMEM((te // 2, tm), jnp.uint32)],
        compiler_params=_cparams(("parallel", "arbitrary")),
        name="peer_dense",
    )(xnt, u_b, v_b, e1, thr, e2b, h1, g)


_W_IN_SRC = ((1600, 3648), (3648, 5696), (0, 1024), (5696, 6208), (6208, 6720), (1024, 1536),
             (1536, 1600), (1568, 1600), (1536, 1568),
             (6720, 6752), (6720, 6752), (6720, 6752))


def _prep_w_in_kernel(w_ref, o_ref):
    col = 0
    tail = []
    for lo, hi in _W_IN_SRC:
        piece = w_ref[:, lo:hi]
        if (hi - lo) % LANE == 0:
            o_ref[:, col:col + hi - lo] = piece.astype(bf16)
            col += hi - lo
        else:
            tail.append(piece)
    rows = w_ref.shape[0]
    tail.append(jnp.zeros((rows, NP - col - sum(t.shape[1] for t in tail)), f32))
    o_ref[:, col:] = jnp.concatenate(tail, axis=1).astype(bf16)


def _prep_w_in(w, *, tr=256):
    _, k, n_in = w.shape
    assert k % tr == 0
    return pl.pallas_call(
        _prep_w_in_kernel,
        grid=(k // tr,),
        in_specs=[pl.BlockSpec((None, tr, n_in), lambda i: (0, i, 0))],
        out_specs=pl.BlockSpec((tr, NP), lambda i: (i, 0)),
        out_shape=jax.ShapeDtypeStruct((k, NP), bf16),
        compiler_params=_cparams(("parallel",)),
        name="prep_w_in",
    )(w)


def _prep_w_uq(w):
    w = w.reshape(Q_LORA, MLA_HEADS, QK_NOPE + QK_ROPE)
    nope, x1, x2 = w[..., :QK_NOPE], w[..., QK_NOPE:QK_NOPE + 32], w[..., QK_NOPE + 32:]
    return jnp.concatenate([nope, x1, x2, x2, x1], axis=-1).reshape(Q_LORA, MLA_HEADS * QK_SLAB).astype(bf16)


def _rope_tables(pos):
    inv_freq = ROPE_THETA ** (-jnp.arange(0, QK_ROPE, 2, dtype=f32) / QK_ROPE)
    ang = pos[:, None] * inv_freq[None, :]
    c, s = jnp.cos(ang), jnp.sin(ang)
    zero = jnp.zeros((pos.shape[0], LANE - QK_ROPE), f32)
    return jnp.concatenate([c, c, zero], axis=1), jnp.concatenate([-s, s, zero], axis=1)


def _expand_matrix(width):
    r = np.arange(LANE)[:, None]
    col = np.arange(SSD_HEADS * width)[None, :]
    return jnp.asarray(((r % SSD_HEADS) == (col // width)) & (r < 3 * SSD_HEADS), dtype=bf16)


def _tile(m, pref):
    t = min(pref, m)
    assert m % t == 0
    return t


def kernel(x, meta_tokens, attn_norm, w_in, q_norm, w_uq, kv_norm, w_ukv, conv_w, conv_b, dt_bias, a_log, d_skip,
           ssd_norm, w_out, ffn_norm, w_query, sub_keys, u_experts, v_experts, final_norm):
    bsz, seq, d = x.shape
    assert d == D_MODEL and seq % CHUNK == 0
    n_tok = bsz * seq
    x2d = x.reshape(n_tok, d)

    w_in_b = _prep_w_in(w_in)
    wq_b = _prep_w_uq(w_uq[0])
    wkv = w_ukv[0].reshape(KV_LORA, MLA_HEADS, QK_NOPE + V_HEAD)
    wk_b = wkv[..., :QK_NOPE].reshape(KV_LORA, MLA_HEADS * QK_NOPE).astype(bf16)
    wv_b = wkv[..., QK_NOPE:].reshape(KV_LORA, MLA_HEADS * V_HEAD).astype(bf16)
    cwx, cwbc = conv_w[0][:, :D_INNER], conv_w[0][:, D_INNER:]
    cbx, cbbc = conv_b[0][None, :D_INNER], conv_b[0][None, D_INNER:]
    rep3 = lambda v, fill: jnp.concatenate([v, v, v, jnp.full((LANE - 3 * SSD_HEADS,), fill, f32)])[None, :]
    dtb = rep3(dt_bias[0], 0.0)
    alog = rep3(a_log[0], 0.0)
    dskip = jnp.repeat(d_skip[0], SSD_HEAD_DIM)[None, :]
    e64 = _expand_matrix(SSD_HEAD_DIM)
    e128 = _expand_matrix(CHUNK)

    lead_rows = jnp.concatenate([jnp.zeros((LEAD_PAD, d), x.dtype), meta_tokens.astype(x.dtype)], axis=0)
    cc_l, ss_l = _rope_tables(jnp.arange(LEAD, dtype=f32) - LEAD_PAD)
    cc_r, ss_r = _rope_tables(jnp.arange(seq, dtype=f32) + N_META)

    g_attn = attn_norm[0][None, :]
    proj_l = _inproj(lead_rows, g_attn, w_in_b, tm=LEAD)
    proj_r = _inproj(x2d, g_attn, w_in_b, tm=_tile(n_tok, 1024), tn=256)

    qn, kvn = q_norm[0][None, :], kv_norm[0][None, :]
    _, k_l, v_l = _mla_up(proj_l, cc_l, ss_l, qn, kvn, wq_b, wk_b, wv_b, tm=LEAD, n_pos_blocks=1)
    tmu = _tile(seq, 256)
    q_r, k_r, v_r = _mla_up(proj_r, cc_r, ss_r, qn, kvn, wq_b, wk_b, wv_b, tm=tmu, n_pos_blocks=seq // tmu)
    a_out, (w_out_b, w_query_b, u_b) = _attention(
        q_r, k_r, v_r, k_l, v_l, (w_out[0], w_query[0], u_experts[0]), bsz=bsz, seq=seq, tq=_tile(seq, 1024))

    s_out = _ssd(proj_l, proj_r, cwx, cwbc, cbx, cbbc, dtb, alog, dskip, ssd_norm[0][None, :], e64, e128,
                 bsz=bsz, seq=seq)

    h1, (v_b,) = _outproj(a_out, s_out, w_out_b, x2d, (v_experts[0],), tm=_tile(n_tok, 1024))

    xnt, e1, thr, e2b = _peer_q(h1, ffn_norm[0][None, :], w_query_b, sub_keys[0], tm=_tile(n_tok, 256))
    out = _peer_dense(xnt, u_b, v_b, e1, thr, e2b, h1, final_norm[None, :], tm=_tile(n_tok, 512), te=512)
    return out.reshape(bsz, seq, d)
```

```python
import functools

import jax
import jax.numpy as jnp
import numpy as np
from jax import lax
from jax.experimental import pallas as pl
from jax.experimental.pallas import tpu as pltpu

f32 = jnp.float32
bf16 = jnp.bfloat16

D_MODEL = 4096
N_META = 16
CHUNK = 128
LEAD = CHUNK
LEAD_PAD = CHUNK - N_META
MLA_HEADS = 16
QK_NOPE = 128
QK_ROPE = 64
V_HEAD = 128
Q_LORA = 1024
KV_LORA = 512
ROPE_THETA = 10000.0
D_INNER = 2048
SSD_HEAD_DIM = 64
SSD_HEADS = 32
SSD_GROUPS = 4
SSD_STATE = 128
CONV_W = 4
PEER_HEADS = 8
PEER_TOPK = 16
N_KEYS = 128
HALF_KEY = 128
EPS = 1e-6
NEG_INF = -1e30

LANE = 128
QK_SLAB = 2 * LANE
GROUP_W = D_INNER // SSD_GROUPS

P_Z = 0
P_XS = 2048
P_CQ = 4096
P_B = 5120
P_C = 5632
P_CKV = 6144
P_KR = 6656
P_DT = 6784
NP = 6912

VMEM_LIMIT = 56 * 1024 * 1024


def _cparams(sem, vmem=VMEM_LIMIT, flags=None):
    return pltpu.CompilerParams(dimension_semantics=sem, vmem_limit_bytes=vmem, flags=flags)


def _dot(a, b):
    return jnp.dot(a, b, preferred_element_type=f32)


def _dot_nt(a, b):
    return lax.dot_general(a, b, (((1,), (1,)), ((), ())), preferred_element_type=f32)


def _rms(x, g):
    return x * lax.rsqrt(jnp.mean(x * x, axis=-1, keepdims=True) + EPS) * g


def _cast_specs(to_cast, n_steps, step_index):
    ins, outs, shapes = [], [], []
    for w in to_cast:
        rows, cols = w.shape
        assert rows % (16 * n_steps) == 0
        spec = pl.BlockSpec((rows // n_steps, cols), step_index)
        ins.append(spec)
        outs.append(spec)
        shapes.append(jax.ShapeDtypeStruct((rows, cols), bf16))
    return ins, outs, shapes


def _cast_slabs(src_refs, dst_refs):
    for src, dst in zip(src_refs, dst_refs):
        sub = min(32, src.shape[0])

        def cast_rows(r, c, src=src, dst=dst, sub=sub):
            rows = pl.ds(pl.multiple_of(r * sub, sub), sub)
            dst[rows, :] = src[rows, :].astype(bf16)
            return c
        lax.fori_loop(0, src.shape[0] // sub, cast_rows, 0)


def _inproj_kernel(x_ref, g_ref, w_ref, o_ref, xn_ref, *, sub):
    @pl.when(pl.program_id(1) == 0)
    def _():
        def body(r, c):
            rows = pl.ds(pl.multiple_of(r * sub, sub), sub)
            xn_ref[rows, :] = _rms(x_ref[rows, :], g_ref[...]).astype(bf16)
            return c
        lax.fori_loop(0, x_ref.shape[0] // sub, body, 0)

    o_ref[...] = _dot(xn_ref[...], w_ref[...])


def _inproj(x2d, g, w, *, tm, tn=768):
    m = x2d.shape[0]
    assert m % tm == 0 and NP % tn == 0
    sub = min(64, tm)
    return pl.pallas_call(
        functools.partial(_inproj_kernel, sub=sub),
        grid=(m // tm, NP // tn),
        in_specs=[pl.BlockSpec((tm, D_MODEL), lambda i, j: (i, 0)),
                  pl.BlockSpec((1, D_MODEL), lambda i, j: (0, 0)),
                  pl.BlockSpec((D_MODEL, tn), lambda i, j: (0, j))],
        out_specs=pl.BlockSpec((tm, tn), lambda i, j: (i, j)),
        out_shape=jax.ShapeDtypeStruct((m, NP), f32),
        scratch_shapes=[pltpu.VMEM((tm, D_MODEL), bf16)],
        compiler_params=_cparams(("parallel", "arbitrary")),
        name="inproj",
    )(x2d, g, w)


def _mla_up_kernel(cq_ref, ckv_ref, kr_ref, cc_ref, ss_ref, qn_ref, kvn_ref, wq_ref, wk_ref, wv_ref,
                   q_ref, k_ref, v_ref, *, scale):
    cc = cc_ref[...]
    ss = ss_ref[...]

    def rope(t):
        return t * cc + pltpu.roll(t, 64, 1) * ss

    cqn = _rms(cq_ref[...], qn_ref[...]).astype(bf16)
    q = _dot(cqn, wq_ref[...])
    for h in range(MLA_HEADS):
        lo = h * QK_SLAB
        q_ref[:, lo:lo + LANE] = (q[:, lo:lo + LANE] * scale).astype(bf16)
        q_ref[:, lo + LANE:lo + QK_SLAB] = (rope(q[:, lo + LANE:lo + QK_SLAB]) * scale).astype(bf16)

    kvn = _rms(ckv_ref[...], kvn_ref[...]).astype(bf16)
    kn = _dot(kvn, wk_ref[...])
    kpe = rope(kr_ref[...]).astype(bf16)
    for h in range(MLA_HEADS):
        lo = h * QK_SLAB
        k_ref[:, lo:lo + LANE] = kn[:, h * LANE:(h + 1) * LANE].astype(bf16)
        k_ref[:, lo + LANE:lo + QK_SLAB] = kpe
    v_ref[...] = _dot(kvn, wv_ref[...]).astype(bf16)


def _mla_up(proj, cc, ss, qn, kvn, wq, wk, wv, *, tm, n_pos_blocks):
    m = proj.shape[0]
    assert m % tm == 0
    scale = float((QK_NOPE + QK_ROPE) ** -0.5 * np.log2(np.e))
    hq = MLA_HEADS * QK_SLAB
    hv = MLA_HEADS * V_HEAD
    const = lambda i: (0, 0)
    return pl.pallas_call(
        functools.partial(_mla_up_kernel, scale=scale),
        grid=(m // tm,),
        in_specs=[pl.BlockSpec((tm, Q_LORA), lambda i: (i, P_CQ // Q_LORA)),
                  pl.BlockSpec((tm, KV_LORA), lambda i: (i, P_CKV // KV_LORA)),
                  pl.BlockSpec((tm, LANE), lambda i: (i, P_KR // LANE)),
                  pl.BlockSpec((tm, LANE), lambda i: (i % n_pos_blocks, 0)),
                  pl.BlockSpec((tm, LANE), lambda i: (i % n_pos_blocks, 0)),
                  pl.BlockSpec((1, Q_LORA), const),
                  pl.BlockSpec((1, KV_LORA), const),
                  pl.BlockSpec((Q_LORA, hq), const),
                  pl.BlockSpec((KV_LORA, hv), const),
                  pl.BlockSpec((KV_LORA, hv), const)],
        out_specs=[pl.BlockSpec((tm, hq), lambda i: (i, 0)),
                   pl.BlockSpec((tm, hq), lambda i: (i, 0)),
                   pl.BlockSpec((tm, hv), lambda i: (i, 0))],
        out_shape=[jax.ShapeDtypeStruct((m, hq), bf16),
                   jax.ShapeDtypeStruct((m, hq), bf16),
                   jax.ShapeDtypeStruct((m, hv), bf16)],
        compiler_params=_cparams(("parallel",)),
        name="mla_up",
    )(proj, proj, proj, cc, ss, qn, kvn, wq, wk, wv)


def _attn_kernel(q_ref, k_ref, v_ref, kl_ref, vl_ref, *rest, tq, nh, n_cast):
    src_refs, o_ref, dst_refs = rest[:n_cast], rest[n_cast], rest[n_cast + 1:]
    _cast_slabs(src_refs, dst_refs)

    i = pl.program_id(2)
    qs = [q_ref[:, h * QK_SLAB:(h + 1) * QK_SLAB] for h in range(nh)]
    ksl = lambda h: slice(h * QK_SLAB, (h + 1) * QK_SLAB)
    vsl = lambda h: slice(h * V_HEAD, (h + 1) * V_HEAD)

    diag = pl.ds(pl.multiple_of(i * tq, tq), tq)
    col = lax.broadcasted_iota(jnp.int32, (tq, LEAD), 1)
    r_id = lax.broadcasted_iota(jnp.int32, (tq, tq), 0)
    c_id = lax.broadcasted_iota(jnp.int32, (tq, tq), 1)
    carry = []
    for h in range(nh):
        s_l = jnp.where(col >= LEAD_PAD, _dot_nt(qs[h], kl_ref[:, ksl(h)]), NEG_INF)
        s_d = jnp.where(c_id <= r_id, _dot_nt(qs[h], k_ref[diag, ksl(h)]), NEG_INF)
        m = jnp.maximum(jnp.max(s_l, axis=-1, keepdims=True), jnp.max(s_d, axis=-1, keepdims=True))
        p_l = jnp.exp2(s_l - m)
        p_d = jnp.exp2(s_d - m)
        l = jnp.sum(p_l, axis=-1, keepdims=True) + jnp.sum(p_d, axis=-1, keepdims=True)
        acc = _dot(p_l.astype(bf16), vl_ref[:, vsl(h)]) + _dot(p_d.astype(bf16), v_ref[diag, vsl(h)])
        carry += [m, l, acc]

    def update(s, vb, m, l, acc):
        m_new = jnp.maximum(m, jnp.max(s, axis=-1, keepdims=True))
        alpha = jnp.exp2(m - m_new)
        p = jnp.exp2(s - m_new)
        l = alpha * l + jnp.sum(p, axis=-1, keepdims=True)
        acc = alpha * acc + _dot(p.astype(bf16), vb)
        return [m_new, l, acc]

    def body(j, carry):
        rows = pl.ds(pl.multiple_of(j * tq, tq), tq)
        out = []
        for h in range(nh):
            out += update(_dot_nt(qs[h], k_ref[rows, ksl(h)]), v_ref[rows, vsl(h)], *carry[3 * h:3 * h + 3])
        return tuple(out)

    carry = lax.fori_loop(0, i, body, tuple(carry))
    for h in range(nh):
        o_ref[:, vsl(h)] = (carry[3 * h + 2] / carry[3 * h + 1]).astype(bf16)


def _attention(q, k, v, kl, vl, to_cast, *, bsz, seq, tq, nh=2):
    assert seq % tq == 0 and MLA_HEADS % nh == 0
    nq = seq // tq
    nhg = MLA_HEADS // nh
    n_steps = bsz * nhg * nq
    kw, vw = nh * QK_SLAB, nh * V_HEAD
    cast_in, cast_out, cast_shape = _cast_specs(to_cast, n_steps, lambda b, h, i: ((b * nhg + h) * nq + i, 0))
    outs = pl.pallas_call(
        functools.partial(_attn_kernel, tq=tq, nh=nh, n_cast=len(to_cast)),
        grid=(bsz, nhg, nq),
        in_specs=[pl.BlockSpec((tq, kw), lambda b, h, i: (b * nq + i, h)),
                  pl.BlockSpec((seq, kw), lambda b, h, i: (b, h)),
                  pl.BlockSpec((seq, vw), lambda b, h, i: (b, h)),
                  pl.BlockSpec((LEAD, kw), lambda b, h, i: (0, h)),
                  pl.BlockSpec((LEAD, vw), lambda b, h, i: (0, h))] + cast_in,
        out_specs=[pl.BlockSpec((tq, vw), lambda b, h, i: (b * nq + i, h))] + cast_out,
        out_shape=[jax.ShapeDtypeStruct((bsz * seq, MLA_HEADS * V_HEAD), bf16)] + cast_shape,
        compiler_params=_cparams(("parallel", "parallel", "arbitrary")),
        name="mla_attn",
    )(q, k, v, kl, vl, *to_cast)
    return outs[0], outs[1:]


def _split3(v):
    b0 = v.astype(bf16).astype(f32)
    r1 = v - b0
    b1 = r1.astype(bf16).astype(f32)
    b2 = (r1 - b1).astype(bf16).astype(f32)
    return b0, b1, b2


def _ssd_kernel(zx_l, bc_l, dt_l, zx_r, bc_r, dt_r, cwx, cwbc, cbx, cbbc, dtb, alog, dskip, ng, e64, e128,
                o_ref, st_ref, px_ref, pbc_ref):
    c = pl.program_id(1)
    is_lead = c == 0

    @pl.when(is_lead)
    def _():
        st_ref[...] = jnp.zeros_like(st_ref)
        px_ref[...] = jnp.zeros_like(px_ref)
        pbc_ref[...] = jnp.zeros_like(pbc_ref)

    zx = jnp.where(is_lead, zx_l[...], zx_r[...])
    bcraw = jnp.where(is_lead, bc_l[...], bc_r[...])
    dtraw = jnp.where(is_lead, dt_l[...], dt_r[...])
    z = zx[:, :D_INNER]
    xraw = zx[:, D_INNER:]

    row1 = lax.broadcasted_iota(jnp.int32, (CHUNK, 1), 0)
    r_id = lax.broadcasted_iota(jnp.int32, (CHUNK, CHUNK), 0)
    c_id = lax.broadcasted_iota(jnp.int32, (CHUNK, CHUNK), 1)
    tril = r_id >= c_id

    def conv_silu(cur, prev_ref, w_ref, b_ref):
        prev = prev_ref[...]
        acc = cur * w_ref[CONV_W - 1:CONV_W, :] + b_ref[...]
        for k in range(1, CONV_W):
            comb = jnp.where(row1 >= CHUNK - k, prev, cur)
            acc = acc + pltpu.roll(comb, k, 0) * w_ref[CONV_W - 1 - k:CONV_W - k, :]
        prev_ref[...] = cur
        return acc * jax.nn.sigmoid(acc)

    xs = conv_silu(xraw, px_ref, cwx, cbx)
    bcc = conv_silu(bcraw, pbc_ref, cwbc, cbbc)
    b_all = bcc[:, :SSD_GROUPS * SSD_STATE]
    c_all = bcc[:, SSD_GROUPS * SSD_STATE:].astype(bf16)

    pre = dtraw + dtb[...]
    dt = jnp.maximum(pre, 0.0) + jnp.log1p(jnp.exp(-jnp.abs(pre)))
    dt = jnp.where(jnp.logical_and(is_lead, row1 < LEAD_PAD), 0.0, dt)
    da = dt * (-jnp.exp(alog[...]))
    tri = tril.astype(f32).astype(bf16)
    d0, d1, d2 = _split3(da)
    cs = _dot(tri, d0.astype(bf16)) + _dot(tri, d1.astype(bf16)) + _dot(tri, d2.astype(bf16))
    cs_t = cs.T
    ecs = jnp.exp(cs)
    dec = jnp.exp(cs[CHUNK - 1:CHUNK, :] - cs)

    def kcat(v):
        b0, b1, b2 = _split3(v)
        return jnp.where(c_id < 32, b0, jnp.where(c_id < 64, b1, jnp.where(c_id < 96, b2, 0.0))).astype(bf16)

    ex = _dot(jnp.concatenate([kcat(dt), kcat(ecs), kcat(dec)], axis=0), e64[...])
    dt_x = ex[0:CHUNK]
    ecs_x = ex[CHUNK:2 * CHUNK]
    dec_x = ex[2 * CHUNK:3 * CHUNK]
    cs_bc = _dot(kcat(cs), e128[...])

    xdt = xs * dt_x
    xdt_b = xdt.astype(bf16)
    xd_b = (xdt * dec_x).astype(bf16)

    ys = []
    for g in range(SSD_GROUPS):
        gl = slice(g * GROUP_W, (g + 1) * GROUP_W)
        b_g = b_all[:, g * SSD_STATE:(g + 1) * SSD_STATE]
        c_g = c_all[:, g * SSD_STATE:(g + 1) * SSD_STATE]
        cb = _dot_nt(c_g, b_g.astype(bf16))
        st_g = st_ref[:, gl]
        y_g = _dot(c_g, st_g.astype(bf16)) * ecs_x[:, gl]
        pieces = []
        for pr in range(GROUP_W // LANE):
            h0 = g * (GROUP_W // SSD_HEAD_DIM) + 2 * pr
            xpair = xdt_b[:, h0 * SSD_HEAD_DIM:(h0 + 2) * SSD_HEAD_DIM]
            yh = []
            for h in (h0, h0 + 1):
                seg = cs_bc[:, h * CHUNK:(h + 1) * CHUNK] - cs_t[h:h + 1, :]
                lm = jnp.where(tril, jnp.exp(jnp.minimum(seg, 0.0)), 0.0)
                yh.append(_dot((cb * lm).astype(bf16), xpair))
            pieces.append(jnp.where(c_id < SSD_HEAD_DIM, yh[0], yh[1]))
        y_g = y_g + jnp.concatenate(pieces, axis=1)
        st_ref[:, gl] = st_g * ecs_x[CHUNK - 1:CHUNK, gl] + _dot(b_g.T.astype(bf16), xd_b[:, gl])
        ys.append(y_g)

    y = jnp.concatenate(ys, axis=1) + xs * dskip[...]
    gated = y * (z * jax.nn.sigmoid(z))
    outs = []
    for g in range(SSD_GROUPS):
        gg = gated[:, g * GROUP_W:(g + 1) * GROUP_W]
        outs.append(gg * lax.rsqrt(jnp.mean(gg * gg, axis=-1, keepdims=True) + EPS))
    o_ref[...] = (jnp.concatenate(outs, axis=1) * ng[...]).astype(bf16)


def _ssd(proj_l, proj_r, cwx, cwbc, cbx, cbbc, dtb, alog, dskip, ng, e64, e128, *, bsz, seq):
    nch = seq // CHUNK
    real = lambda w: (lambda b, c: (b * nch + jnp.maximum(c - 1, 0), w))
    lead = lambda w: (lambda b, c: (0, w))
    const = lambda b, c: (0, 0)
    zx_w, bc_w, dt_w = 2 * D_INNER, 2 * SSD_GROUPS * SSD_STATE, LANE
    return pl.pallas_call(
        _ssd_kernel,
        grid=(bsz, nch + 1),
        in_specs=[pl.BlockSpec((CHUNK, zx_w), lead(P_Z // zx_w)),
                  pl.BlockSpec((CHUNK, bc_w), lead(P_B // bc_w)),
                  pl.BlockSpec((CHUNK, dt_w), lead(P_DT // dt_w)),
                  pl.BlockSpec((CHUNK, zx_w), real(P_Z // zx_w)),
                  pl.BlockSpec((CHUNK, bc_w), real(P_B // bc_w)),
                  pl.BlockSpec((CHUNK, dt_w), real(P_DT // dt_w)),
                  pl.BlockSpec((CONV_W, D_INNER), const),
                  pl.BlockSpec((CONV_W, bc_w), const),
                  pl.BlockSpec((1, D_INNER), const),
                  pl.BlockSpec((1, bc_w), const),
                  pl.BlockSpec((1, LANE), const),
                  pl.BlockSpec((1, LANE), const),
                  pl.BlockSpec((1, D_INNER), const),
                  pl.BlockSpec((1, D_INNER), const),
                  pl.BlockSpec((LANE, D_INNER), const),
                  pl.BlockSpec((LANE, SSD_HEADS * CHUNK), const)],
        out_specs=pl.BlockSpec((CHUNK, D_INNER), real(0)),
        out_shape=jax.ShapeDtypeStruct((bsz * seq, D_INNER), bf16),
        scratch_shapes=[pltpu.VMEM((SSD_STATE, D_INNER), f32),
                        pltpu.VMEM((CHUNK, D_INNER), f32),
                        pltpu.VMEM((CHUNK, bc_w), f32)],
        compiler_params=_cparams(("parallel", "arbitrary")),
        name="ssd",
    )(proj_l, proj_l, proj_l, proj_r, proj_r, proj_r, cwx, cwbc, cbx, cbbc, dtb, alog, dskip, ng, e64, e128)


def _outproj_kernel(a_ref, s_ref, wa_ref, ws_ref, x_ref, *rest, n_cast):
    src_refs, o_ref, dst_refs = rest[:n_cast], rest[n_cast], rest[n_cast + 1:]
    _cast_slabs(src_refs, dst_refs)
    o_ref[...] = _dot(a_ref[...], wa_ref[...]) + _dot(s_ref[...], ws_ref[...]) + x_ref[...]


def _outproj(a, s, w, x2d, to_cast, *, tm, tn=512):
    m = a.shape[0]
    ka = a.shape[1]
    assert m % tm == 0 and D_MODEL % tn == 0
    nj = D_MODEL // tn
    cast_in, cast_out, cast_shape = _cast_specs(to_cast, (m // tm) * nj, lambda i, j: (i * nj + j, 0))
    outs = pl.pallas_call(
        functools.partial(_outproj_kernel, n_cast=len(to_cast)),
        grid=(m // tm, nj),
        in_specs=[pl.BlockSpec((tm, ka), lambda i, j: (i, 0)),
                  pl.BlockSpec((tm, D_INNER), lambda i, j: (i, 0)),
                  pl.BlockSpec((ka, tn), lambda i, j: (0, j)),
                  pl.BlockSpec((D_INNER, tn), lambda i, j: (1, j)),
                  pl.BlockSpec((tm, tn), lambda i, j: (i, j))] + cast_in,
        out_specs=[pl.BlockSpec((tm, tn), lambda i, j: (i, j))] + cast_out,
        out_shape=[jax.ShapeDtypeStruct((m, D_MODEL), f32)] + cast_shape,
        compiler_params=_cparams(("parallel", "arbitrary")),
        name="outproj",
    )(a, s, w, w, x2d, *to_cast)
    return outs[0], outs[1:]


SUBLANES = 8


def _oddeven_mergesort_pairs(n):
    pairs = []
    p = 1
    while p < n:
        k = p
        while k >= 1:
            for j in range(k % p, n - k, 2 * k):
                for i in range(min(k, n - j - k)):
                    if (i + j) // (2 * p) == (i + j + k) // (2 * p):
                        pairs.append((i + j, i + j + k))
            k //= 2
        p *= 2
    return pairs


def _sorted_top17(x):
    n_grp = x.shape[0] // SUBLANES
    assert n_grp == 16
    r = [x[g * SUBLANES:(g + 1) * SUBLANES, :] for g in range(n_grp)]

    def cmpx(a, b):
        r[a], r[b] = jnp.maximum(r[a], r[b]), jnp.minimum(r[a], r[b])

    for a, b in _oddeven_mergesort_pairs(n_grp):
        cmpx(a, b)
    for shift in (4, 2, 1):
        other = [pltpu.roll(v, shift, 0) for v in r]
        r = [jnp.maximum(r[i], other[n_grp - 1 - i]) for i in range(n_grp)]
        k = n_grp // 2
        while k >= 1:
            for i in range(n_grp):
                if i & k == 0:
                    cmpx(i, i + k)
            k //= 2
    below = [jnp.where(x[g * SUBLANES:(g + 1) * SUBLANES, :] < r[n_grp - 1], x[g * SUBLANES:(g + 1) * SUBLANES, :],
                       -jnp.inf) for g in range(n_grp)]
    while len(below) > 1:
        below = [jnp.maximum(below[i], below[i + 1]) for i in range(0, len(below), 2)]
    return [v[0:1, :] for v in r] + [jnp.max(below[0], axis=0, keepdims=True)]


def _stack_rows(vals):
    n = len(vals)
    rid = lax.broadcasted_iota(jnp.int32, (n, vals[0].shape[1]), 0)
    out = jnp.zeros((n, vals[0].shape[1]), f32)
    for r, v in enumerate(vals):
        out = jnp.where(rid == r, v, out)
    return out


N_CAND = PEER_TOPK + (PEER_TOPK // 2 - 1) * (PEER_TOPK // 2) + PEER_TOPK


def _peer_q_kernel(h_ref, g_ref, wq_ref, sk_ref, xnt_ref, e1_ref, thr_ref, e2_ref, cand_ref):
    xn = _rms(h_ref[...], g_ref[...])
    xnt_ref[...] = xn.T.astype(bf16)
    q = _dot(xn.astype(bf16), wq_ref[...]).astype(bf16)
    sk1 = sk_ref[0].astype(bf16)
    sk2 = sk_ref[1].astype(bf16)
    k = PEER_TOPK
    half = k // 2
    for h in range(PEER_HEADS):
        lo = h * 2 * HALF_KEY
        s1 = _dot_nt(sk1, q[:, lo:lo + HALF_KEY])
        s2 = _dot_nt(sk2, q[:, lo + HALF_KEY:lo + 2 * HALF_KEY])
        v1 = _sorted_top17(s1)
        v2 = _sorted_top17(s2)
        v2_top = _stack_rows(v2[:k])
        cand_ref[0:k, :] = v1[0] + v2_top
        for a in range(1, half):
            cand_ref[k + (a - 1) * half:k + a * half, :] = v1[a] + v2_top[0:half, :]
        cand_ref[k + (half - 1) * half:k + half * half, :] = _stack_rows(v1[half:k]) + v2[0]
        ninf = jnp.full_like(v1[0], -jnp.inf)
        cand_ref[k + half * half:N_CAND, :] = _stack_rows([v1[0] + v2[k], v1[k] + v2[0]] + [ninf] * (half - 2))
        cand_ref[N_CAND:, :] = jnp.full((N_KEYS - N_CAND, s1.shape[1]), -jnp.inf, f32)
        top = _sorted_top17(cand_ref[...])
        zsum = jnp.zeros_like(top[0])
        for t in top[:k]:
            zsum = zsum + jnp.exp(t - top[0])
        tau = 0.5 * (top[k - 1] + top[k])
        scale = 0.5 / zsum
        e1_ref[h] = jnp.exp(s1 - v1[0])
        thr_ref[h] = jnp.exp((tau - v2[0]) - s1) * scale
        e2_ref[h] = pltpu.bitcast((jnp.exp(s2 - v2[0]) * scale).astype(bf16), jnp.uint32)


def _peer_q(h1, g, wq, sub_keys, *, tm):
    n = h1.shape[0]
    assert n % tm == 0
    nq = PEER_HEADS * 2 * HALF_KEY
    per_head = lambda rows, dt: (pl.BlockSpec((PEER_HEADS, rows, tm), lambda i: (0, 0, i)),
                                 jax.ShapeDtypeStruct((PEER_HEADS, rows, n), dt))
    specs, shapes = zip((pl.BlockSpec((D_MODEL, tm), lambda i: (0, i)), jax.ShapeDtypeStruct((D_MODEL, n), bf16)),
                        per_head(N_KEYS, f32), per_head(N_KEYS, f32), per_head(N_KEYS // 2, jnp.uint32))
    return pl.pallas_call(
        _peer_q_kernel,
        grid=(n // tm,),
        in_specs=[pl.BlockSpec((tm, D_MODEL), lambda i: (i, 0)),
                  pl.BlockSpec((1, D_MODEL), lambda i: (0, 0)),
                  pl.BlockSpec((D_MODEL, nq), lambda i: (0, 0), pipeline_mode=pl.Buffered(1)),
                  pl.BlockSpec((2, N_KEYS, HALF_KEY), lambda i: (0, 0, 0))],
        out_specs=list(specs),
        out_shape=list(shapes),
        scratch_shapes=[pltpu.VMEM((N_KEYS, tm), f32)],
        compiler_params=_cparams(("parallel",)),
        name="peer_query",
    )(h1, g, wq, sub_keys)


def _peer_dense_kernel(xnt_ref, u_ref, v_ref, e1_ref, thr_ref, e2_ref, h_ref, g_ref, o_ref, ht_ref, act_ref, *, te):
    j = pl.program_id(1)
    tm = xnt_ref.shape[1]
    pk = N_KEYS // 2

    @pl.when(j == 0)
    def _():
        o_ref[...] = jnp.zeros_like(o_ref)

    ht_ref[...] = _dot(u_ref[...], xnt_ref[...])
    for ii in range(te // N_KEYS):
        i1 = j * (te // N_KEYS) + ii
        e1rows = [e1_ref[h, pl.ds(i1, 1), :] for h in range(PEER_HEADS)]
        thrrows = [thr_ref[h, pl.ds(i1, 1), :] for h in range(PEER_HEADS)]
        for c in range(tm // LANE):
            cl = slice(c * LANE, (c + 1) * LANE)
            w = jnp.zeros((N_KEYS, LANE), bf16)
            for h in range(PEER_HEADS):
                e1b = jnp.broadcast_to(e1rows[h][:, cl], (N_KEYS, LANE)).astype(bf16)
                thrb = jnp.broadcast_to(thrrows[h][:, cl], (N_KEYS, LANE)).astype(bf16)
                e2 = pltpu.bitcast(e2_ref[h, :, cl], bf16)
                w = w + jnp.where(e2 >= thrb, e2, jnp.zeros_like(e2)) * e1b
            hh = ht_ref[ii * N_KEYS:(ii + 1) * N_KEYS, cl]
            g = hh * (1.0 + lax.erf(hh * np.float32(1.0 / np.sqrt(2.0))))
            act_ref[ii * pk:(ii + 1) * pk, cl] = pltpu.bitcast(g.astype(bf16) * w, jnp.uint32)
    o_ref[...] += _dot(pltpu.bitcast(act_ref[...], bf16).T, v_ref[...])

    hr = h_ref.shape[0]
    slab = pl.ds(pl.multiple_of(j * hr, hr), hr)
    o_ref[slab, :] += h_ref[...]

    @pl.when(j == pl.num_programs(1) - 1)
    def _():
        sub = 64

        def body(r, c):
            rows = pl.ds(pl.multiple_of(r * sub, sub), sub)
            o_ref[rows, :] = _rms(o_ref[rows, :], g_ref[...])
            return c
        lax.fori_loop(0, tm // sub, body, 0)


def _peer_dense(xnt, u_b, v_b, e1, thr, e2b, h1, g, *, tm, te):
    n = xnt.shape[1]
    n_exp = u_b.shape[0]
    nj = n_exp // te
    assert n % tm == 0 and n_exp % te == 0 and te % N_KEYS == 0 and tm % LANE == 0 and tm % (8 * nj) == 0
    hr = tm // nj
    return pl.pallas_call(
        functools.partial(_peer_dense_kernel, te=te),
        grid=(n // tm, nj),
        in_specs=[pl.BlockSpec((D_MODEL, tm), lambda i, j: (0, i)),
                  pl.BlockSpec((te, D_MODEL), lambda i, j: (j, 0)),
                  pl.BlockSpec((te, D_MODEL), lambda i, j: (j, 0)),
                  pl.BlockSpec((PEER_HEADS, N_KEYS, tm), lambda i, j: (0, 0, i)),
                  pl.BlockSpec((PEER_HEADS, N_KEYS, tm), lambda i, j: (0, 0, i)),
                  pl.BlockSpec((PEER_HEADS, N_KEYS // 2, tm), lambda i, j: (0, 0, i)),
                  pl.BlockSpec((hr, D_MODEL), lambda i, j: (i * nj + j, 0)),
                  pl.BlockSpec((1, D_MODEL), lambda i, j: (0, 0))],
        out_specs=pl.BlockSpec((tm, D_MODEL), lambda i, j: (i, 0)),
        out_shape=jax.ShapeDtypeStruct((n, D_MODEL), f32),
        scratch_shapes=[pltpu.VMEM((te, tm), f32),
                        pltpu.VMEM((te // 2, tm), jnp.uint32)],
        compiler_params=_cparams(("parallel", "arbitrary")),
        name="peer_dense",
    )(xnt, u_b, v_b, e1, thr, e2b, h1, g)


_W_IN_SRC = ((1600, 3648), (3648, 5696), (0, 1024), (5696, 6208), (6208, 6720), (1024, 1536),
             (1536, 1600), (1568, 1600), (1536, 1568),
             (6720, 6752), (6720, 6752), (6720, 6752))


def _prep_w_in_kernel(w_ref, o_ref):
    col = 0
    tail = []
    for lo, hi in _W_IN_SRC:
        piece = w_ref[:, lo:hi]
        if (hi - lo) % LANE == 0:
            o_ref[:, col:col + hi - lo] = piece.astype(bf16)
            col += hi - lo
        else:
            tail.append(piece)
    rows = w_ref.shape[0]
    tail.append(jnp.zeros((rows, NP - col - sum(t.shape[1] for t in tail)), f32))
    o_ref[:, col:] = jnp.concatenate(tail, axis=1).astype(bf16)


def _prep_w_in(w, *, tr=256):
    _, k, n_in = w.shape
    assert k % tr == 0
    n_blk = -(-n_in // LANE) * LANE
    return pl.pallas_call(
        _prep_w_in_kernel,
        grid=(k // tr,),
        in_specs=[pl.BlockSpec((None, tr, n_blk), lambda i: (0, i, 0))],
        out_specs=pl.BlockSpec((tr, NP), lambda i: (i, 0)),
        out_shape=jax.ShapeDtypeStruct((k, NP), bf16),
        compiler_params=_cparams(("parallel",)),
        name="prep_w_in",
    )(w)


def _prep_w_uq(w):
    w = w.reshape(Q_LORA, MLA_HEADS, QK_NOPE + QK_ROPE)
    nope, x1, x2 = w[..., :QK_NOPE], w[..., QK_NOPE:QK_NOPE + 32], w[..., QK_NOPE + 32:]
    return jnp.concatenate([nope, x1, x2, x2, x1], axis=-1).reshape(Q_LORA, MLA_HEADS * QK_SLAB).astype(bf16)


def _rope_tables(pos):
    inv_freq = ROPE_THETA ** (-jnp.arange(0, QK_ROPE, 2, dtype=f32) / QK_ROPE)
    ang = pos[:, None] * inv_freq[None, :]
    c, s = jnp.cos(ang), jnp.sin(ang)
    zero = jnp.zeros((pos.shape[0], LANE - QK_ROPE), f32)
    return jnp.concatenate([c, c, zero], axis=1), jnp.concatenate([-s, s, zero], axis=1)


def _expand_matrix(width):
    r = np.arange(LANE)[:, None]
    col = np.arange(SSD_HEADS * width)[None, :]
    return jnp.asarray(((r % SSD_HEADS) == (col // width)) & (r < 3 * SSD_HEADS), dtype=bf16)


def _tile(m, pref):
    t = min(pref, m)
    assert m % t == 0
    return t


def kernel(x, meta_tokens, attn_norm, w_in, q_norm, w_uq, kv_norm, w_ukv, conv_w, conv_b, dt_bias, a_log, d_skip,
           ssd_norm, w_out, ffn_norm, w_query, sub_keys, u_experts, v_experts, final_norm):
    bsz, seq, d = x.shape
    assert d == D_MODEL and seq % CHUNK == 0
    n_tok = bsz * seq
    x2d = x.reshape(n_tok, d)

    w_in_b = _prep_w_in(w_in)
    wq_b = _prep_w_uq(w_uq[0])
    wkv = w_ukv[0].reshape(KV_LORA, MLA_HEADS, QK_NOPE + V_HEAD)
    wk_b = wkv[..., :QK_NOPE].reshape(KV_LORA, MLA_HEADS * QK_NOPE).astype(bf16)
    wv_b = wkv[..., QK_NOPE:].reshape(KV_LORA, MLA_HEADS * V_HEAD).astype(bf16)
    cwx, cwbc = conv_w[0][:, :D_INNER], conv_w[0][:, D_INNER:]
    cbx, cbbc = conv_b[0][None, :D_INNER], conv_b[0][None, D_INNER:]
    rep3 = lambda v, fill: jnp.concatenate([v, v, v, jnp.full((LANE - 3 * SSD_HEADS,), fill, f32)])[None, :]
    dtb = rep3(dt_bias[0], 0.0)
    alog = rep3(a_log[0], 0.0)
    dskip = jnp.repeat(d_skip[0], SSD_HEAD_DIM)[None, :]
    e64 = _expand_matrix(SSD_HEAD_DIM)
    e128 = _expand_matrix(CHUNK)

    lead_rows = jnp.concatenate([jnp.zeros((LEAD_PAD, d), x.dtype), meta_tokens.astype(x.dtype)], axis=0)
    cc_l, ss_l = _rope_tables(jnp.arange(LEAD, dtype=f32) - LEAD_PAD)
    cc_r, ss_r = _rope_tables(jnp.arange(seq, dtype=f32) + N_META)

    g_attn = attn_norm[0][None, :]
    proj_l = _inproj(lead_rows, g_attn, w_in_b, tm=LEAD)
    proj_r = _inproj(x2d, g_attn, w_in_b, tm=_tile(n_tok, 512))

    qn, kvn = q_norm[0][None, :], kv_norm[0][None, :]
    _, k_l, v_l = _mla_up(proj_l, cc_l, ss_l, qn, kvn, wq_b, wk_b, wv_b, tm=LEAD, n_pos_blocks=1)
    tmu = _tile(seq, 256)
    q_r, k_r, v_r = _mla_up(proj_r, cc_r, ss_r, qn, kvn, wq_b, wk_b, wv_b, tm=tmu, n_pos_blocks=seq // tmu)
    a_out, (w_out_b, w_query_b, u_b) = _attention(
        q_r, k_r, v_r, k_l, v_l, (w_out[0], w_query[0], u_experts[0]), bsz=bsz, seq=seq, tq=_tile(seq, 1024))

    s_out = _ssd(proj_l, proj_r, cwx, cwbc, cbx, cbbc, dtb, alog, dskip, ssd_norm[0][None, :], e64, e128,
                 bsz=bsz, seq=seq)

    h1, (v_b,) = _outproj(a_out, s_out, w_out_b, x2d, (v_experts[0],), tm=_tile(n_tok, 1024))

    xnt, e1, thr, e2b = _peer_q(h1, ffn_norm[0][None, :], w_query_b, sub_keys[0], tm=_tile(n_tok, 256))
    out = _peer_dense(xnt, u_b, v_b, e1, thr, e2b, h1, final_norm[None, :], tm=_tile(n_tok, 512), te=512)
    return out.reshape(bsz, seq, d)
```

```python
import functools

import jax
import jax.numpy as jnp
import numpy as np
from jax import lax
from jax.experimental import pallas as pl
from jax.experimental.pallas import tpu as pltpu

f32 = jnp.float32
bf16 = jnp.bfloat16

D_MODEL = 4096
N_META = 16
CHUNK = 128
LEAD = CHUNK
LEAD_PAD = CHUNK - N_META
MLA_HEADS = 16
QK_NOPE = 128
QK_ROPE = 64
V_HEAD = 128
Q_LORA = 1024
KV_LORA = 512
ROPE_THETA = 10000.0
D_INNER = 2048
SSD_HEAD_DIM = 64
SSD_HEADS = 32
SSD_GROUPS = 4
SSD_STATE = 128
CONV_W = 4
PEER_HEADS = 8
PEER_TOPK = 16
N_KEYS = 128
HALF_KEY = 128
EPS = 1e-6
NEG_INF = -1e30

LANE = 128
QK_SLAB = 2 * LANE
GROUP_W = D_INNER // SSD_GROUPS

P_Z = 0
P_XS = 2048
P_CQ = 4096
P_B = 5120
P_C = 5632
P_CKV = 6144
P_KR = 6656
P_DT = 6784
NP = 6912

VMEM_LIMIT = 56 * 1024 * 1024


def _cparams(sem, vmem=VMEM_LIMIT, flags=None):
    return pltpu.CompilerParams(dimension_semantics=sem, vmem_limit_bytes=vmem, flags=flags)


def _dot(a, b):
    return jnp.dot(a, b, preferred_element_type=f32)


def _dot_nt(a, b):
    return lax.dot_general(a, b, (((1,), (1,)), ((), ())), preferred_element_type=f32)


def _rms(x, g):
    return x * lax.rsqrt(jnp.mean(x * x, axis=-1, keepdims=True) + EPS) * g


def _cast_specs(to_cast, n_steps, step_index):
    ins, outs, shapes = [], [], []
    for w in to_cast:
        rows, cols = w.shape
        assert rows % (16 * n_steps) == 0
        spec = pl.BlockSpec((rows // n_steps, cols), step_index)
        ins.append(spec)
        outs.append(spec)
        shapes.append(jax.ShapeDtypeStruct((rows, cols), bf16))
    return ins, outs, shapes


def _cast_slabs(src_refs, dst_refs):
    for src, dst in zip(src_refs, dst_refs):
        sub = min(32, src.shape[0])

        def cast_rows(r, c, src=src, dst=dst, sub=sub):
            rows = pl.ds(pl.multiple_of(r * sub, sub), sub)
            dst[rows, :] = src[rows, :].astype(bf16)
            return c
        lax.fori_loop(0, src.shape[0] // sub, cast_rows, 0)


def _inproj_kernel(x_ref, g_ref, w_ref, o_ref, xn_ref, *, sub):
    @pl.when(pl.program_id(1) == 0)
    def _():
        def body(r, c):
            rows = pl.ds(pl.multiple_of(r * sub, sub), sub)
            xn_ref[rows, :] = _rms(x_ref[rows, :], g_ref[...]).astype(bf16)
            return c
        lax.fori_loop(0, x_ref.shape[0] // sub, body, 0)

    o_ref[...] = _dot(xn_ref[...], w_ref[...])


def _inproj(x2d, g, w, *, tm, tn=768):
    m = x2d.shape[0]
    assert m % tm == 0 and NP % tn == 0
    sub = min(64, tm)
    return pl.pallas_call(
        functools.partial(_inproj_kernel, sub=sub),
        grid=(m // tm, NP // tn),
        in_specs=[pl.BlockSpec((tm, D_MODEL), lambda i, j: (i, 0)),
                  pl.BlockSpec((1, D_MODEL), lambda i, j: (0, 0)),
                  pl.BlockSpec((D_MODEL, tn), lambda i, j: (0, j))],
        out_specs=pl.BlockSpec((tm, tn), lambda i, j: (i, j)),
        out_shape=jax.ShapeDtypeStruct((m, NP), f32),
        scratch_shapes=[pltpu.VMEM((tm, D_MODEL), bf16)],
        compiler_params=_cparams(("parallel", "arbitrary")),
        name="inproj",
    )(x2d, g, w)


def _mla_up_kernel(cq_ref, ckv_ref, kr_ref, cc_ref, ss_ref, qn_ref, kvn_ref, wq_ref, wk_ref, wv_ref,
                   q_ref, k_ref, v_ref, *, scale):
    cc = cc_ref[...]
    ss = ss_ref[...]

    def rope(t):
        return t * cc + pltpu.roll(t, 64, 1) * ss

    cqn = _rms(cq_ref[...], qn_ref[...]).astype(bf16)
    q = _dot(cqn, wq_ref[...])
    for h in range(MLA_HEADS):
        lo = h * QK_SLAB
        q_ref[:, lo:lo + LANE] = (q[:, lo:lo + LANE] * scale).astype(bf16)
        q_ref[:, lo + LANE:lo + QK_SLAB] = (rope(q[:, lo + LANE:lo + QK_SLAB]) * scale).astype(bf16)

    kvn = _rms(ckv_ref[...], kvn_ref[...]).astype(bf16)
    kn = _dot(kvn, wk_ref[...])
    kpe = rope(kr_ref[...]).astype(bf16)
    for h in range(MLA_HEADS):
        lo = h * QK_SLAB
        k_ref[:, lo:lo + LANE] = kn[:, h * LANE:(h + 1) * LANE].astype(bf16)
        k_ref[:, lo + LANE:lo + QK_SLAB] = kpe
    v_ref[...] = _dot(kvn, wv_ref[...]).astype(bf16)


def _mla_up(proj, cc, ss, qn, kvn, wq, wk, wv, *, tm, n_pos_blocks):
    m = proj.shape[0]
    assert m % tm == 0
    scale = float((QK_NOPE + QK_ROPE) ** -0.5 * np.log2(np.e))
    hq = MLA_HEADS * QK_SLAB
    hv = MLA_HEADS * V_HEAD
    const = lambda i: (0, 0)
    return pl.pallas_call(
        functools.partial(_mla_up_kernel, scale=scale),
        grid=(m // tm,),
        in_specs=[pl.BlockSpec((tm, Q_LORA), lambda i: (i, P_CQ // Q_LORA)),
                  pl.BlockSpec((tm, KV_LORA), lambda i: (i, P_CKV // KV_LORA)),
                  pl.BlockSpec((tm, LANE), lambda i: (i, P_KR // LANE)),
                  pl.BlockSpec((tm, LANE), lambda i: (i % n_pos_blocks, 0)),
                  pl.BlockSpec((tm, LANE), lambda i: (i % n_pos_blocks, 0)),
                  pl.BlockSpec((1, Q_LORA), const),
                  pl.BlockSpec((1, KV_LORA), const),
                  pl.BlockSpec((Q_LORA, hq), const),
                  pl.BlockSpec((KV_LORA, hv), const),
                  pl.BlockSpec((KV_LORA, hv), const)],
        out_specs=[pl.BlockSpec((tm, hq), lambda i: (i, 0)),
                   pl.BlockSpec((tm, hq), lambda i: (i, 0)),
                   pl.BlockSpec((tm, hv), lambda i: (i, 0))],
        out_shape=[jax.ShapeDtypeStruct((m, hq), bf16),
                   jax.ShapeDtypeStruct((m, hq), bf16),
                   jax.ShapeDtypeStruct((m, hv), bf16)],
        compiler_params=_cparams(("parallel",)),
        name="mla_up",
    )(proj, proj, proj, cc, ss, qn, kvn, wq, wk, wv)


def _attn_kernel(q_ref, k_ref, v_ref, kl_ref, vl_ref, *rest, tq, nh, n_cast):
    src_refs, o_ref, dst_refs = rest[:n_cast], rest[n_cast], rest[n_cast + 1:]
    _cast_slabs(src_refs, dst_refs)

    i = pl.program_id(2)
    qs = [q_ref[:, h * QK_SLAB:(h + 1) * QK_SLAB] for h in range(nh)]
    ksl = lambda h: slice(h * QK_SLAB, (h + 1) * QK_SLAB)
    vsl = lambda h: slice(h * V_HEAD, (h + 1) * V_HEAD)

    diag = pl.ds(pl.multiple_of(i * tq, tq), tq)
    col = lax.broadcasted_iota(jnp.int32, (tq, LEAD), 1)
    r_id = lax.broadcasted_iota(jnp.int32, (tq, tq), 0)
    c_id = lax.broadcasted_iota(jnp.int32, (tq, tq), 1)
    carry = []
    for h in range(nh):
        s_l = jnp.where(col >= LEAD_PAD, _dot_nt(qs[h], kl_ref[:, ksl(h)]), NEG_INF)
        s_d = jnp.where(c_id <= r_id, _dot_nt(qs[h], k_ref[diag, ksl(h)]), NEG_INF)
        m = jnp.maximum(jnp.max(s_l, axis=-1, keepdims=True), jnp.max(s_d, axis=-1, keepdims=True))
        p_l = jnp.exp2(s_l - m)
        p_d = jnp.exp2(s_d - m)
        l = jnp.sum(p_l, axis=-1, keepdims=True) + jnp.sum(p_d, axis=-1, keepdims=True)
        acc = _dot(p_l.astype(bf16), vl_ref[:, vsl(h)]) + _dot(p_d.astype(bf16), v_ref[diag, vsl(h)])
        carry += [m, l, acc]

    def update(s, vb, m, l, acc):
        m_new = jnp.maximum(m, jnp.max(s, axis=-1, keepdims=True))
        alpha = jnp.exp2(m - m_new)
        p = jnp.exp2(s - m_new)
        l = alpha * l + jnp.sum(p, axis=-1, keepdims=True)
        acc = alpha * acc + _dot(p.astype(bf16), vb)
        return [m_new, l, acc]

    def body(j, carry):
        rows = pl.ds(pl.multiple_of(j * tq, tq), tq)
        out = []
        for h in range(nh):
            out += update(_dot_nt(qs[h], k_ref[rows, ksl(h)]), v_ref[rows, vsl(h)], *carry[3 * h:3 * h + 3])
        return tuple(out)

    carry = lax.fori_loop(0, i, body, tuple(carry))
    for h in range(nh):
        o_ref[:, vsl(h)] = (carry[3 * h + 2] / carry[3 * h + 1]).astype(bf16)


def _attention(q, k, v, kl, vl, to_cast, *, bsz, seq, tq, nh=2):
    assert seq % tq == 0 and MLA_HEADS % nh == 0
    nq = seq // tq
    nhg = MLA_HEADS // nh
    n_steps = bsz * nhg * nq
    kw, vw = nh * QK_SLAB, nh * V_HEAD
    cast_in, cast_out, cast_shape = _cast_specs(to_cast, n_steps, lambda b, h, i: ((b * nhg + h) * nq + i, 0))
    outs = pl.pallas_call(
        functools.partial(_attn_kernel, tq=tq, nh=nh, n_cast=len(to_cast)),
        grid=(bsz, nhg, nq),
        in_specs=[pl.BlockSpec((tq, kw), lambda b, h, i: (b * nq + i, h)),
                  pl.BlockSpec((seq, kw), lambda b, h, i: (b, h)),
                  pl.BlockSpec((seq, vw), lambda b, h, i: (b, h)),
                  pl.BlockSpec((LEAD, kw), lambda b, h, i: (0, h)),
                  pl.BlockSpec((LEAD, vw), lambda b, h, i: (0, h))] + cast_in,
        out_specs=[pl.BlockSpec((tq, vw), lambda b, h, i: (b * nq + i, h))] + cast_out,
        out_shape=[jax.ShapeDtypeStruct((bsz * seq, MLA_HEADS * V_HEAD), bf16)] + cast_shape,
        compiler_params=_cparams(("parallel", "parallel", "arbitrary")),
        name="mla_attn",
    )(q, k, v, kl, vl, *to_cast)
    return outs[0], outs[1:]


def _split3(v):
    b0 = v.astype(bf16).astype(f32)
    r1 = v - b0
    b1 = r1.astype(bf16).astype(f32)
    b2 = (r1 - b1).astype(bf16).astype(f32)
    return b0, b1, b2


def _ssd_kernel(zx_l, bc_l, dt_l, zx_r, bc_r, dt_r, cwx, cwbc, cbx, cbbc, dtb, alog, dskip, ng, e64, e128,
                o_ref, st_ref, px_ref, pbc_ref):
    c = pl.program_id(1)
    is_lead = c == 0

    @pl.when(is_lead)
    def _():
        st_ref[...] = jnp.zeros_like(st_ref)
        px_ref[...] = jnp.zeros_like(px_ref)
        pbc_ref[...] = jnp.zeros_like(pbc_ref)

    zx = jnp.where(is_lead, zx_l[...], zx_r[...])
    bcraw = jnp.where(is_lead, bc_l[...], bc_r[...])
    dtraw = jnp.where(is_lead, dt_l[...], dt_r[...])
    z = zx[:, :D_INNER]
    xraw = zx[:, D_INNER:]

    row1 = lax.broadcasted_iota(jnp.int32, (CHUNK, 1), 0)
    r_id = lax.broadcasted_iota(jnp.int32, (CHUNK, CHUNK), 0)
    c_id = lax.broadcasted_iota(jnp.int32, (CHUNK, CHUNK), 1)
    tril = r_id >= c_id

    def conv_silu(cur, prev_ref, w_ref, b_ref):
        prev = prev_ref[...]
        acc = cur * w_ref[CONV_W - 1:CONV_W, :] + b_ref[...]
        for k in range(1, CONV_W):
            comb = jnp.where(row1 >= CHUNK - k, prev, cur)
            acc = acc + pltpu.roll(comb, k, 0) * w_ref[CONV_W - 1 - k:CONV_W - k, :]
        prev_ref[...] = cur
        return acc * jax.nn.sigmoid(acc)

    xs = conv_silu(xraw, px_ref, cwx, cbx)
    bcc = conv_silu(bcraw, pbc_ref, cwbc, cbbc)
    b_all = bcc[:, :SSD_GROUPS * SSD_STATE]
    c_all = bcc[:, SSD_GROUPS * SSD_STATE:].astype(bf16)

    pre = dtraw + dtb[...]
    dt = jnp.maximum(pre, 0.0) + jnp.log1p(jnp.exp(-jnp.abs(pre)))
    dt = jnp.where(jnp.logical_and(is_lead, row1 < LEAD_PAD), 0.0, dt)
    da = dt * (-jnp.exp(alog[...]))
    tri = tril.astype(f32).astype(bf16)
    d0, d1, d2 = _split3(da)
    cs = _dot(tri, d0.astype(bf16)) + _dot(tri, d1.astype(bf16)) + _dot(tri, d2.astype(bf16))
    cs_t = cs.T
    ecs = jnp.exp(cs)
    dec = jnp.exp(cs[CHUNK - 1:CHUNK, :] - cs)

    def kcat(v):
        b0, b1, b2 = _split3(v)
        return jnp.where(c_id < 32, b0, jnp.where(c_id < 64, b1, jnp.where(c_id < 96, b2, 0.0))).astype(bf16)

    ex = _dot(jnp.concatenate([kcat(dt), kcat(ecs), kcat(dec)], axis=0), e64[...])
    dt_x = ex[0:CHUNK]
    ecs_x = ex[CHUNK:2 * CHUNK]
    dec_x = ex[2 * CHUNK:3 * CHUNK]
    cs_bc = _dot(kcat(cs), e128[...])

    xdt = xs * dt_x
    xdt_b = xdt.astype(bf16)
    xd_b = (xdt * dec_x).astype(bf16)

    ys = []
    for g in range(SSD_GROUPS):
        gl = slice(g * GROUP_W, (g + 1) * GROUP_W)
        b_g = b_all[:, g * SSD_STATE:(g + 1) * SSD_STATE]
        c_g = c_all[:, g * SSD_STATE:(g + 1) * SSD_STATE]
        cb = _dot_nt(c_g, b_g.astype(bf16))
        st_g = st_ref[:, gl]
        y_g = _dot(c_g, st_g.astype(bf16)) * ecs_x[:, gl]
        pieces = []
        for pr in range(GROUP_W // LANE):
            h0 = g * (GROUP_W // SSD_HEAD_DIM) + 2 * pr
            xpair = xdt_b[:, h0 * SSD_HEAD_DIM:(h0 + 2) * SSD_HEAD_DIM]
            yh = []
            for h in (h0, h0 + 1):
                seg = cs_bc[:, h * CHUNK:(h + 1) * CHUNK] - cs_t[h:h + 1, :]
                lm = jnp.where(tril, jnp.exp(jnp.minimum(seg, 0.0)), 0.0)
                yh.append(_dot((cb * lm).astype(bf16), xpair))
            pieces.append(jnp.where(c_id < SSD_HEAD_DIM, yh[0], yh[1]))
        y_g = y_g + jnp.concatenate(pieces, axis=1)
        st_ref[:, gl] = st_g * ecs_x[CHUNK - 1:CHUNK, gl] + _dot(b_g.T.astype(bf16), xd_b[:, gl])
        ys.append(y_g)

    y = jnp.concatenate(ys, axis=1) + xs * dskip[...]
    gated = y * (z * jax.nn.sigmoid(z))
    outs = []
    for g in range(SSD_GROUPS):
        gg = gated[:, g * GROUP_W:(g + 1) * GROUP_W]
        outs.append(gg * lax.rsqrt(jnp.mean(gg * gg, axis=-1, keepdims=True) + EPS))
    o_ref[...] = (jnp.concatenate(outs, axis=1) * ng[...]).astype(bf16)


def _ssd(proj_l, proj_r, cwx, cwbc, cbx, cbbc, dtb, alog, dskip, ng, e64, e128, *, bsz, seq):
    nch = seq // CHUNK
    real = lambda w: (lambda b, c: (b * nch + jnp.maximum(c - 1, 0), w))
    lead = lambda w: (lambda b, c: (0, w))
    const = lambda b, c: (0, 0)
    zx_w, bc_w, dt_w = 2 * D_INNER, 2 * SSD_GROUPS * SSD_STATE, LANE
    return pl.pallas_call(
        _ssd_kernel,
        grid=(bsz, nch + 1),
        in_specs=[pl.BlockSpec((CHUNK, zx_w), lead(P_Z // zx_w)),
                  pl.BlockSpec((CHUNK, bc_w), lead(P_B // bc_w)),
                  pl.BlockSpec((CHUNK, dt_w), lead(P_DT // dt_w)),
                  pl.BlockSpec((CHUNK, zx_w), real(P_Z // zx_w)),
                  pl.BlockSpec((CHUNK, bc_w), real(P_B // bc_w)),
                  pl.BlockSpec((CHUNK, dt_w), real(P_DT // dt_w)),
                  pl.BlockSpec((CONV_W, D_INNER), const),
                  pl.BlockSpec((CONV_W, bc_w), const),
                  pl.BlockSpec((1, D_INNER), const),
                  pl.BlockSpec((1, bc_w), const),
                  pl.BlockSpec((1, LANE), const),
                  pl.BlockSpec((1, LANE), const),
                  pl.BlockSpec((1, D_INNER), const),
                  pl.BlockSpec((1, D_INNER), const),
                  pl.BlockSpec((LANE, D_INNER), const),
                  pl.BlockSpec((LANE, SSD_HEADS * CHUNK), const)],
        out_specs=pl.BlockSpec((CHUNK, D_INNER), real(0)),
        out_shape=jax.ShapeDtypeStruct((bsz * seq, D_INNER), bf16),
        scratch_shapes=[pltpu.VMEM((SSD_STATE, D_INNER), f32),
                        pltpu.VMEM((CHUNK, D_INNER), f32),
                        pltpu.VMEM((CHUNK, bc_w), f32)],
        compiler_params=_cparams(("parallel", "arbitrary")),
        name="ssd",
    )(proj_l, proj_l, proj_l, proj_r, proj_r, proj_r, cwx, cwbc, cbx, cbbc, dtb, alog, dskip, ng, e64, e128)


def _outproj_kernel(a_ref, s_ref, wa_ref, ws_ref, x_ref, *rest, n_cast):
    src_refs, o_ref, dst_refs = rest[:n_cast], rest[n_cast], rest[n_cast + 1:]
    _cast_slabs(src_refs, dst_refs)
    o_ref[...] = _dot(a_ref[...], wa_ref[...]) + _dot(s_ref[...], ws_ref[...]) + x_ref[...]


def _outproj(a, s, w, x2d, to_cast, *, tm, tn=512):
    m = a.shape[0]
    ka = a.shape[1]
    assert m % tm == 0 and D_MODEL % tn == 0
    nj = D_MODEL // tn
    cast_in, cast_out, cast_shape = _cast_specs(to_cast, (m // tm) * nj, lambda i, j: (i * nj + j, 0))
    outs = pl.pallas_call(
        functools.partial(_outproj_kernel, n_cast=len(to_cast)),
        grid=(m // tm, nj),
        in_specs=[pl.BlockSpec((tm, ka), lambda i, j: (i, 0)),
                  pl.BlockSpec((tm, D_INNER), lambda i, j: (i, 0)),
                  pl.BlockSpec((ka, tn), lambda i, j: (0, j)),
                  pl.BlockSpec((D_INNER, tn), lambda i, j: (1, j)),
                  pl.BlockSpec((tm, tn), lambda i, j: (i, j))] + cast_in,
        out_specs=[pl.BlockSpec((tm, tn), lambda i, j: (i, j))] + cast_out,
        out_shape=[jax.ShapeDtypeStruct((m, D_MODEL), f32)] + cast_shape,
        compiler_params=_cparams(("parallel", "arbitrary")),
        name="outproj",
    )(a, s, w, w, x2d, *to_cast)
    return outs[0], outs[1:]


SUBLANES = 8


def _oddeven_mergesort_pairs(n):
    pairs = []
    p = 1
    while p < n:
        k = p
        while k >= 1:
            for j in range(k % p, n - k, 2 * k):
                for i in range(min(k, n - j - k)):
                    if (i + j) // (2 * p) == (i + j + k) // (2 * p):
                        pairs.append((i + j, i + j + k))
            k //= 2
        p *= 2
    return pairs


def _sorted_top17(x):
    n_grp = x.shape[0] // SUBLANES
    assert n_grp == 16
    r = [x[g * SUBLANES:(g + 1) * SUBLANES, :] for g in range(n_grp)]

    def cmpx(a, b):
        r[a], r[b] = jnp.maximum(r[a], r[b]), jnp.minimum(r[a], r[b])

    for a, b in _oddeven_mergesort_pairs(n_grp):
        cmpx(a, b)
    for shift in (4, 2, 1):
        other = [pltpu.roll(v, shift, 0) for v in r]
        r = [jnp.maximum(r[i], other[n_grp - 1 - i]) for i in range(n_grp)]
        k = n_grp // 2
        while k >= 1:
            for i in range(n_grp):
                if i & k == 0:
                    cmpx(i, i + k)
            k //= 2
    below = [jnp.where(x[g * SUBLANES:(g + 1) * SUBLANES, :] < r[n_grp - 1], x[g * SUBLANES:(g + 1) * SUBLANES, :],
                       -jnp.inf) for g in range(n_grp)]
    while len(below) > 1:
        below = [jnp.maximum(below[i], below[i + 1]) for i in range(0, len(below), 2)]
    return [v[0:1, :] for v in r] + [jnp.max(below[0], axis=0, keepdims=True)]


def _stack_rows(vals):
    n = len(vals)
    rid = lax.broadcasted_iota(jnp.int32, (n, vals[0].shape[1]), 0)
    out = jnp.zeros((n, vals[0].shape[1]), f32)
    for r, v in enumerate(vals):
        out = jnp.where(rid == r, v, out)
    return out


N_CAND = PEER_TOPK + (PEER_TOPK // 2 - 1) * (PEER_TOPK // 2) + PEER_TOPK


def _peer_q_kernel(h_ref, g_ref, wq_ref, sk_ref, xnt_ref, e1_ref, thr_ref, e2_ref, cand_ref):
    xn = _rms(h_ref[...], g_ref[...])
    xnt_ref[...] = xn.T.astype(bf16)
    q = _dot(xn.astype(bf16), wq_ref[...]).astype(bf16)
    sk1 = sk_ref[0].astype(bf16)
    sk2 = sk_ref[1].astype(bf16)
    k = PEER_TOPK
    half = k // 2
    for h in range(PEER_HEADS):
        lo = h * 2 * HALF_KEY
        s1 = _dot_nt(sk1, q[:, lo:lo + HALF_KEY])
        s2 = _dot_nt(sk2, q[:, lo + HALF_KEY:lo + 2 * HALF_KEY])
        v1 = _sorted_top17(s1)
        v2 = _sorted_top17(s2)
        v2_top = _stack_rows(v2[:k])
        cand_ref[0:k, :] = v1[0] + v2_top
        for a in range(1, half):
            cand_ref[k + (a - 1) * half:k + a * half, :] = v1[a] + v2_top[0:half, :]
        cand_ref[k + (half - 1) * half:k + half * half, :] = _stack_rows(v1[half:k]) + v2[0]
        ninf = jnp.full_like(v1[0], -jnp.inf)
        cand_ref[k + half * half:N_CAND, :] = _stack_rows([v1[0] + v2[k], v1[k] + v2[0]] + [ninf] * (half - 2))
        cand_ref[N_CAND:, :] = jnp.full((N_KEYS - N_CAND, s1.shape[1]), -jnp.inf, f32)
        top = _sorted_top17(cand_ref[...])
        zsum = jnp.zeros_like(top[0])
        for t in top[:k]:
            zsum = zsum + jnp.exp(t - top[0])
        tau = 0.5 * (top[k - 1] + top[k])
        scale = 0.5 / zsum
        e1_ref[h] = jnp.exp(s1 - v1[0])
        thr_ref[h] = jnp.exp((tau - v2[0]) - s1) * scale
        e2_ref[h] = pltpu.bitcast((jnp.exp(s2 - v2[0]) * scale).astype(bf16), jnp.uint32)


def _peer_q(h1, g, wq, sub_keys, *, tm):
    n = h1.shape[0]
    assert n % tm == 0
    nq = PEER_HEADS * 2 * HALF_KEY
    per_head = lambda rows, dt: (pl.BlockSpec((PEER_HEADS, rows, tm), lambda i: (0, 0, i)),
                                 jax.ShapeDtypeStruct((PEER_HEADS, rows, n), dt))
    specs, shapes = zip((pl.BlockSpec((D_MODEL, tm), lambda i: (0, i)), jax.ShapeDtypeStruct((D_MODEL, n), bf16)),
                        per_head(N_KEYS, f32), per_head(N_KEYS, f32), per_head(N_KEYS // 2, jnp.uint32))
    return pl.pallas_call(
        _peer_q_kernel,
        grid=(n // tm,),
        in_specs=[pl.BlockSpec((tm, D_MODEL), lambda i: (i, 0)),
                  pl.BlockSpec((1, D_MODEL), lambda i: (0, 0)),
                  pl.BlockSpec((D_MODEL, nq), lambda i: (0, 0), pipeline_mode=pl.Buffered(1)),
                  pl.BlockSpec((2, N_KEYS, HALF_KEY), lambda i: (0, 0, 0))],
        out_specs=list(specs),
        out_shape=list(shapes),
        scratch_shapes=[pltpu.VMEM((N_KEYS, tm), f32)],
        compiler_params=_cparams(("parallel",)),
        name="peer_query",
    )(h1, g, wq, sub_keys)


def _peer_dense_kernel(xnt_ref, u_ref, v_ref, e1_ref, thr_ref, e2_ref, h_ref, g_ref, o_ref, ht_ref, act_ref, *, te):
    j = pl.program_id(1)
    tm = xnt_ref.shape[1]
    pk = N_KEYS // 2

    @pl.when(j == 0)
    def _():
        o_ref[...] = jnp.zeros_like(o_ref)

    ht_ref[...] = _dot(u_ref[...], xnt_ref[...])
    for ii in range(te // N_KEYS):
        i1 = j * (te // N_KEYS) + ii
        e1rows = [e1_ref[h, pl.ds(i1, 1), :] for h in range(PEER_HEADS)]
        thrrows = [thr_ref[h, pl.ds(i1, 1), :] for h in range(PEER_HEADS)]
        for c in range(tm // LANE):
            cl = slice(c * LANE, (c + 1) * LANE)
            w = jnp.zeros((N_KEYS, LANE), bf16)
            for h in range(PEER_HEADS):
                e1b = jnp.broadcast_to(e1rows[h][:, cl], (N_KEYS, LANE)).astype(bf16)
                thrb = jnp.broadcast_to(thrrows[h][:, cl], (N_KEYS, LANE)).astype(bf16)
                e2 = pltpu.bitcast(e2_ref[h, :, cl], bf16)
                w = w + jnp.where(e2 >= thrb, e2, jnp.zeros_like(e2)) * e1b
            hh = ht_ref[ii * N_KEYS:(ii + 1) * N_KEYS, cl]
            g = hh * (1.0 + lax.erf(hh * np.float32(1.0 / np.sqrt(2.0))))
            act_ref[ii * pk:(ii + 1) * pk, cl] = pltpu.bitcast(g.astype(bf16) * w, jnp.uint32)
    o_ref[...] += _dot(pltpu.bitcast(act_ref[...], bf16).T, v_ref[...])

    hr = h_ref.shape[0]
    slab = pl.ds(pl.multiple_of(j * hr, hr), hr)
    o_ref[slab, :] += h_ref[...]

    @pl.when(j == pl.num_programs(1) - 1)
    def _():
        sub = 64

        def body(r, c):
            rows = pl.ds(pl.multiple_of(r * sub, sub), sub)
            o_ref[rows, :] = _rms(o_ref[rows, :], g_ref[...])
            return c
        lax.fori_loop(0, tm // sub, body, 0)


def _peer_dense(xnt, u_b, v_b, e1, thr, e2b, h1, g, *, tm, te):
    n = xnt.shape[1]
    n_exp = u_b.shape[0]
    nj = n_exp // te
    assert n % tm == 0 and n_exp % te == 0 and te % N_KEYS == 0 and tm % LANE == 0 and tm % (8 * nj) == 0
    hr = tm // nj
    return pl.pallas_call(
        functools.partial(_peer_dense_kernel, te=te),
        grid=(n // tm, nj),
        in_specs=[pl.BlockSpec((D_MODEL, tm), lambda i, j: (0, i)),
                  pl.BlockSpec((te, D_MODEL), lambda i, j: (j, 0)),
                  pl.BlockSpec((te, D_MODEL), lambda i, j: (j, 0)),
                  pl.BlockSpec((PEER_HEADS, N_KEYS, tm), lambda i, j: (0, 0, i)),
                  pl.BlockSpec((PEER_HEADS, N_KEYS, tm), lambda i, j: (0, 0, i)),
                  pl.BlockSpec((PEER_HEADS, N_KEYS // 2, tm), lambda i, j: (0, 0, i)),
                  pl.BlockSpec((hr, D_MODEL), lambda i, j: (i * nj + j, 0)),
                  pl.BlockSpec((1, D_MODEL), lambda i, j: (0, 0))],
        out_specs=pl.BlockSpec((tm, D_MODEL), lambda i, j: (i, 0)),
        out_shape=jax.ShapeDtypeStruct((n, D_MODEL), f32),
        scratch_shapes=[pltpu.VMEM((te, tm), f32),
                        pltpu.VMEM((te // 2, tm), jnp.uint32)],
        compiler_params=_cparams(("parallel", "arbitrary")),
        name="peer_dense",
    )(xnt, u_b, v_b, e1, thr, e2b, h1, g)


_W_IN_SRC = ((1600, 3648), (3648, 5696), (0, 1024), (5696, 6208), (6208, 6720), (1024, 1536),
             (1536, 1600), (1568, 1600), (1536, 1568),
             (6720, 6752), (6720, 6752), (6720, 6752))


def _prep_w_in_kernel(w_ref, o_ref):
    col = 0
    tail = []
    for lo, hi in _W_IN_SRC:
        piece = w_ref[:, lo:hi].astype(f32)
        if (hi - lo) % LANE == 0:
            o_ref[:, col:col + hi - lo] = piece.astype(bf16)
            col += hi - lo
        else:
            tail.append(piece)
    rows = w_ref.shape[0]
    tail.append(jnp.zeros((rows, NP - col - sum(t.shape[1] for t in tail)), f32))
    o_ref[:, col:] = jnp.concatenate(tail, axis=1).astype(bf16)


def _prep_w_in(w, *, tr=256):
    _, k, n_in = w.shape
    assert k % tr == 0
    return pl.pallas_call(
        _prep_w_in_kernel,
        grid=(k // tr,),
        in_specs=[pl.BlockSpec((None, tr, n_in), lambda i: (0, i, 0))],
        out_specs=pl.BlockSpec((tr, NP), lambda i: (i, 0)),
        out_shape=jax.ShapeDtypeStruct((k, NP), bf16),
        compiler_params=_cparams(("parallel",)),
        name="prep_w_in",
    )(w)


def _prep_w_uq(w):
    w = w.reshape(Q_LORA, MLA_HEADS, QK_NOPE + QK_ROPE)
    nope, x1, x2 = w[..., :QK_NOPE], w[..., QK_NOPE:QK_NOPE + 32], w[..., QK_NOPE + 32:]
    return jnp.concatenate([nope, x1, x2, x2, x1], axis=-1).reshape(Q_LORA, MLA_HEADS * QK_SLAB).astype(bf16)


def _rope_tables(pos):
    inv_freq = ROPE_THETA ** (-jnp.arange(0, QK_ROPE, 2, dtype=f32) / QK_ROPE)
    ang = pos[:, None] * inv_freq[None, :]
    c, s = jnp.cos(ang), jnp.sin(ang)
    zero = jnp.zeros((pos.shape[0], LANE - QK_ROPE), f32)
    return jnp.concatenate([c, c, zero], axis=1), jnp.concatenate([-s, s, zero], axis=1)


def _expand_matrix(width):
    r = np.arange(LANE)[:, None]
    col = np.arange(SSD_HEADS * width)[None, :]
    return jnp.asarray(((r % SSD_HEADS) == (col // width)) & (r < 3 * SSD_HEADS), dtype=bf16)


def _tile(m, pref):
    t = min(pref, m)
    assert m % t == 0
    return t


def kernel(x, meta_tokens, attn_norm, w_in, q_norm, w_uq, kv_norm, w_ukv, conv_w, conv_b, dt_bias, a_log, d_skip,
           ssd_norm, w_out, ffn_norm, w_query, sub_keys, u_experts, v_experts, final_norm):
    bsz, seq, d = x.shape
    assert d == D_MODEL and seq % CHUNK == 0
    n_tok = bsz * seq
    x2d = x.reshape(n_tok, d)

    w_in_b = _prep_w_in(w_in.astype(bf16))
    wq_b = _prep_w_uq(w_uq[0])
    wkv = w_ukv[0].reshape(KV_LORA, MLA_HEADS, QK_NOPE + V_HEAD)
    wk_b = wkv[..., :QK_NOPE].reshape(KV_LORA, MLA_HEADS * QK_NOPE).astype(bf16)
    wv_b = wkv[..., QK_NOPE:].reshape(KV_LORA, MLA_HEADS * V_HEAD).astype(bf16)
    cwx, cwbc = conv_w[0][:, :D_INNER], conv_w[0][:, D_INNER:]
    cbx, cbbc = conv_b[0][None, :D_INNER], conv_b[0][None, D_INNER:]
    rep3 = lambda v, fill: jnp.concatenate([v, v, v, jnp.full((LANE - 3 * SSD_HEADS,), fill, f32)])[None, :]
    dtb = rep3(dt_bias[0], 0.0)
    alog = rep3(a_log[0], 0.0)
    dskip = jnp.repeat(d_skip[0], SSD_HEAD_DIM)[None, :]
    e64 = _expand_matrix(SSD_HEAD_DIM)
    e128 = _expand_matrix(CHUNK)

    lead_rows = jnp.concatenate([jnp.zeros((LEAD_PAD, d), x.dtype), meta_tokens.astype(x.dtype)], axis=0)
    cc_l, ss_l = _rope_tables(jnp.arange(LEAD, dtype=f32) - LEAD_PAD)
    cc_r, ss_r = _rope_tables(jnp.arange(seq, dtype=f32) + N_META)

    g_attn = attn_norm[0][None, :]
    proj_l = _inproj(lead_rows, g_attn, w_in_b, tm=LEAD)
    proj_r = _inproj(x2d, g_attn, w_in_b, tm=_tile(n_tok, 512))

    qn, kvn = q_norm[0][None, :], kv_norm[0][None, :]
    _, k_l, v_l = _mla_up(proj_l, cc_l, ss_l, qn, kvn, wq_b, wk_b, wv_b, tm=LEAD, n_pos_blocks=1)
    tmu = _tile(seq, 256)
    q_r, k_r, v_r = _mla_up(proj_r, cc_r, ss_r, qn, kvn, wq_b, wk_b, wv_b, tm=tmu, n_pos_blocks=seq // tmu)
    a_out, (w_out_b, w_query_b, u_b) = _attention(
        q_r, k_r, v_r, k_l, v_l, (w_out[0], w_query[0], u_experts[0]), bsz=bsz, seq=seq, tq=_tile(seq, 1024))

    s_out = _ssd(proj_l, proj_r, cwx, cwbc, cbx, cbbc, dtb, alog, dskip, ssd_norm[0][None, :], e64, e128,
                 bsz=bsz, seq=seq)

    h1, (v_b,) = _outproj(a_out, s_out, w_out_b, x2d, (v_experts[0],), tm=_tile(n_tok, 1024))

    xnt, e1, thr, e2b = _peer_q(h1, ffn_norm[0][None, :], w_query_b, sub_keys[0], tm=_tile(n_tok, 256))
    out = _peer_dense(xnt, u_b, v_b, e1, thr, e2b, h1, final_norm[None, :], tm=_tile(n_tok, 512), te=512)
    return out.reshape(bsz, seq, d)
```
